```python
import jax, jax.numpy as jnp
from jax import lax
import numpy as np

D_MODEL = 1024
BATCH = 8
SEQ = 2048
DEPTH = 4

N_A_LAYERS = DEPTH // 2
N_B_LAYERS = DEPTH - N_A_LAYERS
HEAD_DIM = 64
SB_HEADS = D_MODEL // HEAD_DIM
NSA_HEADS = D_MODEL // HEAD_DIM
NSA_KV_HEADS = 4
NSA_GROUP = NSA_HEADS // NSA_KV_HEADS
N_BRANCHES = 3
ROPE_DIMS = HEAD_DIM // 4
ROPE_THETA = 500000.0
D_FF = -(-8 * D_MODEL // (3 * 256)) * 256
Q_BLOCK = 128
SEL_Q_BLOCK = 64
CMP_BLOCK = 32
CMP_STRIDE = 16
CMP_HIDDEN = 4 * HEAD_DIM
SEL_BLOCK = 64
N_SEL = 8
N_LOCAL_SEL = 2
WINDOW = 512
EPS = 1e-6
NEG = -1e30
FORCE = 1e4

kernel_name = 'hybrid_stickbreak_nsa_yoco'


def _rmsnorm(x, g):
    x32 = x.astype(jnp.float32)
    y = x32 * lax.rsqrt(jnp.mean(x32 * x32, axis=-1, keepdims=True) + EPS)
    return (y * g.astype(jnp.float32)).astype(x.dtype)


def _swiglu(x, w_in, w_out):
    a, b = jnp.split(x @ w_in, 2, axis=-1)
    return (jax.nn.silu(a) * b) @ w_out


def _rope_tables(positions):
    inv_freq = jnp.power(ROPE_THETA, -jnp.arange(0, ROPE_DIMS, 2, dtype=jnp.float32) / ROPE_DIMS)
    ang = positions.astype(jnp.float32)[:, None] * inv_freq[None, :]
    return jnp.cos(ang), jnp.sin(ang)


def _apply_rope(x, cos, sin):
    half = ROPE_DIMS // 2
    x32 = x.astype(jnp.float32)
    x1, x2 = x32[..., :half], x32[..., half:ROPE_DIMS]
    c, s = cos[:, None, :], sin[:, None, :]
    return jnp.concatenate([x1 * c - x2 * s, x2 * c + x1 * s, x32[..., ROPE_DIMS:]], axis=-1).astype(x.dtype)


def _stick_breaking_attention(q, k, v):
    B, S, H, Dh = q.shape
    scale = Dh ** -0.5
    outs = []
    for start in range(0, S, Q_BLOCK):
        end = start + Q_BLOCK
        z = jnp.einsum('bthd,bshd->bhts', q[:, start:end], k[:, :end], preferred_element_type=jnp.float32) * scale
        t_pos = start + jnp.arange(Q_BLOCK)[:, None]
        s_pos = jnp.arange(end)[None, :]
        past = s_pos < t_pos
        log_1m = jnp.where(past, jax.nn.log_sigmoid(-z), 0.0)
        later = lax.cumsum(log_1m, axis=3, reverse=True) - log_1m
        w = jnp.where(past, jnp.exp(jax.nn.log_sigmoid(z) + later), 0.0)
        outs.append(jnp.einsum('bhts,bshd->bthd', w.astype(v.dtype), v[:, :end]))
    return jnp.concatenate(outs, axis=1)


def _stick_breaking_mixer(hn, w_qkv, w_out):
    B, S, _ = hn.shape
    qkv = (hn @ w_qkv).reshape(B, S, 3, SB_HEADS, HEAD_DIM)
    o = _stick_breaking_attention(qkv[:, :, 0], qkv[:, :, 1], qkv[:, :, 2])
    return o.reshape(B, S, SB_HEADS * HEAD_DIM) @ w_out


def _compress(x, pos_emb, w1, w2):
    B, S, G, Dh = x.shape
    n_cmp = (S - CMP_BLOCK) // CMP_STRIDE + 1
    idx = np.arange(n_cmp)[:, None] * CMP_STRIDE + np.arange(CMP_BLOCK)[None, :]
    blocks = x[:, idx] + pos_emb[:, None, :]
    blocks = blocks.transpose(0, 1, 3, 2, 4).reshape(B, n_cmp, G, CMP_BLOCK * Dh)
    return jax.nn.silu(blocks @ w1) @ w2


def _nsa_shared_kv(h, kv_norm, w_kv, k_norm, pos_k, pos_v, k_w1, k_w2, v_w1, v_w2, cos, sin):
    B, S, _ = h.shape
    kv = (_rmsnorm(h, kv_norm) @ w_kv).reshape(B, S, 2 * N_BRANCHES, NSA_KV_HEADS, HEAD_DIM)
    k_c_raw, v_c_raw, k_s, v_s, k_w, v_w = [kv[:, :, i] for i in range(2 * N_BRANCHES)]
    n_cmp = (S - CMP_BLOCK) // CMP_STRIDE + 1
    cc, sc = _rope_tables(jnp.arange(n_cmp) * CMP_STRIDE + (CMP_BLOCK - 1))
    k_c = _apply_rope(_rmsnorm(_compress(k_c_raw, pos_k, k_w1, k_w2), k_norm[0]), cc, sc)
    v_c = _compress(v_c_raw, pos_v, v_w1, v_w2)
    k_s = _apply_rope(_rmsnorm(k_s, k_norm[1]), cos, sin)
    k_w = _apply_rope(_rmsnorm(k_w, k_norm[2]), cos, sin)
    return (k_c, v_c, k_s, v_s, k_w, v_w)


def _nsa_selected(q, sel_idx, k_sel, v_sel):
    B, S, G, R, Dh = q.shape
    n_k = sel_idx.shape[-1]
    L = n_k * SEL_BLOCK
    scale = Dh ** -0.5
    nblk = S // SEL_Q_BLOCK
    kT = jnp.transpose(k_sel, (0, 2, 1, 3))
    vT = jnp.transpose(v_sel, (0, 2, 1, 3))
    b_ar = jnp.arange(B)[:, None, None, None]
    g_ar = jnp.arange(G)[None, :, None, None]
    offs = jnp.arange(SEL_BLOCK)
    q_blk = q.reshape(B, nblk, SEL_Q_BLOCK, G, R, Dh).transpose(1, 0, 2, 3, 4, 5)
    i_blk = sel_idx.reshape(B, G, nblk, SEL_Q_BLOCK, n_k).transpose(2, 0, 1, 3, 4)
    starts = jnp.arange(nblk) * SEL_Q_BLOCK

    def one_block(args):
        qb, ib, t0 = args
        tok = (ib[..., None] * SEL_BLOCK + offs).reshape(B, G, SEL_Q_BLOCK, L)
        kg = kT[b_ar, g_ar, tok]
        vg = vT[b_ar, g_ar, tok]
        s = jnp.einsum('btgrd,bgtld->bgrtl', qb, kg, preferred_element_type=jnp.float32) * scale
        t_pos = t0 + jnp.arange(SEL_Q_BLOCK)
        mask = tok[:, :, None] <= t_pos[:, None]
        p = jax.nn.softmax(jnp.where(mask, s, NEG), axis=-1)
        return jnp.einsum('bgrtl,bgtld->btgrd', p.astype(vg.dtype), vg)

    out = lax.map(one_block, (q_blk, i_blk, starts))
    return out.transpose(1, 0, 2, 3, 4, 5).reshape(B, S, G, R, Dh)


def _nsa_window(q, k_win, v_win):
    B, S, G, R, Dh = q.shape
    scale = Dh ** -0.5
    nblk = S // Q_BLOCK
    span = WINDOW + Q_BLOCK
    k_pad = jnp.pad(k_win, ((0, 0), (WINDOW, 0), (0, 0), (0, 0)))
    v_pad = jnp.pad(v_win, ((0, 0), (WINDOW, 0), (0, 0), (0, 0)))
    q_blk = q.reshape(B, nblk, Q_BLOCK, G, R, Dh).transpose(1, 0, 2, 3, 4, 5)
    starts = jnp.arange(nblk) * Q_BLOCK

    def one_block(args):
        qb, t0 = args
        kb = lax.dynamic_slice_in_dim(k_pad, t0, span, axis=1)
        vb = lax.dynamic_slice_in_dim(v_pad, t0, span, axis=1)
        s = jnp.einsum('btgrd,bsgd->bgrts', qb, kb, preferred_element_type=jnp.float32) * scale
        t_pos = t0 + jnp.arange(Q_BLOCK)[:, None]
        s_pos = t0 - WINDOW + jnp.arange(span)[None, :]
        mask = (s_pos <= t_pos) & (t_pos - s_pos < WINDOW) & (s_pos >= 0)
        p = jax.nn.softmax(jnp.where(mask, s, NEG), axis=-1)
        return jnp.einsum('bgrts,bsgd->btgrd', p.astype(vb.dtype), vb)

    out = lax.map(one_block, (q_blk, starts))
    return out.transpose(1, 0, 2, 3, 4, 5).reshape(B, S, G, R, Dh)


def _nsa_mixer(hn, kvs, w_in, q_norm, w_out, cos, sin):
    k_c, v_c, k_s, v_s, k_w, v_w = kvs
    B, S, _ = hn.shape
    G, R, Dh = NSA_KV_HEADS, NSA_GROUP, HEAD_DIM
    scale = Dh ** -0.5
    proj = hn @ w_in
    q = proj[..., :NSA_HEADS * Dh].reshape(B, S, NSA_HEADS, Dh)
    q = _apply_rope(_rmsnorm(q, q_norm), cos, sin).reshape(B, S, G, R, Dh)
    gates = jax.nn.sigmoid(proj[..., NSA_HEADS * Dh:].astype(jnp.float32)).reshape(B, S, N_BRANCHES, G, R)
    t_pos = jnp.arange(S)
    n_cmp = k_c.shape[1]
    s_c = jnp.einsum('btgrd,bcgd->bgrtc', q, k_c, preferred_element_type=jnp.float32) * scale
    c_end = jnp.arange(n_cmp) * CMP_STRIDE + (CMP_BLOCK - 1)
    valid_c = c_end[None, :] <= t_pos[:, None]
    p_c = jax.nn.softmax(jnp.where(valid_c, s_c, NEG), axis=-1) * jnp.any(valid_c, axis=-1)[:, None]
    o_c = jnp.einsum('bgrtc,bcgd->btgrd', p_c.astype(v_c.dtype), v_c)
    n_sel = S // SEL_BLOCK
    c_start = np.arange(n_cmp) * CMP_STRIDE
    j_start = np.arange(n_sel) * SEL_BLOCK
    overlap = ((c_start[:, None] < j_start[None, :] + SEL_BLOCK)
               & (c_start[:, None] + CMP_BLOCK > j_start[None, :])).astype(np.float32)
    imp = jnp.einsum('bgrtc,cj->bgtj', p_c, jnp.asarray(overlap))
    j = jnp.arange(n_sel)[None, :]
    cur = (t_pos // SEL_BLOCK)[:, None]
    forced = (j == 0) | ((cur - j >= 0) & (cur - j < N_LOCAL_SEL))
    sel_score = jnp.where(forced, FORCE, jnp.where(j <= cur, imp, -FORCE))
    _, sel_idx = lax.top_k(sel_score, min(N_SEL, n_sel))
    o_s = _nsa_selected(q, sel_idx, k_s, v_s)
    o_w = _nsa_window(q, k_w, v_w)
    o = (gates[:, :, 0, :, :, None] * o_c + gates[:, :, 1, :, :, None] * o_s
         + gates[:, :, 2, :, :, None] * o_w).astype(hn.dtype)
    return o.reshape(B, S, NSA_HEADS * Dh) @ w_out


def setup_inputs(seed: int = 0) -> dict:
    key = jax.random.key(seed)
    ks = jax.random.split(key, 22)
    f32 = jnp.float32

    def dense(k, shape):
        return jax.random.normal(k, shape, f32) * shape[-2] ** -0.5

    def gain(k, shape):
        return 1.0 + 0.02 * jax.random.normal(k, shape, f32)

    H, G, Dh = NSA_HEADS, NSA_KV_HEADS, HEAD_DIM
    return {
        'x': jax.random.normal(ks[0], (BATCH, SEQ, D_MODEL), f32),
        'ffn1_norm': gain(ks[1], (DEPTH, D_MODEL)),
        'ffn1_w_in': dense(ks[2], (DEPTH, D_MODEL, 2 * D_FF)),
        'ffn1_w_out': dense(ks[3], (DEPTH, D_FF, D_MODEL)),
        'mix_norm': gain(ks[4], (DEPTH, D_MODEL)),
        'ffn2_norm': gain(ks[5], (DEPTH, D_MODEL)),
        'ffn2_w_in': dense(ks[6], (DEPTH, D_MODEL, 2 * D_FF)),
        'ffn2_w_out': dense(ks[7], (DEPTH, D_FF, D_MODEL)),
        'sb_w_qkv': dense(ks[8], (N_A_LAYERS, D_MODEL, 3 * SB_HEADS * Dh)),
        'sb_w_out': dense(ks[9], (N_A_LAYERS, SB_HEADS * Dh, D_MODEL)),
        'kv_norm': gain(ks[10], (D_MODEL,)),
        'nsa_w_kv': dense(ks[11], (D_MODEL, 2 * N_BRANCHES * G * Dh)),
        'nsa_k_norm': gain(ks[12], (N_BRANCHES, Dh)),
        'cmp_pos_k': 0.1 * jax.random.normal(ks[13], (CMP_BLOCK, Dh), f32),
        'cmp_pos_v': 0.1 * jax.random.normal(ks[14], (CMP_BLOCK, Dh), f32),
        'cmp_k_w1': dense(ks[15], (CMP_BLOCK * Dh, CMP_HIDDEN)),
        'cmp_k_w2': dense(ks[16], (CMP_HIDDEN, Dh)),
        'cmp_v_w1': dense(ks[17], (CMP_BLOCK * Dh, CMP_HIDDEN)),
        'cmp_v_w2': dense(ks[18], (CMP_HIDDEN, Dh)),
        'nsa_w_in': dense(ks[19], (N_B_LAYERS, D_MODEL, H * Dh + N_BRANCHES * H)),
        'nsa_q_norm': gain(ks[20], (N_B_LAYERS, Dh)),
        'nsa_w_out': dense(ks[21], (N_B_LAYERS, H * Dh, D_MODEL)),
    }


def reference(x, ffn1_norm, ffn1_w_in, ffn1_w_out, mix_norm, ffn2_norm, ffn2_w_in, ffn2_w_out,
              sb_w_qkv, sb_w_out, kv_norm, nsa_w_kv, nsa_k_norm, cmp_pos_k, cmp_pos_v,
              cmp_k_w1, cmp_k_w2, cmp_v_w1, cmp_v_w2, nsa_w_in, nsa_q_norm, nsa_w_out):
    B, S, _ = x.shape
    cos, sin = _rope_tables(jnp.arange(S))
    h = x
    kvs = None
    for layer in range(DEPTH):
        h = h + 0.5 * _swiglu(_rmsnorm(h, ffn1_norm[layer]), ffn1_w_in[layer], ffn1_w_out[layer])
        hn = _rmsnorm(h, mix_norm[layer])
        if layer < N_A_LAYERS:
            h = h + _stick_breaking_mixer(hn, sb_w_qkv[layer], sb_w_out[layer])
        else:
            i = layer - N_A_LAYERS
            h = h + _nsa_mixer(hn, kvs, nsa_w_in[i], nsa_q_norm[i], nsa_w_out[i], cos, sin)
        h = h + 0.5 * _swiglu(_rmsnorm(h, ffn2_norm[layer]), ffn2_w_in[layer], ffn2_w_out[layer])
        if layer == N_A_LAYERS - 1:
            kvs = _nsa_shared_kv(h, kv_norm, nsa_w_kv, nsa_k_norm, cmp_pos_k, cmp_pos_v,
                                 cmp_k_w1, cmp_k_w2, cmp_v_w1, cmp_v_w2, cos, sin)
    return h
```

```python
import functools

import jax
import jax.numpy as jnp
import numpy as np
from jax import lax
from jax.experimental import pallas as pl
from jax.experimental.pallas import tpu as pltpu

F32 = jnp.float32
BF16 = jnp.bfloat16

HEAD_DIM = 64
NSA_KV_HEADS = 4
NSA_GROUP = 4
N_BRANCHES = 3
ROPE_DIMS = 16
ROPE_THETA = 500000.0
CMP_BLOCK = 32
CMP_STRIDE = 16
SEL_BLOCK = 64
N_SEL = 8
N_LOCAL_SEL = 2
WINDOW = 512
EPS = 1e-6
NEG = -1e30
FORCE = 1e4

LANES = 128
GROUP_LANES = NSA_GROUP * HEAD_DIM
VMEM_LIMIT_BYTES = 56 * 1024 * 1024

FFN_TM = 512
PROJ_TM = 1024
ATT_TQ = 256
ATT_BK = 256


def _dot(a, b):
    return jnp.dot(a, b, preferred_element_type=F32)


def _dot_nt(a, b):
    return lax.dot_general(a, b, (((1,), (1,)), ((), ())), preferred_element_type=F32)


def _split_bf16(x):
    hi = x.astype(BF16)
    lo = (x - hi.astype(F32)).astype(BF16)
    return hi, lo


def _dot_split(x, w):
    hi, lo = _split_bf16(x)
    return _dot(hi, w) + _dot(lo, w)


def _sigmoid(x):
    return 1.0 / (1.0 + jnp.exp(-x))


def _iota(shape, dim):
    return lax.broadcasted_iota(jnp.int32, shape, dim)


def _params(*sem):
    return pltpu.CompilerParams(dimension_semantics=sem, vmem_limit_bytes=VMEM_LIMIT_BYTES)


def _ffn_kernel(h_ref, g_ref, wa_ref, wb_ref, wo_ref, o_ref, xn_ref, acc_ref):
    j = pl.program_id(1)

    @pl.when(j == 0)
    def _():
        x = h_ref[...]
        ms = jnp.mean(x * x, axis=-1, keepdims=True)
        xn_ref[...] = (x * lax.rsqrt(ms + EPS) * g_ref[...]).astype(BF16)
        acc_ref[...] = jnp.zeros_like(acc_ref)

    xn = xn_ref[...]
    a = _dot(xn, wa_ref[...])
    b = _dot(xn, wb_ref[...])
    act = (a * _sigmoid(a) * b).astype(BF16)
    acc_ref[...] += _dot(act, wo_ref[...])

    @pl.when(j == pl.num_programs(1) - 1)
    def _():
        o_ref[...] = h_ref[...] + 0.5 * acc_ref[...]


def _ffn(h, gain, w_in, w_out):
    T, D = h.shape
    F = w_out.shape[0]
    tm = min(FFN_TM, T)
    tf = F // 2 if (F // 2) % LANES == 0 else F
    nf = F // tf
    return pl.pallas_call(
        _ffn_kernel,
        grid=(T // tm, nf),
        in_specs=[
            pl.BlockSpec((tm, D), lambda i, j: (i, 0)),
            pl.BlockSpec((1, D), lambda i, j: (0, 0)),
            pl.BlockSpec((D, tf), lambda i, j: (0, j)),
            pl.BlockSpec((D, tf), lambda i, j: (0, j + nf)),
            pl.BlockSpec((tf, D), lambda i, j: (j, 0)),
        ],
        out_specs=pl.BlockSpec((tm, D), lambda i, j: (i, 0)),
        out_shape=jax.ShapeDtypeStruct((T, D), F32),
        scratch_shapes=[pltpu.VMEM((tm, D), BF16), pltpu.VMEM((tm, D), F32)],
        compiler_params=_params("parallel", "arbitrary"),
        name="ffn",
    )(h, gain.reshape(1, D), w_in, w_in, w_out)


def _norm_proj_kernel(h_ref, g_ref, w_ref, o_ref, xn_ref):
    @pl.when(pl.program_id(1) == 0)
    def _():
        x = h_ref[...]
        ms = jnp.mean(x * x, axis=-1, keepdims=True)
        xn_ref[...] = (x * lax.rsqrt(ms + EPS) * g_ref[...]).astype(BF16)

    o_ref[...] = _dot(xn_ref[...], w_ref[...]).astype(o_ref.dtype)


def _norm_proj(h, gain, w, tn, out_dtype):
    T, D = h.shape
    N = w.shape[1]
    tm = min(PROJ_TM, T)
    return pl.pallas_call(
        _norm_proj_kernel,
        grid=(T // tm, N // tn),
        in_specs=[
            pl.BlockSpec((tm, D), lambda i, j: (i, 0)),
            pl.BlockSpec((1, D), lambda i, j: (0, 0)),
            pl.BlockSpec((D, tn), lambda i, j: (0, j)),
        ],
        out_specs=pl.BlockSpec((tm, tn), lambda i, j: (i, j)),
        out_shape=jax.ShapeDtypeStruct((T, N), out_dtype),
        scratch_shapes=[pltpu.VMEM((tm, D), BF16)],
        compiler_params=_params("parallel", "arbitrary"),
        name="norm_proj",
    )(h, gain.reshape(1, D), w)


def _out_proj_kernel(h_ref, x_ref, w_ref, o_ref):
    o_ref[...] = h_ref[...] + _dot(x_ref[...], w_ref[...])


def _out_proj(h, x, w):
    T, D = h.shape
    K = x.shape[1]
    tm = min(PROJ_TM, T)
    return pl.pallas_call(
        _out_proj_kernel,
        grid=(T // tm,),
        in_specs=[
            pl.BlockSpec((tm, D), lambda i: (i, 0)),
            pl.BlockSpec((tm, K), lambda i: (i, 0)),
            pl.BlockSpec((K, D), lambda i: (0, 0)),
        ],
        out_specs=pl.BlockSpec((tm, D), lambda i: (i, 0)),
        out_shape=jax.ShapeDtypeStruct((T, D), F32),
        compiler_params=_params("parallel"),
        name="out_proj",
    )(h, x, w)


def _sb_kernel(q_ref, k_ref, v_ref, o_ref, acc_ref, c_ref, *, tq, bk, scale):
    i = pl.program_id(2)
    lane = _iota((1, LANES), 1)
    is_a = lane < HEAD_DIM
    q = q_ref[...]
    t_pos = i * tq + _iota((tq, 1), 0)
    tri = jnp.where(_iota((bk, bk), 0) > _iota((bk, bk), 1), 1.0, 0.0).astype(BF16)
    nblk = (i * tq + tq) // bk
    outs = []
    for first in (True, False):
        qh = jnp.where(is_a if first else jnp.logical_not(is_a), q, jnp.zeros_like(q))
        acc_ref[...] = jnp.zeros_like(acc_ref)
        c_ref[...] = jnp.zeros_like(c_ref)

        def body(jj, carry, qh=qh):
            j = nblk - 1 - jj
            off = pl.multiple_of(j * bk, bk)
            kb = k_ref[pl.ds(off, bk), :]
            vb = v_ref[pl.ds(off, bk), :]
            past = (j * bk + _iota((1, bk), 1)) < t_pos
            z = _dot_nt(qh, kb) * scale
            log_b = jnp.minimum(z, 0.0) - jnp.log(1.0 + jnp.exp(-jnp.abs(z)))
            log_1m = jnp.where(past, log_b - z, 0.0)
            later = c_ref[...] + _dot_split(log_1m, tri)
            w = jnp.where(past, jnp.exp(log_b + later), 0.0)
            acc_ref[...] += _dot(w.astype(BF16), vb)
            c_ref[...] += jnp.sum(log_1m, axis=1, keepdims=True)
            return carry

        lax.fori_loop(0, nblk, body, 0)
        outs.append(acc_ref[...])
    o_ref[...] = jnp.where(is_a, outs[0], outs[1]).astype(o_ref.dtype)


def _sb_attention(qkv, B, S, n_heads):
    T = B * S
    tq, bk = ATT_TQ, ATT_BK
    nq = S // tq
    n_pairs = n_heads * HEAD_DIM // LANES
    kern = functools.partial(_sb_kernel, tq=tq, bk=bk, scale=HEAD_DIM ** -0.5)
    return pl.pallas_call(
        kern,
        grid=(B, n_pairs, nq),
        in_specs=[
            pl.BlockSpec((tq, LANES), lambda b, p, i: (b * nq + i, p)),
            pl.BlockSpec((S, LANES), lambda b, p, i: (b, n_pairs + p)),
            pl.BlockSpec((S, LANES), lambda b, p, i: (b, 2 * n_pairs + p)),
        ],
        out_specs=pl.BlockSpec((tq, LANES), lambda b, p, i: (b * nq + i, p)),
        out_shape=jax.ShapeDtypeStruct((T, n_heads * HEAD_DIM), BF16),
        scratch_shapes=[pltpu.VMEM((tq, LANES), F32), pltpu.VMEM((tq, 1), F32)],
        compiler_params=_params("parallel", "parallel", "arbitrary"),
        name="sb_attn",
    )(qkv, qkv, qkv)


def _head_rmsnorm(x, bd, gain):
    ms = _dot_split(x * x, bd)
    return x * lax.rsqrt(ms + EPS) * gain


def _rope(y, c, sa, sb):
    w = y.shape[-1]
    half = ROPE_DIMS // 2
    return y * c + pltpu.roll(y, w - half, 1) * sa + pltpu.roll(y, half, 1) * sb


def _rope_coeffs(positions, n_heads):
    inv_freq = jnp.power(ROPE_THETA, -jnp.arange(0, ROPE_DIMS, 2, dtype=F32) / ROPE_DIMS)
    ang = positions.astype(F32)[:, None] * inv_freq[None, :]
    cos, sin = jnp.cos(ang), jnp.sin(ang)
    p = positions.shape[0]
    rest = HEAD_DIM - ROPE_DIMS
    zeros_h = jnp.zeros((p, ROPE_DIMS // 2), F32)
    c = jnp.concatenate([cos, cos, jnp.ones((p, rest), F32)], axis=-1)
    sa = jnp.concatenate([-sin, zeros_h, jnp.zeros((p, rest), F32)], axis=-1)
    sb = jnp.concatenate([zeros_h, sin, jnp.zeros((p, rest), F32)], axis=-1)
    return tuple(jnp.tile(t, (1, n_heads)) for t in (c, sa, sb))


def _block_mean_matrix(width):
    idx = np.arange(width) // HEAD_DIM
    return jnp.asarray((idx[:, None] == idx[None, :]).astype(np.float32) / HEAD_DIM, dtype=BF16)


def _kv_prep_kernel(sel_ref, win_ref, bd_ref, gs_ref, gw_ref, c_ref, sa_ref, sb_ref, o_ref):
    w = GROUP_LANES
    bd = bd_ref[...]
    c, sa, sb = c_ref[...], sa_ref[...], sb_ref[...]
    ks = _rope(_head_rmsnorm(sel_ref[:, :w], bd, gs_ref[...]), c, sa, sb)
    kw = _rope(_head_rmsnorm(win_ref[:, :w], bd, gw_ref[...]), c, sa, sb)
    o_ref[:, 0 * w:1 * w] = ks.astype(BF16)
    o_ref[:, 1 * w:2 * w] = sel_ref[:, w:].astype(BF16)
    o_ref[:, 2 * w:3 * w] = kw.astype(BF16)
    o_ref[:, 3 * w:4 * w] = win_ref[:, w:].astype(BF16)


def _kv_prep(kv, k_norm, coeffs, S):
    T = kv.shape[0]
    w = GROUP_LANES
    tm = min(PROJ_TM, S)
    ns = S // tm
    gs = jnp.tile(k_norm[1], NSA_KV_HEADS).reshape(1, w)
    gw = jnp.tile(k_norm[2], NSA_KV_HEADS).reshape(1, w)
    tab = pl.BlockSpec((tm, w), lambda i: (i % ns, 0))
    one = pl.BlockSpec((1, w), lambda i: (0, 0))
    return pl.pallas_call(
        _kv_prep_kernel,
        grid=(T // tm,),
        in_specs=[
            pl.BlockSpec((tm, 2 * w), lambda i: (i, 1)),
            pl.BlockSpec((tm, 2 * w), lambda i: (i, 2)),
            pl.BlockSpec((w, w), lambda i: (0, 0)),
            one, one, tab, tab, tab,
        ],
        out_specs=pl.BlockSpec((tm, 4 * w), lambda i: (i, 0)),
        out_shape=jax.ShapeDtypeStruct((T, 4 * w), BF16),
        compiler_params=_params("parallel"),
        name="kv_prep",
    )(kv, kv, _block_mean_matrix(w), gs, gw, *coeffs)


def _compress_kernel(x_ref, pos_ref, w1_ref, w2_ref, bd_ref, g_ref, c_ref, sa_ref, sb_ref,
                     o_ref, top_ref, bot_ref):
    t = pl.program_id(0)
    l = pl.program_id(1)
    half = CMP_BLOCK // 2

    @pl.when(l == 0)
    def _():
        top_ref[...] = jnp.zeros_like(top_ref)
        bot_ref[...] = jnp.zeros_like(bot_ref)

    x = x_ref[0]
    top_ref[...] += _dot((x + pos_ref[0, 0]).astype(BF16), w1_ref[0, 0])
    bot_ref[...] += _dot((x + pos_ref[0, 1]).astype(BF16), w1_ref[0, 1])

    @pl.when(l == half - 1)
    def _():
        rows = bot_ref.shape[0]
        hid = top_ref[...] + pltpu.roll(bot_ref[...], rows - 1, 0)
        act = (hid * _sigmoid(hid)).astype(BF16)
        y = _dot(act, w2_ref[0])
        yk = _rope(_head_rmsnorm(y, bd_ref[...], g_ref[...]), c_ref[...], sa_ref[...], sb_ref[...])
        o_ref[0] = jnp.where(t == 0, yk, y).astype(BF16)


def _compress(kv, B, S, pos_k, pos_v, k_w1, k_w2, v_w1, v_w2, k_gain):
    w = GROUP_LANES
    G = NSA_KV_HEADS
    half = CMP_BLOCK // 2
    nc = S // CMP_STRIDE
    rows = B * nc
    hid = k_w1.shape[1]
    raw = kv[:, :2 * w].reshape(B, nc, half, 2, w)
    xc = jnp.transpose(raw, (3, 2, 0, 1, 4)).reshape(2, half, rows, w)
    eye = jnp.eye(G, dtype=F32)

    def blockdiag(wm):
        out = jnp.einsum('gh,...ab->...gahb', eye, wm)
        return out.reshape(*wm.shape[:-2], G * wm.shape[-2], G * wm.shape[-1])

    def prep_w1(w1):
        w1 = w1.reshape(2, half, HEAD_DIM, hid)
        return jnp.transpose(blockdiag(w1), (1, 0, 2, 3))

    w1 = jnp.stack([prep_w1(k_w1), prep_w1(v_w1)]).astype(BF16)
    w2 = jnp.stack([blockdiag(k_w2), blockdiag(v_w2)]).astype(BF16)

    def prep_pos(p):
        p = jnp.tile(p, (1, G)).reshape(2, half, 1, w)
        return jnp.transpose(p, (1, 0, 2, 3))

    pos = jnp.stack([prep_pos(pos_k), prep_pos(pos_v)])
    c_end = jnp.arange(nc) * CMP_STRIDE + (CMP_BLOCK - 1)
    coeffs = tuple(jnp.tile(tb, (B, 1)) for tb in _rope_coeffs(c_end, G))
    gain = jnp.tile(k_gain, G).reshape(1, w)
    const2 = lambda t, l: (0, 0)
    out = pl.pallas_call(
        _compress_kernel,
        grid=(2, half),
        in_specs=[
            pl.BlockSpec((None, 1, rows, w), lambda t, l: (t, l, 0, 0)),
            pl.BlockSpec((None, 1, 2, 1, w), lambda t, l: (t, l, 0, 0, 0)),
            pl.BlockSpec((None, 1, 2, w, G * hid), lambda t, l: (t, l, 0, 0, 0)),
            pl.BlockSpec((1, G * hid, w), lambda t, l: (t, 0, 0)),
            pl.BlockSpec((w, w), const2),
            pl.BlockSpec((1, w), const2),
            pl.BlockSpec((rows, w), const2),
            pl.BlockSpec((rows, w), const2),
            pl.BlockSpec((rows, w), const2),
        ],
        out_specs=pl.BlockSpec((1, rows, w), lambda t, l: (t, 0, 0)),
        out_shape=jax.ShapeDtypeStruct((2, rows, w), BF16),
        scratch_shapes=[pltpu.VMEM((rows, G * hid), F32), pltpu.VMEM((rows, G * hid), F32)],
        compiler_params=_params("arbitrary", "arbitrary"),
        name="nsa_compress",
    )(xc, pos, w1, w2, _block_mean_matrix(w), gain, *coeffs)
    return out[0], out[1]


def _nsa_kernel(q_ref, gl_ref, kc_ref, vc_ref, ks_ref, vs_ref, kw_ref, vw_ref,
                bd_ref, qg_ref, c_ref, sa_ref, sb_ref, ov_ref,
                o_ref,
                kc_rep, vc_rep, ks_rep, vs_rep, kw_rep, vw_rep, m_ref, l_ref, acc_ref,
                *, tq, bk, nsb, scale):
    g = pl.program_id(1)
    i = pl.program_id(2)
    w = GROUP_LANES
    R = NSA_GROUP
    rows = R * tq
    lane_head = _iota((1, w), 1) >> 6

    @pl.when(i == 0)
    def _():
        rr, cc = _iota((w, w), 0), _iota((w, w), 1)
        pick = jnp.where(((rr >> 6) == g) & ((rr & 63) == (cc & 63)), 1.0, 0.0).astype(BF16)
        for src, dst in ((kc_ref, kc_rep), (vc_ref, vc_rep), (ks_ref, ks_rep),
                         (vs_ref, vs_rep), (kw_ref, kw_rep), (vw_ref, vw_rep)):
            dst[...] = _dot(src[...], pick).astype(BF16)

    qn = _rope(_head_rmsnorm(q_ref[...], bd_ref[...], qg_ref[...]), c_ref[...], sa_ref[...], sb_ref[...])
    qn = qn.astype(BF16)
    qm = jnp.concatenate(
        [jnp.where(lane_head == r, qn, jnp.zeros_like(qn)) for r in range(R)], axis=0)
    t_row = i * tq + (_iota((rows, 1), 0) & (tq - 1))

    def unstack(x):
        out = jnp.where(lane_head == 0, x[0:tq], 0.0)
        for r in range(1, R):
            out = out + jnp.where(lane_head == r, x[r * tq:(r + 1) * tq], 0.0)
        return out

    ncp = kc_rep.shape[0]
    s = _dot_nt(qm, kc_rep[...]) * scale
    c_end = _iota((1, ncp), 1) * CMP_STRIDE + (CMP_BLOCK - 1)
    valid = c_end <= t_row
    s = jnp.where(valid, s, NEG)
    e = jnp.exp(s - jnp.max(s, axis=1, keepdims=True))
    p = e / jnp.sum(e, axis=1, keepdims=True)
    p = jnp.where(t_row >= CMP_BLOCK - 1, p, 0.0)
    o_cmp = unstack(_dot(p.astype(BF16), vc_rep[...]))
    p_sum = p[0:tq]
    for r in range(1, R):
        p_sum = p_sum + p[r * tq:(r + 1) * tq]

    hi, lo = _split_bf16(p_sum)
    ov = ov_ref[...]
    imp = (_dot_nt(ov, hi) + _dot_nt(ov, lo))[0:nsb]
    jb = _iota((nsb, 1), 0)
    cur = (i * tq + _iota((1, tq), 1)) >> 6
    forced = (jb == 0) | ((cur - jb >= 0) & (cur - jb < N_LOCAL_SEL))
    score = jnp.where(forced, FORCE, jnp.where(jb <= cur, imp, -FORCE))
    cnt = jnp.zeros((nsb, tq), F32)
    for i2 in range(nsb):
        row = score[i2:i2 + 1, :]
        beats = (row > score) | ((row == score) & (jb > i2))
        cnt = cnt + jnp.where(beats, 1.0, 0.0)
    sel_t = jnp.where(cnt < float(min(N_SEL, nsb)), 1.0, 0.0)
    if nsb < LANES:
        sel_t = jnp.concatenate([sel_t, jnp.zeros((LANES - nsb, tq), F32)], axis=0)
    sel = sel_t.T.astype(BF16)

    def flash_block(k_rep, v_rep, jblk, ok_fn):
        off = pl.multiple_of(jblk * bk, bk)
        kb = k_rep[pl.ds(off, bk), :]
        vb = v_rep[pl.ds(off, bk), :]
        s_pos = jblk * bk + _iota((1, bk), 1)
        ok = ok_fn(s_pos, jblk)
        sc = jnp.where(ok, _dot_nt(qm, kb) * scale, NEG)
        m_old = m_ref[...]
        m_new = jnp.maximum(m_old, jnp.max(sc, axis=1, keepdims=True))
        alpha = jnp.exp(m_old - m_new)
        pe = jnp.where(ok, jnp.exp(sc - m_new), 0.0)
        l_ref[...] = alpha * l_ref[...] + jnp.sum(pe, axis=1, keepdims=True)
        acc_ref[...] = alpha * acc_ref[...] + _dot(pe.astype(BF16), vb)
        m_ref[...] = m_new

    def reset():
        m_ref[...] = jnp.full_like(m_ref, NEG)
        l_ref[...] = jnp.zeros_like(l_ref)
        acc_ref[...] = jnp.zeros_like(acc_ref)

    def sel_ok(s_pos, jblk):
        blk = (jblk * bk + _iota((LANES, bk), 1)) >> 6
        expand = jnp.where(_iota((LANES, bk), 0) == blk, 1.0, 0.0).astype(BF16)
        md = _dot(sel, expand)
        md = jnp.concatenate([md] * R, axis=0)
        return (md > 0.5) & (s_pos <= t_row)

    reset()

    def sel_body(jblk, carry):
        flash_block(ks_rep, vs_rep, jblk, sel_ok)
        return carry

    lax.fori_loop(0, (i * tq + tq) // bk, sel_body, 0)
    o_sel = unstack(acc_ref[...] / l_ref[...])

    def win_ok(s_pos, jblk):
        return (s_pos <= t_row) & (t_row - s_pos < WINDOW)

    reset()

    def win_body(jblk, carry):
        flash_block(kw_rep, vw_rep, jblk, win_ok)
        return carry

    lax.fori_loop(jnp.maximum(i * tq - (WINDOW - 1), 0) // bk, (i * tq + tq) // bk, win_body, 0)
    o_win = unstack(acc_ref[...] / l_ref[...])

    ghi, glo = _split_bf16(_sigmoid(gl_ref[...]))
    col = _iota((LANES, w), 0)
    out = jnp.zeros((tq, w), F32)
    for br, o_br in enumerate((o_cmp, o_sel, o_win)):
        target = br * (NSA_KV_HEADS * R) + g * R + (_iota((LANES, w), 1) >> 6)
        spread = jnp.where(col == target, 1.0, 0.0).astype(BF16)
        out = out + (_dot(ghi, spread) + _dot(glo, spread)) * o_br
    o_ref[...] = out.astype(o_ref.dtype)


def _nsa_attention(proj, kc, vc, ksw, q_gain, coeffs, B, S):
    T = B * S
    w = GROUP_LANES
    G = NSA_KV_HEADS
    tq, bk = ATT_TQ, ATT_BK
    nq = S // tq
    nc = S // CMP_STRIDE
    nsb = S // SEL_BLOCK
    n_cmp = (S - CMP_BLOCK) // CMP_STRIDE + 1
    cs = np.arange(nc) * CMP_STRIDE
    js = np.arange(LANES) * SEL_BLOCK
    ov = ((cs[None, :] < js[:, None] + SEL_BLOCK) & (cs[None, :] + CMP_BLOCK > js[:, None])
          & (np.arange(nc)[None, :] < n_cmp) & (np.arange(LANES)[:, None] < nsb))
    ov = jnp.asarray(ov.astype(np.float32), dtype=BF16)
    gate_blk = (G * NSA_GROUP * HEAD_DIM) // LANES
    kern = functools.partial(_nsa_kernel, tq=tq, bk=bk, nsb=nsb, scale=HEAD_DIM ** -0.5)
    seq = lambda col: pl.BlockSpec((S, w), lambda b, g, i: (b, col))
    const = lambda shape: pl.BlockSpec(shape, lambda b, g, i: (0, 0))
    tab = pl.BlockSpec((tq, w), lambda b, g, i: (i, 0))
    return pl.pallas_call(
        kern,
        grid=(B, G, nq),
        in_specs=[
            pl.BlockSpec((tq, w), lambda b, g, i: (b * nq + i, g)),
            pl.BlockSpec((tq, LANES), lambda b, g, i: (b * nq + i, gate_blk)),
            pl.BlockSpec((nc, w), lambda b, g, i: (b, 0)),
            pl.BlockSpec((nc, w), lambda b, g, i: (b, 0)),
            seq(0), seq(1), seq(2), seq(3),
            const((w, w)), const((1, w)), tab, tab, tab, const((LANES, nc)),
        ],
        out_specs=pl.BlockSpec((tq, w), lambda b, g, i: (b * nq + i, g)),
        out_shape=jax.ShapeDtypeStruct((T, G * w), BF16),
        scratch_shapes=[
            pltpu.VMEM((nc, w), BF16), pltpu.VMEM((nc, w), BF16),
            pltpu.VMEM((S, w), BF16), pltpu.VMEM((S, w), BF16),
            pltpu.VMEM((S, w), BF16), pltpu.VMEM((S, w), BF16),
            pltpu.VMEM((NSA_GROUP * tq, 1), F32), pltpu.VMEM((NSA_GROUP * tq, 1), F32),
            pltpu.VMEM((NSA_GROUP * tq, w), F32),
        ],
        compiler_params=_params("parallel", "arbitrary", "arbitrary"),
        name="nsa_attn",
    )(proj, proj, kc, vc, ksw, ksw, ksw, ksw,
      _block_mean_matrix(w), jnp.tile(q_gain, NSA_GROUP).reshape(1, w), *coeffs, ov)


def kernel(x, ffn1_norm, ffn1_w_in, ffn1_w_out, mix_norm, ffn2_norm, ffn2_w_in, ffn2_w_out,
           sb_w_qkv, sb_w_out, kv_norm, nsa_w_kv, nsa_k_norm, cmp_pos_k, cmp_pos_v,
           cmp_k_w1, cmp_k_w2, cmp_v_w1, cmp_v_w2, nsa_w_in, nsa_q_norm, nsa_w_out):
    B, S, D = x.shape
    depth = ffn1_norm.shape[0]
    n_a = sb_w_qkv.shape[0]
    n_sb_heads = sb_w_out.shape[1] // HEAD_DIM
    assert S % 512 == 0 and S // CMP_STRIDE == LANES, "sequence tiling assumes 16*128 tokens"
    h = x.reshape(B * S, D)
    bf = lambda a: a.astype(BF16)
    coeffs = _rope_coeffs(jnp.arange(S), NSA_KV_HEADS)
    n_q = nsa_w_in.shape[-1]
    n_q_pad = -(-n_q // LANES) * LANES
    kc = vc = ksw = None
    for layer in range(depth):
        h = _ffn(h, ffn1_norm[layer], bf(ffn1_w_in[layer]), bf(ffn1_w_out[layer]))
        if layer < n_a:
            qkv = _norm_proj(h, mix_norm[layer], bf(sb_w_qkv[layer]), 1024, BF16)
            o = _sb_attention(qkv, B, S, n_sb_heads)
            h = _out_proj(h, o, bf(sb_w_out[layer]))
        else:
            li = layer - n_a
            w_in = jnp.pad(bf(nsa_w_in[li]), ((0, 0), (0, n_q_pad - n_q)))
            proj = _norm_proj(h, mix_norm[layer], w_in, n_q_pad, F32)
            o = _nsa_attention(proj, kc, vc, ksw, nsa_q_norm[li], coeffs, B, S)
            h = _out_proj(h, o, bf(nsa_w_out[li]))
        h = _ffn(h, ffn2_norm[layer], bf(ffn2_w_in[layer]), bf(ffn2_w_out[layer]))
        if layer == n_a - 1:
            kv = _norm_proj(h, kv_norm, bf(nsa_w_kv), 768, F32)
            kc, vc = _compress(kv, B, S, cmp_pos_k, cmp_pos_v, cmp_k_w1, cmp_k_w2,
                               cmp_v_w1, cmp_v_w2, nsa_k_norm[0])
            ksw = _kv_prep(kv, nsa_k_norm, coeffs, S)
    return h.reshape(B, S, D)
```

```python
import functools

import jax
import jax.numpy as jnp
import numpy as np
from jax import lax
from jax.experimental import pallas as pl
from jax.experimental.pallas import tpu as pltpu

F32 = jnp.float32
BF16 = jnp.bfloat16

HEAD_DIM = 64
NSA_KV_HEADS = 4
NSA_GROUP = 4
N_BRANCHES = 3
ROPE_DIMS = 16
ROPE_THETA = 500000.0
CMP_BLOCK = 32
CMP_STRIDE = 16
SEL_BLOCK = 64
N_SEL = 8
N_LOCAL_SEL = 2
WINDOW = 512
EPS = 1e-6
NEG = -1e30
FORCE = 1e4

LANES = 128
GROUP_LANES = NSA_GROUP * HEAD_DIM
VMEM_LIMIT_BYTES = 56 * 1024 * 1024

FFN_TM = 512
PROJ_TM = 1024
ATT_TQ = 256
ATT_BK = 256


def _dot(a, b):
    return jnp.dot(a, b, preferred_element_type=F32)


def _dot_nt(a, b):
    return lax.dot_general(a, b, (((1,), (1,)), ((), ())), preferred_element_type=F32)


def _split_bf16(x):
    hi = x.astype(BF16)
    lo = (x - hi.astype(F32)).astype(BF16)
    return hi, lo


def _dot_split(x, w):
    hi, lo = _split_bf16(x)
    n = x.shape[0]
    r = _dot(jnp.concatenate([hi, lo], axis=0), w)
    return r[0:n] + r[n:2 * n]


def _sigmoid(x):
    return 1.0 / (1.0 + jnp.exp(-x))


def _iota(shape, dim):
    return lax.broadcasted_iota(jnp.int32, shape, dim)


def _params(*sem):
    return pltpu.CompilerParams(dimension_semantics=sem, vmem_limit_bytes=VMEM_LIMIT_BYTES)


def _ffn_kernel(h_ref, g_ref, wa_ref, wb_ref, wo_ref, o_ref, xn_ref, acc_ref):
    j = pl.program_id(1)

    @pl.when(j == 0)
    def _():
        x = h_ref[...]
        ms = jnp.mean(x * x, axis=-1, keepdims=True)
        xn_ref[...] = (x * lax.rsqrt(ms + EPS) * g_ref[...]).astype(BF16)
        acc_ref[...] = jnp.zeros_like(acc_ref)

    xn = xn_ref[...]
    a = _dot(xn, wa_ref[...])
    b = _dot(xn, wb_ref[...])
    act = (a * _sigmoid(a) * b).astype(BF16)
    acc_ref[...] += _dot(act, wo_ref[...])

    @pl.when(j == pl.num_programs(1) - 1)
    def _():
        o_ref[...] = h_ref[...] + 0.5 * acc_ref[...]


def _ffn(h, gain, w_in, w_out):
    T, D = h.shape
    F = w_out.shape[0]
    tm = min(FFN_TM, T)
    tf = F // 2 if (F // 2) % LANES == 0 else F
    nf = F // tf
    return pl.pallas_call(
        _ffn_kernel,
        grid=(T // tm, nf),
        in_specs=[
            pl.BlockSpec((tm, D), lambda i, j: (i, 0)),
            pl.BlockSpec((1, D), lambda i, j: (0, 0)),
            pl.BlockSpec((D, tf), lambda i, j: (0, j)),
            pl.BlockSpec((D, tf), lambda i, j: (0, j + nf)),
            pl.BlockSpec((tf, D), lambda i, j: (j, 0)),
        ],
        out_specs=pl.BlockSpec((tm, D), lambda i, j: (i, 0)),
        out_shape=jax.ShapeDtypeStruct((T, D), F32),
        scratch_shapes=[pltpu.VMEM((tm, D), BF16), pltpu.VMEM((tm, D), F32)],
        compiler_params=_params("parallel", "arbitrary"),
        name="ffn",
    )(h, gain.reshape(1, D), w_in, w_in, w_out)


def _norm_proj_kernel(h_ref, g_ref, w_ref, o_ref, xn_ref):
    @pl.when(pl.program_id(1) == 0)
    def _():
        x = h_ref[...]
        ms = jnp.mean(x * x, axis=-1, keepdims=True)
        xn_ref[...] = (x * lax.rsqrt(ms + EPS) * g_ref[...]).astype(BF16)

    o_ref[...] = _dot(xn_ref[...], w_ref[...]).astype(o_ref.dtype)


def _norm_proj(h, gain, w, tn, out_dtype):
    T, D = h.shape
    N = w.shape[1]
    tm = min(PROJ_TM, T)
    return pl.pallas_call(
        _norm_proj_kernel,
        grid=(T // tm, N // tn),
        in_specs=[
            pl.BlockSpec((tm, D), lambda i, j: (i, 0)),
            pl.BlockSpec((1, D), lambda i, j: (0, 0)),
            pl.BlockSpec((D, tn), lambda i, j: (0, j)),
        ],
        out_specs=pl.BlockSpec((tm, tn), lambda i, j: (i, j)),
        out_shape=jax.ShapeDtypeStruct((T, N), out_dtype),
        scratch_shapes=[pltpu.VMEM((tm, D), BF16)],
        compiler_params=_params("parallel", "arbitrary"),
        name="norm_proj",
    )(h, gain.reshape(1, D), w)


def _out_proj_kernel(h_ref, x_ref, w_ref, o_ref):
    o_ref[...] = h_ref[...] + _dot(x_ref[...], w_ref[...])


def _out_proj(h, x, w):
    T, D = h.shape
    K = x.shape[1]
    tm = min(PROJ_TM, T)
    return pl.pallas_call(
        _out_proj_kernel,
        grid=(T // tm,),
        in_specs=[
            pl.BlockSpec((tm, D), lambda i: (i, 0)),
            pl.BlockSpec((tm, K), lambda i: (i, 0)),
            pl.BlockSpec((K, D), lambda i: (0, 0)),
        ],
        out_specs=pl.BlockSpec((tm, D), lambda i: (i, 0)),
        out_shape=jax.ShapeDtypeStruct((T, D), F32),
        compiler_params=_params("parallel"),
        name="out_proj",
    )(h, x, w)


def _sb_kernel(q_ref, k_ref, v_ref, o_ref, acc_ref, c_ref, *, tq, bk, scale):
    i = pl.program_id(2)
    lane = _iota((1, LANES), 1)
    is_a = lane < HEAD_DIM
    q = q_ref[...] * scale
    zero = jnp.zeros_like(q)
    qh = (jnp.where(is_a, q, zero), jnp.where(is_a, zero, q))
    tri = jnp.where(_iota((bk, bk), 0) > _iota((bk, bk), 1), 1.0, 0.0).astype(BF16)
    acc_ref[...] = jnp.zeros_like(acc_ref)
    c_ref[...] = jnp.zeros_like(c_ref)

    def sweep(blocks, diagonal=False):
        chains = [(b, hd) for b in range(len(blocks)) for hd in range(2)]
        offs = [pl.multiple_of(j * bk, bk) for j in blocks]
        ks = [k_ref[pl.ds(off, bk), :] for off in offs]
        vs = [v_ref[pl.ds(off, bk), :] for off in offs]
        past = _iota((tq, bk), 1) < _iota((tq, bk), 0)
        z = {c: _dot_nt(qh[c[1]], ks[c[0]]) for c in chains}
        log_b, log_1m, parts = {}, {}, {}
        for c in chains:
            log_b[c] = jnp.minimum(z[c], 0.0) - jnp.log(1.0 + jnp.exp(-jnp.abs(z[c])))
            m = log_b[c] - z[c]
            log_1m[c] = jnp.where(past, m, 0.0) if (diagonal and c[0] == 0) else m
            parts[c] = _split_bf16(log_1m[c])
        within = {c: _dot(parts[c][0], tri) + _dot(parts[c][1], tri) for c in chains}
        carry = [c_ref[0:tq, :], c_ref[tq:2 * tq, :]]
        w = {}
        for c in chains:
            w[c] = jnp.exp(log_b[c] + (carry[c[1]] + within[c]))
            if diagonal and c[0] == 0:
                w[c] = jnp.where(past, w[c], 0.0)
            carry[c[1]] = carry[c[1]] + jnp.sum(log_1m[c], axis=1, keepdims=True)
        for hd in range(2):
            rws = slice(hd * tq, (hd + 1) * tq)
            upd = acc_ref[rws, :]
            for b in range(len(blocks)):
                upd = upd + _dot(w[(b, hd)].astype(BF16), vs[b])
            acc_ref[rws, :] = upd
            c_ref[rws, :] = carry[hd]

    sweep([i], diagonal=True)

    def body(jj, carry):
        j = i - 1 - 2 * jj
        sweep([j, j - 1])
        return carry

    lax.fori_loop(0, i // 2, body, 0)

    @pl.when(i % 2 == 1)
    def _():
        sweep([0])

    acc = acc_ref[...]
    o_ref[...] = jnp.where(is_a, acc[0:tq], acc[tq:2 * tq]).astype(o_ref.dtype)


def _sb_attention(qkv, B, S, n_heads):
    T = B * S
    tq, bk = ATT_TQ, ATT_BK
    nq = S // tq
    n_pairs = n_heads * HEAD_DIM // LANES
    kern = functools.partial(_sb_kernel, tq=tq, bk=bk, scale=HEAD_DIM ** -0.5)
    return pl.pallas_call(
        kern,
        grid=(B, n_pairs, nq),
        in_specs=[
            pl.BlockSpec((tq, LANES), lambda b, p, i: (b * nq + i, p)),
            pl.BlockSpec((S, LANES), lambda b, p, i: (b, n_pairs + p)),
            pl.BlockSpec((S, LANES), lambda b, p, i: (b, 2 * n_pairs + p)),
        ],
        out_specs=pl.BlockSpec((tq, LANES), lambda b, p, i: (b * nq + i, p)),
        out_shape=jax.ShapeDtypeStruct((T, n_heads * HEAD_DIM), BF16),
        scratch_shapes=[pltpu.VMEM((2 * tq, LANES), F32), pltpu.VMEM((2 * tq, 1), F32)],
        compiler_params=_params("parallel", "parallel", "arbitrary"),
        name="sb_attn",
    )(qkv, qkv, qkv)


def _head_rmsnorm(x, bd, gain):
    ms = _dot_split(x * x, bd)
    return x * lax.rsqrt(ms + EPS) * gain


def _rope(y, c, sa, sb):
    w = y.shape[-1]
    half = ROPE_DIMS // 2
    return y * c + pltpu.roll(y, w - half, 1) * sa + pltpu.roll(y, half, 1) * sb


def _rope_coeffs(positions, n_heads):
    inv_freq = jnp.power(ROPE_THETA, -jnp.arange(0, ROPE_DIMS, 2, dtype=F32) / ROPE_DIMS)
    ang = positions.astype(F32)[:, None] * inv_freq[None, :]
    cos, sin = jnp.cos(ang), jnp.sin(ang)
    p = positions.shape[0]
    rest = HEAD_DIM - ROPE_DIMS
    zeros_h = jnp.zeros((p, ROPE_DIMS // 2), F32)
    c = jnp.concatenate([cos, cos, jnp.ones((p, rest), F32)], axis=-1)
    sa = jnp.concatenate([-sin, zeros_h, jnp.zeros((p, rest), F32)], axis=-1)
    sb = jnp.concatenate([zeros_h, sin, jnp.zeros((p, rest), F32)], axis=-1)
    return tuple(jnp.tile(t, (1, n_heads)) for t in (c, sa, sb))


def _block_mean_matrix(width):
    idx = np.arange(width) // HEAD_DIM
    return jnp.asarray((idx[:, None] == idx[None, :]).astype(np.float32) / HEAD_DIM, dtype=BF16)


def _kv_prep_kernel(sel_ref, win_ref, bd_ref, gs_ref, gw_ref, c_ref, sa_ref, sb_ref, o_ref):
    w = GROUP_LANES
    bd = bd_ref[...]
    c, sa, sb = c_ref[...], sa_ref[...], sb_ref[...]
    ks = _rope(_head_rmsnorm(sel_ref[:, :w], bd, gs_ref[...]), c, sa, sb)
    kw = _rope(_head_rmsnorm(win_ref[:, :w], bd, gw_ref[...]), c, sa, sb)
    o_ref[:, 0 * w:1 * w] = ks.astype(BF16)
    o_ref[:, 1 * w:2 * w] = sel_ref[:, w:].astype(BF16)
    o_ref[:, 2 * w:3 * w] = kw.astype(BF16)
    o_ref[:, 3 * w:4 * w] = win_ref[:, w:].astype(BF16)


def _kv_prep(kv, k_norm, coeffs, S):
    T = kv.shape[0]
    w = GROUP_LANES
    tm = min(PROJ_TM, S)
    ns = S // tm
    gs = jnp.tile(k_norm[1], NSA_KV_HEADS).reshape(1, w)
    gw = jnp.tile(k_norm[2], NSA_KV_HEADS).reshape(1, w)
    tab = pl.BlockSpec((tm, w), lambda i: (i % ns, 0))
    one = pl.BlockSpec((1, w), lambda i: (0, 0))
    return pl.pallas_call(
        _kv_prep_kernel,
        grid=(T // tm,),
        in_specs=[
            pl.BlockSpec((tm, 2 * w), lambda i: (i, 1)),
            pl.BlockSpec((tm, 2 * w), lambda i: (i, 2)),
            pl.BlockSpec((w, w), lambda i: (0, 0)),
            one, one, tab, tab, tab,
        ],
        out_specs=pl.BlockSpec((tm, 4 * w), lambda i: (i, 0)),
        out_shape=jax.ShapeDtypeStruct((T, 4 * w), BF16),
        compiler_params=_params("parallel"),
        name="kv_prep",
    )(kv, kv, _block_mean_matrix(w), gs, gw, *coeffs)


def _compress_kernel(x_ref, pos_ref, w1_ref, w2_ref, bd_ref, g_ref, c_ref, sa_ref, sb_ref,
                     o_ref, top_ref, bot_ref):
    t = pl.program_id(0)
    l = pl.program_id(1)
    half = CMP_BLOCK // 2

    @pl.when(l == 0)
    def _():
        top_ref[...] = jnp.zeros_like(top_ref)
        bot_ref[...] = jnp.zeros_like(bot_ref)

    x = x_ref[0]
    top_ref[...] += _dot((x + pos_ref[0, 0]).astype(BF16), w1_ref[0, 0])
    bot_ref[...] += _dot((x + pos_ref[0, 1]).astype(BF16), w1_ref[0, 1])

    @pl.when(l == half - 1)
    def _():
        rows = bot_ref.shape[0]
        hid = top_ref[...] + pltpu.roll(bot_ref[...], rows - 1, 0)
        act = (hid * _sigmoid(hid)).astype(BF16)
        y = _dot(act, w2_ref[0])
        yk = _rope(_head_rmsnorm(y, bd_ref[...], g_ref[...]), c_ref[...], sa_ref[...], sb_ref[...])
        o_ref[0] = jnp.where(t == 0, yk, y).astype(BF16)


def _compress(kv, B, S, pos_k, pos_v, k_w1, k_w2, v_w1, v_w2, k_gain):
    w = GROUP_LANES
    G = NSA_KV_HEADS
    half = CMP_BLOCK // 2
    nc = S // CMP_STRIDE
    rows = B * nc
    hid = k_w1.shape[1]
    raw = kv[:, :2 * w].reshape(B, nc, half, 2, w)
    xc = jnp.transpose(raw, (3, 2, 0, 1, 4)).reshape(2, half, rows, w)
    eye = jnp.eye(G, dtype=F32)

    def blockdiag(wm):
        out = jnp.einsum('gh,...ab->...gahb', eye, wm)
        return out.reshape(*wm.shape[:-2], G * wm.shape[-2], G * wm.shape[-1])

    def prep_w1(w1):
        w1 = w1.reshape(2, half, HEAD_DIM, hid)
        return jnp.transpose(blockdiag(w1), (1, 0, 2, 3))

    w1 = jnp.stack([prep_w1(k_w1), prep_w1(v_w1)]).astype(BF16)
    w2 = jnp.stack([blockdiag(k_w2), blockdiag(v_w2)]).astype(BF16)

    def prep_pos(p):
        p = jnp.tile(p, (1, G)).reshape(2, half, 1, w)
        return jnp.transpose(p, (1, 0, 2, 3))

    pos = jnp.stack([prep_pos(pos_k), prep_pos(pos_v)])
    c_end = jnp.arange(nc) * CMP_STRIDE + (CMP_BLOCK - 1)
    coeffs = tuple(jnp.tile(tb, (B, 1)) for tb in _rope_coeffs(c_end, G))
    gain = jnp.tile(k_gain, G).reshape(1, w)
    const2 = lambda t, l: (0, 0)
    out = pl.pallas_call(
        _compress_kernel,
        grid=(2, half),
        in_specs=[
            pl.BlockSpec((None, 1, rows, w), lambda t, l: (t, l, 0, 0)),
            pl.BlockSpec((None, 1, 2, 1, w), lambda t, l: (t, l, 0, 0, 0)),
            pl.BlockSpec((None, 1, 2, w, G * hid), lambda t, l: (t, l, 0, 0, 0)),
            pl.BlockSpec((1, G * hid, w), lambda t, l: (t, 0, 0)),
            pl.BlockSpec((w, w), const2),
            pl.BlockSpec((1, w), const2),
            pl.BlockSpec((rows, w), const2),
            pl.BlockSpec((rows, w), const2),
            pl.BlockSpec((rows, w), const2),
        ],
        out_specs=pl.BlockSpec((1, rows, w), lambda t, l: (t, 0, 0)),
        out_shape=jax.ShapeDtypeStruct((2, rows, w), BF16),
        scratch_shapes=[pltpu.VMEM((rows, G * hid), F32), pltpu.VMEM((rows, G * hid), F32)],
        compiler_params=_params("arbitrary", "arbitrary"),
        name="nsa_compress",
    )(xc, pos, w1, w2, _block_mean_matrix(w), gain, *coeffs)
    return out[0], out[1]


def _nsa_kernel_old(q_ref, gl_ref, kc_ref, vc_ref, ks_ref, vs_ref, kw_ref, vw_ref,
                bd_ref, qg_ref, c_ref, sa_ref, sb_ref, ov_ref,
                o_ref,
                kc_rep, vc_rep, ks_rep, vs_rep, kw_rep, vw_rep, m_ref, l_ref, acc_ref,
                *, tq, bk, nsb, scale):
    g = pl.program_id(1)
    i = pl.program_id(2)
    w = GROUP_LANES
    R = NSA_GROUP
    rows = R * tq
    lane_head = _iota((1, w), 1) >> 6

    @pl.when(i == 0)
    def _():
        rr, cc = _iota((w, w), 0), _iota((w, w), 1)
        pick = jnp.where(((rr >> 6) == g) & ((rr & 63) == (cc & 63)), 1.0, 0.0).astype(BF16)
        for src, dst in ((kc_ref, kc_rep), (vc_ref, vc_rep), (ks_ref, ks_rep),
                         (vs_ref, vs_rep), (kw_ref, kw_rep), (vw_ref, vw_rep)):
            dst[...] = _dot(src[...], pick).astype(BF16)

    qn = _rope(_head_rmsnorm(q_ref[...], bd_ref[...], qg_ref[...]), c_ref[...], sa_ref[...], sb_ref[...])
    qn = qn.astype(BF16)
    qm = jnp.concatenate(
        [jnp.where(lane_head == r, qn, jnp.zeros_like(qn)) for r in range(R)], axis=0)
    t_row = i * tq + (_iota((rows, 1), 0) & (tq - 1))

    def unstack(x):
        out = jnp.where(lane_head == 0, x[0:tq], 0.0)
        for r in range(1, R):
            out = out + jnp.where(lane_head == r, x[r * tq:(r + 1) * tq], 0.0)
        return out

    ncp = kc_rep.shape[0]
    s = _dot_nt(qm, kc_rep[...]) * scale
    c_end = _iota((1, ncp), 1) * CMP_STRIDE + (CMP_BLOCK - 1)
    valid = c_end <= t_row
    s = jnp.where(valid, s, NEG)
    e = jnp.exp(s - jnp.max(s, axis=1, keepdims=True))
    p = e / jnp.sum(e, axis=1, keepdims=True)
    p = jnp.where(t_row >= CMP_BLOCK - 1, p, 0.0)
    o_cmp = unstack(_dot(p.astype(BF16), vc_rep[...]))
    p_sum = p[0:tq]
    for r in range(1, R):
        p_sum = p_sum + p[r * tq:(r + 1) * tq]

    hi, lo = _split_bf16(p_sum)
    ov = ov_ref[...]
    imp = (_dot_nt(ov, hi) + _dot_nt(ov, lo))[0:nsb]
    jb = _iota((nsb, 1), 0)
    cur = (i * tq + _iota((1, tq), 1)) >> 6
    forced = (jb == 0) | ((cur - jb >= 0) & (cur - jb < N_LOCAL_SEL))
    score = jnp.where(forced, FORCE, jnp.where(jb <= cur, imp, -FORCE))
    cnt = jnp.zeros((nsb, tq), F32)
    for i2 in range(nsb):
        row = score[i2:i2 + 1, :]
        beats = (row > score) | ((row == score) & (jb > i2))
        cnt = cnt + jnp.where(beats, 1.0, 0.0)
    sel_t = jnp.where(cnt < float(min(N_SEL, nsb)), 1.0, 0.0)
    if nsb < LANES:
        sel_t = jnp.concatenate([sel_t, jnp.zeros((LANES - nsb, tq), F32)], axis=0)
    sel = sel_t.T.astype(BF16)

    def flash_block(k_rep, v_rep, jblk, ok_fn):
        off = pl.multiple_of(jblk * bk, bk)
        kb = k_rep[pl.ds(off, bk), :]
        vb = v_rep[pl.ds(off, bk), :]
        s_pos = jblk * bk + _iota((1, bk), 1)
        ok = ok_fn(s_pos, jblk)
        sc = jnp.where(ok, _dot_nt(qm, kb) * scale, NEG)
        m_old = m_ref[...]
        m_new = jnp.maximum(m_old, jnp.max(sc, axis=1, keepdims=True))
        alpha = jnp.exp(m_old - m_new)
        pe = jnp.where(ok, jnp.exp(sc - m_new), 0.0)
        l_ref[...] = alpha * l_ref[...] + jnp.sum(pe, axis=1, keepdims=True)
        acc_ref[...] = alpha * acc_ref[...] + _dot(pe.astype(BF16), vb)
        m_ref[...] = m_new

    def reset():
        m_ref[...] = jnp.full_like(m_ref, NEG)
        l_ref[...] = jnp.zeros_like(l_ref)
        acc_ref[...] = jnp.zeros_like(acc_ref)

    def sel_ok(s_pos, jblk):
        blk = (jblk * bk + _iota((LANES, bk), 1)) >> 6
        expand = jnp.where(_iota((LANES, bk), 0) == blk, 1.0, 0.0).astype(BF16)
        md = _dot(sel, expand)
        md = jnp.concatenate([md] * R, axis=0)
        return (md > 0.5) & (s_pos <= t_row)

    reset()

    def sel_body(jblk, carry):
        flash_block(ks_rep, vs_rep, jblk, sel_ok)
        return carry

    lax.fori_loop(0, (i * tq + tq) // bk, sel_body, 0)
    o_sel = unstack(acc_ref[...] / l_ref[...])

    def win_ok(s_pos, jblk):
        return (s_pos <= t_row) & (t_row - s_pos < WINDOW)

    reset()

    def win_body(jblk, carry):
        flash_block(kw_rep, vw_rep, jblk, win_ok)
        return carry

    lax.fori_loop(jnp.maximum(i * tq - (WINDOW - 1), 0) // bk, (i * tq + tq) // bk, win_body, 0)
    o_win = unstack(acc_ref[...] / l_ref[...])

    ghi, glo = _split_bf16(_sigmoid(gl_ref[...]))
    col = _iota((LANES, w), 0)
    out = jnp.zeros((tq, w), F32)
    for br, o_br in enumerate((o_cmp, o_sel, o_win)):
        target = br * (NSA_KV_HEADS * R) + g * R + (_iota((LANES, w), 1) >> 6)
        spread = jnp.where(col == target, 1.0, 0.0).astype(BF16)
        out = out + (_dot(ghi, spread) + _dot(glo, spread)) * o_br
    o_ref[...] = out.astype(o_ref.dtype)


MASK_BIG = 2.0 ** 30


def _nsa_kernel(q_ref, gl_ref, kc_ref, vc_ref, ks_ref, vs_ref, kw_ref, vw_ref,
                bd_ref, qg_ref, c_ref, sa_ref, sb_ref, ov_ref,
                o_ref,
                kc_a, vc_rep, ks_a, vs_rep, kw_a, vw_rep, s_scr, mx_scr, l_scr, acc_scr,
                *, tq, bk, nsb, scale):
    g = pl.program_id(1)
    i = pl.program_id(2)
    w = GROUP_LANES
    R = NSA_GROUP
    rows = R * tq
    lane = _iota((1, w), 1)
    lane_head = lane >> 6

    @pl.when(i == 0)
    def _():
        rr, cc = _iota((w, w), 0), _iota((w, w), 1)
        mine = ((rr >> 6) == g) & ((rr & 63) == (cc & 63))
        to_slot0 = jnp.where(mine & (cc < HEAD_DIM), 1.0, 0.0).astype(BF16)
        to_all = jnp.where(mine, 1.0, 0.0).astype(BF16)
        kc_a[...] = _dot(kc_ref[...], to_slot0).astype(BF16)
        kw_a[...] = _dot(kw_ref[...], to_slot0).astype(BF16)
        seq = ks_a.shape[0]
        tag = jnp.where(_iota((seq, w), 1) - HEAD_DIM == (_iota((seq, w), 0) >> 6), 1.0, 0.0)
        ks_a[...] = (_dot(ks_ref[...], to_slot0) + tag).astype(BF16)
        for src, dst in ((vc_ref, vc_rep), (vs_ref, vs_rep), (vw_ref, vw_rep)):
            dst[...] = _dot(src[...], to_all).astype(BF16)

    qn = _rope(_head_rmsnorm(q_ref[...], bd_ref[...], qg_ref[...]), c_ref[...], sa_ref[...], sb_ref[...])
    qn = qn * scale
    slot0 = lane < HEAD_DIM
    parts = []
    for r in range(R):
        qr = qn if r == 0 else pltpu.roll(qn, w - r * HEAD_DIM, 1)
        parts.append(jnp.where(slot0, qr, 0.0))
    qs_f32 = jnp.concatenate(parts, axis=0)
    qs = qs_f32.astype(BF16)
    t_row = i * tq + (_iota((rows, 1), 0) & (tq - 1))

    def unstack(x):
        out = jnp.where(lane_head == 0, x[0:tq], 0.0)
        for r in range(1, R):
            out = out + jnp.where(lane_head == r, x[r * tq:(r + 1) * tq], 0.0)
        return out

    def stack(x):
        return jnp.concatenate([x] * R, axis=0)

    ncp = kc_a.shape[0]
    s = _dot_nt(qs, kc_a[...])
    c_end = _iota((1, ncp), 1) * CMP_STRIDE + (CMP_BLOCK - 1)
    s = jnp.where(c_end <= t_row, s, NEG)
    e = jnp.exp(s - jnp.max(s, axis=1, keepdims=True))
    p = e / jnp.sum(e, axis=1, keepdims=True)
    p = jnp.where(t_row >= CMP_BLOCK - 1, p, 0.0)
    o_cmp = unstack(_dot(p.astype(BF16), vc_rep[...]))
    p_sum = p[0:tq]
    for r in range(1, R):
        p_sum = p_sum + p[r * tq:(r + 1) * tq]

    hi, lo = _split_bf16(p_sum)
    ov = ov_ref[...]
    imp = (_dot_nt(ov, hi) + _dot_nt(ov, lo))[0:nsb]
    jb = _iota((nsb, 1), 0)
    cur = (i * tq + _iota((1, tq), 1)) >> 6
    forced = (jb == 0) | ((cur - jb >= 0) & (cur - jb < N_LOCAL_SEL))
    score = jnp.where(forced, FORCE, jnp.where(jb <= cur, imp, -FORCE))
    cnt = jnp.zeros((nsb, tq), F32)
    for i2 in range(nsb):
        row = score[i2:i2 + 1, :]
        beats = (row > score) | ((row == score) & (jb > i2))
        cnt = cnt + jnp.where(beats, 1.0, 0.0)
    sel_t = jnp.where((cnt < float(min(N_SEL, nsb))) & (jb <= cur), 0.0, -MASK_BIG)
    pad_t = jnp.concatenate([jnp.zeros((HEAD_DIM, tq), F32), sel_t,
                             jnp.zeros((LANES - HEAD_DIM - nsb, tq), F32)], axis=0)
    aug = jnp.concatenate([pad_t.T, jnp.zeros((tq, w - LANES), F32)], axis=1)
    qsel = (qs_f32 + stack(aug)).astype(BF16)

    tri_ok = _iota((tq, bk), 1) <= _iota((tq, bk), 0)
    diag_bias = stack(jnp.where(tri_ok, 0.0, NEG))

    mx_scr[...] = jnp.full_like(mx_scr, NEG)

    def pass1(jblk, carry):
        off = pl.multiple_of(jblk * bk, bk)
        sc = _dot_nt(qsel, ks_a[pl.ds(off, bk), :])
        s_scr[jblk] = sc
        mx_scr[...] = jnp.maximum(mx_scr[...], sc)
        return carry

    lax.fori_loop(0, i, pass1, 0)
    off_d = pl.multiple_of(i * bk, bk)
    sc = _dot_nt(qsel, ks_a[pl.ds(off_d, bk), :]) + diag_bias
    s_scr[i] = sc
    m = jnp.max(jnp.maximum(mx_scr[...], sc), axis=1, keepdims=True)
    mx_scr[...] = jnp.broadcast_to(m, (rows, bk))
    l_scr[...] = jnp.zeros_like(l_scr)
    acc_scr[...] = jnp.zeros_like(acc_scr)

    def pass2(jblk, carry):
        off = pl.multiple_of(jblk * bk, bk)
        pe = jnp.exp(s_scr[jblk] - mx_scr[...])
        l_scr[...] += pe
        acc_scr[...] += _dot(pe.astype(BF16), vs_rep[pl.ds(off, bk), :])
        return carry

    lax.fori_loop(0, i + 1, pass2, 0)
    o_sel = unstack(acc_scr[...] / jnp.sum(l_scr[...], axis=1, keepdims=True))

    far_bias = jnp.where(tri_ok, NEG, 0.0)
    scs, offs = [], []
    for d in (2, 1, 0):
        jblk = i - d
        off = pl.multiple_of(jnp.maximum(jblk, 0) * bk, bk)
        sc = _dot_nt(qs, kw_a[pl.ds(off, bk), :])
        missing = jnp.where(jblk >= 0, 0.0, NEG)
        if d == 2:
            sc = sc + stack(far_bias + missing)
        elif d == 1:
            sc = sc + missing
        else:
            sc = sc + diag_bias
        scs.append(sc)
        offs.append(off)
    m = jnp.max(jnp.maximum(jnp.maximum(scs[0], scs[1]), scs[2]), axis=1, keepdims=True)
    mb = jnp.broadcast_to(m, (rows, bk))
    pes = [jnp.exp(sc - mb) for sc in scs]
    l_win = jnp.sum(pes[0] + pes[1] + pes[2], axis=1, keepdims=True)
    acc = _dot(pes[0].astype(BF16), vw_rep[pl.ds(offs[0], bk), :])
    acc = acc + _dot(pes[1].astype(BF16), vw_rep[pl.ds(offs[1], bk), :])
    acc = acc + _dot(pes[2].astype(BF16), vw_rep[pl.ds(offs[2], bk), :])
    o_win = unstack(acc / l_win)

    ghi, glo = _split_bf16(_sigmoid(gl_ref[...]))
    col = _iota((LANES, w), 0)
    out = jnp.zeros((tq, w), F32)
    for br, o_br in enumerate((o_cmp, o_sel, o_win)):
        target = br * (NSA_KV_HEADS * R) + g * R + (_iota((LANES, w), 1) >> 6)
        spread = jnp.where(col == target, 1.0, 0.0).astype(BF16)
        out = out + (_dot(ghi, spread) + _dot(glo, spread)) * o_br
    o_ref[...] = out.astype(o_ref.dtype)


def _nsa_attention(proj, kc, vc, ksw, q_gain, coeffs, B, S):
    T = B * S
    w = GROUP_LANES
    G = NSA_KV_HEADS
    tq, bk = ATT_TQ, ATT_BK
    nq = S // tq
    nc = S // CMP_STRIDE
    nsb = S // SEL_BLOCK
    n_cmp = (S - CMP_BLOCK) // CMP_STRIDE + 1
    cs = np.arange(nc) * CMP_STRIDE
    js = np.arange(LANES) * SEL_BLOCK
    ov = ((cs[None, :] < js[:, None] + SEL_BLOCK) & (cs[None, :] + CMP_BLOCK > js[:, None])
          & (np.arange(nc)[None, :] < n_cmp) & (np.arange(LANES)[:, None] < nsb))
    ov = jnp.asarray(ov.astype(np.float32), dtype=BF16)
    gate_blk = (G * NSA_GROUP * HEAD_DIM) // LANES
    assert tq == bk and WINDOW == 2 * bk and HEAD_DIM + nsb <= LANES
    kern = functools.partial(_nsa_kernel, tq=tq, bk=bk, nsb=nsb, scale=HEAD_DIM ** -0.5)
    seq = lambda col: pl.BlockSpec((S, w), lambda b, g, i: (b, col))
    const = lambda shape: pl.BlockSpec(shape, lambda b, g, i: (0, 0))
    tab = pl.BlockSpec((tq, w), lambda b, g, i: (i, 0))
    rows = NSA_GROUP * tq
    return pl.pallas_call(
        kern,
        grid=(B, G, nq),
        in_specs=[
            pl.BlockSpec((tq, w), lambda b, g, i: (b * nq + i, g)),
            pl.BlockSpec((tq, LANES), lambda b, g, i: (b * nq + i, gate_blk)),
            pl.BlockSpec((nc, w), lambda b, g, i: (b, 0)),
            pl.BlockSpec((nc, w), lambda b, g, i: (b, 0)),
            seq(0), seq(1), seq(2), seq(3),
            const((w, w)), const((1, w)), tab, tab, tab, const((LANES, nc)),
        ],
        out_specs=pl.BlockSpec((tq, w), lambda b, g, i: (b * nq + i, g)),
        out_shape=jax.ShapeDtypeStruct((T, G * w), BF16),
        scratch_shapes=[
            pltpu.VMEM((nc, w), BF16), pltpu.VMEM((nc, w), BF16),
            pltpu.VMEM((S, w), BF16), pltpu.VMEM((S, w), BF16),
            pltpu.VMEM((S, w), BF16), pltpu.VMEM((S, w), BF16),
            pltpu.VMEM((S // bk, rows, bk), F32),
            pltpu.VMEM((rows, bk), F32), pltpu.VMEM((rows, bk), F32), pltpu.VMEM((rows, w), F32),
        ],
        compiler_params=_params("parallel", "arbitrary", "arbitrary"),
        name="nsa_attn",
    )(proj, proj, kc, vc, ksw, ksw, ksw, ksw,
      _block_mean_matrix(w), jnp.tile(q_gain, NSA_GROUP).reshape(1, w), *coeffs, ov)


def kernel(x, ffn1_norm, ffn1_w_in, ffn1_w_out, mix_norm, ffn2_norm, ffn2_w_in, ffn2_w_out,
           sb_w_qkv, sb_w_out, kv_norm, nsa_w_kv, nsa_k_norm, cmp_pos_k, cmp_pos_v,
           cmp_k_w1, cmp_k_w2, cmp_v_w1, cmp_v_w2, nsa_w_in, nsa_q_norm, nsa_w_out):
    B, S, D = x.shape
    depth = ffn1_norm.shape[0]
    n_a = sb_w_qkv.shape[0]
    n_sb_heads = sb_w_out.shape[1] // HEAD_DIM
    assert S % 512 == 0 and S // CMP_STRIDE == LANES, "sequence tiling assumes 16*128 tokens"
    h = x.reshape(B * S, D)
    bf = lambda a: a.astype(BF16)
    coeffs = _rope_coeffs(jnp.arange(S), NSA_KV_HEADS)
    n_q = nsa_w_in.shape[-1]
    n_q_pad = -(-n_q // LANES) * LANES
    kc = vc = ksw = None
    for layer in range(depth):
        h = _ffn(h, ffn1_norm[layer], bf(ffn1_w_in[layer]), bf(ffn1_w_out[layer]))
        if layer < n_a:
            qkv = _norm_proj(h, mix_norm[layer], bf(sb_w_qkv[layer]), 1024, BF16)
            o = _sb_attention(qkv, B, S, n_sb_heads)
            h = _out_proj(h, o, bf(sb_w_out[layer]))
        else:
            li = layer - n_a
            w_in = jnp.pad(bf(nsa_w_in[li]), ((0, 0), (0, n_q_pad - n_q)))
            proj = _norm_proj(h, mix_norm[layer], w_in, n_q_pad, F32)
            o = _nsa_attention(proj, kc, vc, ksw, nsa_q_norm[li], coeffs, B, S)
            h = _out_proj(h, o, bf(nsa_w_out[li]))
        h = _ffn(h, ffn2_norm[layer], bf(ffn2_w_in[layer]), bf(ffn2_w_out[layer]))
        if layer == n_a - 1:
            kv = _norm_proj(h, kv_norm, bf(nsa_w_kv), 768, F32)
            kc, vc = _compress(kv, B, S, cmp_pos_k, cmp_pos_v, cmp_k_w1, cmp_k_w2,
                               cmp_v_w1, cmp_v_w2, nsa_k_norm[0])
            ksw = _kv_prep(kv, nsa_k_norm, coeffs, S)
    return h.reshape(B, S, D)
```

```python
import functools

import jax
import jax.numpy as jnp
import numpy as np
from jax import lax
from jax.experimental import pallas as pl
from jax.experimental.pallas import tpu as pltpu

F32 = jnp.float32
BF16 = jnp.bfloat16

HEAD_DIM = 64
NSA_KV_HEADS = 4
NSA_GROUP = 4
N_BRANCHES = 3
ROPE_DIMS = 16
ROPE_THETA = 500000.0
CMP_BLOCK = 32
CMP_STRIDE = 16
SEL_BLOCK = 64
N_SEL = 8
N_LOCAL_SEL = 2
WINDOW = 512
EPS = 1e-6
NEG = -1e30
FORCE = 1e4

LANES = 128
GROUP_LANES = NSA_GROUP * HEAD_DIM
VMEM_LIMIT_BYTES = 56 * 1024 * 1024

FFN_TM = 512
PROJ_TM = 1024
ATT_TQ = 256
ATT_BK = 256


def _dot(a, b):
    return jnp.dot(a, b, preferred_element_type=F32)


def _dot_nt(a, b):
    return lax.dot_general(a, b, (((1,), (1,)), ((), ())), preferred_element_type=F32)


def _split_bf16(x):
    hi = x.astype(BF16)
    lo = (x - hi.astype(F32)).astype(BF16)
    return hi, lo


def _dot_split(x, w):
    hi, lo = _split_bf16(x)
    n = x.shape[0]
    r = _dot(jnp.concatenate([hi, lo], axis=0), w)
    return r[0:n] + r[n:2 * n]


def _sigmoid(x):
    return 1.0 / (1.0 + jnp.exp(-x))


def _iota(shape, dim):
    return lax.broadcasted_iota(jnp.int32, shape, dim)


def _params(*sem):
    return pltpu.CompilerParams(dimension_semantics=sem, vmem_limit_bytes=VMEM_LIMIT_BYTES)


def _ffn_kernel(h_ref, g_ref, wa_ref, wb_ref, wo_ref, o_ref, xn_ref, acc_ref):
    j = pl.program_id(1)

    @pl.when(j == 0)
    def _():
        x = h_ref[...]
        ms = jnp.mean(x * x, axis=-1, keepdims=True)
        xn_ref[...] = (x * lax.rsqrt(ms + EPS) * g_ref[...]).astype(BF16)
        acc_ref[...] = jnp.zeros_like(acc_ref)

    xn = xn_ref[...]
    a = _dot(xn, wa_ref[...])
    b = _dot(xn, wb_ref[...])
    act = (a * _sigmoid(a) * b).astype(BF16)
    acc_ref[...] += _dot(act, wo_ref[...])

    @pl.when(j == pl.num_programs(1) - 1)
    def _():
        o_ref[...] = h_ref[...] + 0.5 * acc_ref[...]


def _ffn(h, gain, w_in, w_out):
    T, D = h.shape
    F = w_out.shape[0]
    tm = min(FFN_TM, T)
    tf = F // 2 if (F // 2) % LANES == 0 else F
    nf = F // tf
    return pl.pallas_call(
        _ffn_kernel,
        grid=(T // tm, nf),
        in_specs=[
            pl.BlockSpec((tm, D), lambda i, j: (i, 0)),
            pl.BlockSpec((1, D), lambda i, j: (0, 0)),
            pl.BlockSpec((D, tf), lambda i, j: (0, j)),
            pl.BlockSpec((D, tf), lambda i, j: (0, j + nf)),
            pl.BlockSpec((tf, D), lambda i, j: (j, 0)),
        ],
        out_specs=pl.BlockSpec((tm, D), lambda i, j: (i, 0)),
        out_shape=jax.ShapeDtypeStruct((T, D), F32),
        scratch_shapes=[pltpu.VMEM((tm, D), BF16), pltpu.VMEM((tm, D), F32)],
        compiler_params=_params("parallel", "arbitrary"),
        name="ffn",
    )(h, gain.reshape(1, D), w_in, w_in, w_out)


def _norm_proj_kernel(h_ref, g_ref, w_ref, o_ref, xn_ref):
    @pl.when(pl.program_id(1) == 0)
    def _():
        x = h_ref[...]
        ms = jnp.mean(x * x, axis=-1, keepdims=True)
        xn_ref[...] = (x * lax.rsqrt(ms + EPS) * g_ref[...]).astype(BF16)

    o_ref[...] = _dot(xn_ref[...], w_ref[...]).astype(o_ref.dtype)


def _norm_proj(h, gain, w, tn, out_dtype):
    T, D = h.shape
    N = w.shape[1]
    tm = min(PROJ_TM, T)
    return pl.pallas_call(
        _norm_proj_kernel,
        grid=(T // tm, N // tn),
        in_specs=[
            pl.BlockSpec((tm, D), lambda i, j: (i, 0)),
            pl.BlockSpec((1, D), lambda i, j: (0, 0)),
            pl.BlockSpec((D, tn), lambda i, j: (0, j)),
        ],
        out_specs=pl.BlockSpec((tm, tn), lambda i, j: (i, j)),
        out_shape=jax.ShapeDtypeStruct((T, N), out_dtype),
        scratch_shapes=[pltpu.VMEM((tm, D), BF16)],
        compiler_params=_params("parallel", "arbitrary"),
        name="norm_proj",
    )(h, gain.reshape(1, D), w)


def _out_proj_kernel(h_ref, x_ref, w_ref, o_ref):
    o_ref[...] = h_ref[...] + _dot(x_ref[...], w_ref[...])


def _out_proj(h, x, w):
    T, D = h.shape
    K = x.shape[1]
    tm = min(PROJ_TM, T)
    return pl.pallas_call(
        _out_proj_kernel,
        grid=(T // tm,),
        in_specs=[
            pl.BlockSpec((tm, D), lambda i: (i, 0)),
            pl.BlockSpec((tm, K), lambda i: (i, 0)),
            pl.BlockSpec((K, D), lambda i: (0, 0)),
        ],
        out_specs=pl.BlockSpec((tm, D), lambda i: (i, 0)),
        out_shape=jax.ShapeDtypeStruct((T, D), F32),
        compiler_params=_params("parallel"),
        name="out_proj",
    )(h, x, w)


def _sb_kernel(q_ref, k_ref, v_ref, o_ref, acc_ref, c_ref, *, tq, bk, scale):
    i = pl.program_id(2)
    lane = _iota((1, LANES), 1)
    is_a = lane < HEAD_DIM
    q = q_ref[...] * scale
    zero = jnp.zeros_like(q)
    qh = (jnp.where(is_a, q, zero), jnp.where(is_a, zero, q))
    tri = jnp.where(_iota((bk, bk), 0) > _iota((bk, bk), 1), 1.0, 0.0).astype(BF16)
    acc_ref[...] = jnp.zeros_like(acc_ref)
    c_ref[...] = jnp.zeros_like(c_ref)

    def sweep(blocks, diagonal=False):
        chains = [(b, hd) for b in range(len(blocks)) for hd in range(2)]
        offs = [pl.multiple_of(j * bk, bk) for j in blocks]
        ks = [k_ref[pl.ds(off, bk), :] for off in offs]
        vs = [v_ref[pl.ds(off, bk), :] for off in offs]
        past = _iota((tq, bk), 1) < _iota((tq, bk), 0)
        z = {c: _dot_nt(qh[c[1]], ks[c[0]]) for c in chains}
        log_b, log_1m, parts = {}, {}, {}
        for c in chains:
            log_b[c] = jnp.minimum(z[c], 0.0) - jnp.log(1.0 + jnp.exp(-jnp.abs(z[c])))
            m = log_b[c] - z[c]
            log_1m[c] = jnp.where(past, m, 0.0) if (diagonal and c[0] == 0) else m
            parts[c] = _split_bf16(log_1m[c])
        within = {c: _dot(parts[c][0], tri) + _dot(parts[c][1], tri) for c in chains}
        carry = [c_ref[0:tq, :], c_ref[tq:2 * tq, :]]
        w = {}
        for c in chains:
            w[c] = jnp.exp(log_b[c] + (carry[c[1]] + within[c]))
            if diagonal and c[0] == 0:
                w[c] = jnp.where(past, w[c], 0.0)
            carry[c[1]] = carry[c[1]] + jnp.sum(log_1m[c], axis=1, keepdims=True)
        for hd in range(2):
            rws = slice(hd * tq, (hd + 1) * tq)
            upd = acc_ref[rws, :]
            for b in range(len(blocks)):
                upd = upd + _dot(w[(b, hd)].astype(BF16), vs[b])
            acc_ref[rws, :] = upd
            c_ref[rws, :] = carry[hd]

    odd_i = i % 2

    @pl.when(odd_i == 1)
    def _():
        sweep([i, i - 1], diagonal=True)

    @pl.when(odd_i == 0)
    def _():
        sweep([i], diagonal=True)

    def body(jj, carry):
        j = i - 1 - odd_i - 2 * jj
        sweep([j, j - 1])
        return carry

    lax.fori_loop(0, i // 2, body, 0)

    acc = acc_ref[...]
    o_ref[...] = jnp.where(is_a, acc[0:tq], acc[tq:2 * tq]).astype(o_ref.dtype)


def _sb_attention(qkv, B, S, n_heads):
    T = B * S
    tq, bk = ATT_TQ, ATT_BK
    nq = S // tq
    n_pairs = n_heads * HEAD_DIM // LANES
    kern = functools.partial(_sb_kernel, tq=tq, bk=bk, scale=HEAD_DIM ** -0.5)
    return pl.pallas_call(
        kern,
        grid=(B, n_pairs, nq),
        in_specs=[
            pl.BlockSpec((tq, LANES), lambda b, p, i: (b * nq + i, p)),
            pl.BlockSpec((S, LANES), lambda b, p, i: (b, n_pairs + p)),
            pl.BlockSpec((S, LANES), lambda b, p, i: (b, 2 * n_pairs + p)),
        ],
        out_specs=pl.BlockSpec((tq, LANES), lambda b, p, i: (b * nq + i, p)),
        out_shape=jax.ShapeDtypeStruct((T, n_heads * HEAD_DIM), BF16),
        scratch_shapes=[pltpu.VMEM((2 * tq, LANES), F32), pltpu.VMEM((2 * tq, 1), F32)],
        compiler_params=_params("parallel", "parallel", "arbitrary"),
        name="sb_attn",
    )(qkv, qkv, qkv)


def _head_rmsnorm(x, bd, gain):
    ms = _dot_split(x * x, bd)
    return x * lax.rsqrt(ms + EPS) * gain


def _rope(y, c, sa, sb):
    w = y.shape[-1]
    half = ROPE_DIMS // 2
    return y * c + pltpu.roll(y, w - half, 1) * sa + pltpu.roll(y, half, 1) * sb


def _rope_coeffs(positions, n_heads):
    inv_freq = jnp.power(ROPE_THETA, -jnp.arange(0, ROPE_DIMS, 2, dtype=F32) / ROPE_DIMS)
    ang = positions.astype(F32)[:, None] * inv_freq[None, :]
    cos, sin = jnp.cos(ang), jnp.sin(ang)
    p = positions.shape[0]
    rest = HEAD_DIM - ROPE_DIMS
    zeros_h = jnp.zeros((p, ROPE_DIMS // 2), F32)
    c = jnp.concatenate([cos, cos, jnp.ones((p, rest), F32)], axis=-1)
    sa = jnp.concatenate([-sin, zeros_h, jnp.zeros((p, rest), F32)], axis=-1)
    sb = jnp.concatenate([zeros_h, sin, jnp.zeros((p, rest), F32)], axis=-1)
    return tuple(jnp.tile(t, (1, n_heads)) for t in (c, sa, sb))


def _block_mean_matrix(width):
    idx = np.arange(width) // HEAD_DIM
    return jnp.asarray((idx[:, None] == idx[None, :]).astype(np.float32) / HEAD_DIM, dtype=BF16)


def _kv_prep_kernel(sel_ref, win_ref, bd_ref, gs_ref, gw_ref, c_ref, sa_ref, sb_ref, o_ref):
    w = GROUP_LANES
    bd = bd_ref[...]
    c, sa, sb = c_ref[...], sa_ref[...], sb_ref[...]
    ks = _rope(_head_rmsnorm(sel_ref[:, :w], bd, gs_ref[...]), c, sa, sb)
    kw = _rope(_head_rmsnorm(win_ref[:, :w], bd, gw_ref[...]), c, sa, sb)
    o_ref[:, 0 * w:1 * w] = ks.astype(BF16)
    o_ref[:, 1 * w:2 * w] = sel_ref[:, w:].astype(BF16)
    o_ref[:, 2 * w:3 * w] = kw.astype(BF16)
    o_ref[:, 3 * w:4 * w] = win_ref[:, w:].astype(BF16)


def _kv_prep(kv, k_norm, coeffs, S):
    T = kv.shape[0]
    w = GROUP_LANES
    tm = min(PROJ_TM, S)
    ns = S // tm
    gs = jnp.tile(k_norm[1], NSA_KV_HEADS).reshape(1, w)
    gw = jnp.tile(k_norm[2], NSA_KV_HEADS).reshape(1, w)
    tab = pl.BlockSpec((tm, w), lambda i: (i % ns, 0))
    one = pl.BlockSpec((1, w), lambda i: (0, 0))
    return pl.pallas_call(
        _kv_prep_kernel,
        grid=(T // tm,),
        in_specs=[
            pl.BlockSpec((tm, 2 * w), lambda i: (i, 1)),
            pl.BlockSpec((tm, 2 * w), lambda i: (i, 2)),
            pl.BlockSpec((w, w), lambda i: (0, 0)),
            one, one, tab, tab, tab,
        ],
        out_specs=pl.BlockSpec((tm, 4 * w), lambda i: (i, 0)),
        out_shape=jax.ShapeDtypeStruct((T, 4 * w), BF16),
        compiler_params=_params("parallel"),
        name="kv_prep",
    )(kv, kv, _block_mean_matrix(w), gs, gw, *coeffs)


def _compress_kernel(x_ref, pos_ref, w1_ref, w2_ref, bd_ref, g_ref, c_ref, sa_ref, sb_ref,
                     o_ref, top_ref, bot_ref):
    t = pl.program_id(0)
    l = pl.program_id(1)
    half = CMP_BLOCK // 2

    @pl.when(l == 0)
    def _():
        top_ref[...] = jnp.zeros_like(top_ref)
        bot_ref[...] = jnp.zeros_like(bot_ref)

    x = x_ref[0]
    top_ref[...] += _dot((x + pos_ref[0, 0]).astype(BF16), w1_ref[0, 0])
    bot_ref[...] += _dot((x + pos_ref[0, 1]).astype(BF16), w1_ref[0, 1])

    @pl.when(l == half - 1)
    def _():
        rows = bot_ref.shape[0]
        hid = top_ref[...] + pltpu.roll(bot_ref[...], rows - 1, 0)
        act = (hid * _sigmoid(hid)).astype(BF16)
        y = _dot(act, w2_ref[0])
        yk = _rope(_head_rmsnorm(y, bd_ref[...], g_ref[...]), c_ref[...], sa_ref[...], sb_ref[...])
        o_ref[0] = jnp.where(t == 0, yk, y).astype(BF16)


def _compress(kv, B, S, pos_k, pos_v, k_w1, k_w2, v_w1, v_w2, k_gain):
    w = GROUP_LANES
    G = NSA_KV_HEADS
    half = CMP_BLOCK // 2
    nc = S // CMP_STRIDE
    rows = B * nc
    hid = k_w1.shape[1]
    raw = kv[:, :2 * w].reshape(B, nc, half, 2, w)
    xc = jnp.transpose(raw, (3, 2, 0, 1, 4)).reshape(2, half, rows, w)
    eye = jnp.eye(G, dtype=F32)

    def blockdiag(wm):
        out = jnp.einsum('gh,...ab->...gahb', eye, wm)
        return out.reshape(*wm.shape[:-2], G * wm.shape[-2], G * wm.shape[-1])

    def prep_w1(w1):
        w1 = w1.reshape(2, half, HEAD_DIM, hid)
        return jnp.transpose(blockdiag(w1), (1, 0, 2, 3))

    w1 = jnp.stack([prep_w1(k_w1), prep_w1(v_w1)]).astype(BF16)
    w2 = jnp.stack([blockdiag(k_w2), blockdiag(v_w2)]).astype(BF16)

    def prep_pos(p):
        p = jnp.tile(p, (1, G)).reshape(2, half, 1, w)
        return jnp.transpose(p, (1, 0, 2, 3))

    pos = jnp.stack([prep_pos(pos_k), prep_pos(pos_v)])
    c_end = jnp.arange(nc) * CMP_STRIDE + (CMP_BLOCK - 1)
    coeffs = tuple(jnp.tile(tb, (B, 1)) for tb in _rope_coeffs(c_end, G))
    gain = jnp.tile(k_gain, G).reshape(1, w)
    const2 = lambda t, l: (0, 0)
    out = pl.pallas_call(
        _compress_kernel,
        grid=(2, half),
        in_specs=[
            pl.BlockSpec((None, 1, rows, w), lambda t, l: (t, l, 0, 0)),
            pl.BlockSpec((None, 1, 2, 1, w), lambda t, l: (t, l, 0, 0, 0)),
            pl.BlockSpec((None, 1, 2, w, G * hid), lambda t, l: (t, l, 0, 0, 0)),
            pl.BlockSpec((1, G * hid, w), lambda t, l: (t, 0, 0)),
            pl.BlockSpec((w, w), const2),
            pl.BlockSpec((1, w), const2),
            pl.BlockSpec((rows, w), const2),
            pl.BlockSpec((rows, w), const2),
            pl.BlockSpec((rows, w), const2),
        ],
        out_specs=pl.BlockSpec((1, rows, w), lambda t, l: (t, 0, 0)),
        out_shape=jax.ShapeDtypeStruct((2, rows, w), BF16),
        scratch_shapes=[pltpu.VMEM((rows, G * hid), F32), pltpu.VMEM((rows, G * hid), F32)],
        compiler_params=_params("arbitrary", "arbitrary"),
        name="nsa_compress",
    )(xc, pos, w1, w2, _block_mean_matrix(w), gain, *coeffs)
    return out[0], out[1]


def _nsa_kernel_old(q_ref, gl_ref, kc_ref, vc_ref, ks_ref, vs_ref, kw_ref, vw_ref,
                bd_ref, qg_ref, c_ref, sa_ref, sb_ref, ov_ref,
                o_ref,
                kc_rep, vc_rep, ks_rep, vs_rep, kw_rep, vw_rep, m_ref, l_ref, acc_ref,
                *, tq, bk, nsb, scale):
    g = pl.program_id(1)
    i = pl.program_id(2)
    w = GROUP_LANES
    R = NSA_GROUP
    rows = R * tq
    lane_head = _iota((1, w), 1) >> 6

    @pl.when(i == 0)
    def _():
        rr, cc = _iota((w, w), 0), _iota((w, w), 1)
        pick = jnp.where(((rr >> 6) == g) & ((rr & 63) == (cc & 63)), 1.0, 0.0).astype(BF16)
        for src, dst in ((kc_ref, kc_rep), (vc_ref, vc_rep), (ks_ref, ks_rep),
                         (vs_ref, vs_rep), (kw_ref, kw_rep), (vw_ref, vw_rep)):
            dst[...] = _dot(src[...], pick).astype(BF16)

    qn = _rope(_head_rmsnorm(q_ref[...], bd_ref[...], qg_ref[...]), c_ref[...], sa_ref[...], sb_ref[...])
    qn = qn.astype(BF16)
    qm = jnp.concatenate(
        [jnp.where(lane_head == r, qn, jnp.zeros_like(qn)) for r in range(R)], axis=0)
    t_row = i * tq + (_iota((rows, 1), 0) & (tq - 1))

    def unstack(x):
        out = jnp.where(lane_head == 0, x[0:tq], 0.0)
        for r in range(1, R):
            out = out + jnp.where(lane_head == r, x[r * tq:(r + 1) * tq], 0.0)
        return out

    ncp = kc_rep.shape[0]
    s = _dot_nt(qm, kc_rep[...]) * scale
    c_end = _iota((1, ncp), 1) * CMP_STRIDE + (CMP_BLOCK - 1)
    valid = c_end <= t_row
    s = jnp.where(valid, s, NEG)
    e = jnp.exp(s - jnp.max(s, axis=1, keepdims=True))
    p = e / jnp.sum(e, axis=1, keepdims=True)
    p = jnp.where(t_row >= CMP_BLOCK - 1, p, 0.0)
    o_cmp = unstack(_dot(p.astype(BF16), vc_rep[...]))
    p_sum = p[0:tq]
    for r in range(1, R):
        p_sum = p_sum + p[r * tq:(r + 1) * tq]

    hi, lo = _split_bf16(p_sum)
    ov = ov_ref[...]
    imp = (_dot_nt(ov, hi) + _dot_nt(ov, lo))[0:nsb]
    jb = _iota((nsb, 1), 0)
    cur = (i * tq + _iota((1, tq), 1)) >> 6
    forced = (jb == 0) | ((cur - jb >= 0) & (cur - jb < N_LOCAL_SEL))
    score = jnp.where(forced, FORCE, jnp.where(jb <= cur, imp, -FORCE))
    cnt = jnp.zeros((nsb, tq), F32)
    for i2 in range(nsb):
        row = score[i2:i2 + 1, :]
        beats = (row > score) | ((row == score) & (jb > i2))
        cnt = cnt + jnp.where(beats, 1.0, 0.0)
    sel_t = jnp.where(cnt < float(min(N_SEL, nsb)), 1.0, 0.0)
    if nsb < LANES:
        sel_t = jnp.concatenate([sel_t, jnp.zeros((LANES - nsb, tq), F32)], axis=0)
    sel = sel_t.T.astype(BF16)

    def flash_block(k_rep, v_rep, jblk, ok_fn):
        off = pl.multiple_of(jblk * bk, bk)
        kb = k_rep[pl.ds(off, bk), :]
        vb = v_rep[pl.ds(off, bk), :]
        s_pos = jblk * bk + _iota((1, bk), 1)
        ok = ok_fn(s_pos, jblk)
        sc = jnp.where(ok, _dot_nt(qm, kb) * scale, NEG)
        m_old = m_ref[...]
        m_new = jnp.maximum(m_old, jnp.max(sc, axis=1, keepdims=True))
        alpha = jnp.exp(m_old - m_new)
        pe = jnp.where(ok, jnp.exp(sc - m_new), 0.0)
        l_ref[...] = alpha * l_ref[...] + jnp.sum(pe, axis=1, keepdims=True)
        acc_ref[...] = alpha * acc_ref[...] + _dot(pe.astype(BF16), vb)
        m_ref[...] = m_new

    def reset():
        m_ref[...] = jnp.full_like(m_ref, NEG)
        l_ref[...] = jnp.zeros_like(l_ref)
        acc_ref[...] = jnp.zeros_like(acc_ref)

    def sel_ok(s_pos, jblk):
        blk = (jblk * bk + _iota((LANES, bk), 1)) >> 6
        expand = jnp.where(_iota((LANES, bk), 0) == blk, 1.0, 0.0).astype(BF16)
        md = _dot(sel, expand)
        md = jnp.concatenate([md] * R, axis=0)
        return (md > 0.5) & (s_pos <= t_row)

    reset()

    def sel_body(jblk, carry):
        flash_block(ks_rep, vs_rep, jblk, sel_ok)
        return carry

    lax.fori_loop(0, (i * tq + tq) // bk, sel_body, 0)
    o_sel = unstack(acc_ref[...] / l_ref[...])

    def win_ok(s_pos, jblk):
        return (s_pos <= t_row) & (t_row - s_pos < WINDOW)

    reset()

    def win_body(jblk, carry):
        flash_block(kw_rep, vw_rep, jblk, win_ok)
        return carry

    lax.fori_loop(jnp.maximum(i * tq - (WINDOW - 1), 0) // bk, (i * tq + tq) // bk, win_body, 0)
    o_win = unstack(acc_ref[...] / l_ref[...])

    ghi, glo = _split_bf16(_sigmoid(gl_ref[...]))
    col = _iota((LANES, w), 0)
    out = jnp.zeros((tq, w), F32)
    for br, o_br in enumerate((o_cmp, o_sel, o_win)):
        target = br * (NSA_KV_HEADS * R) + g * R + (_iota((LANES, w), 1) >> 6)
        spread = jnp.where(col == target, 1.0, 0.0).astype(BF16)
        out = out + (_dot(ghi, spread) + _dot(glo, spread)) * o_br
    o_ref[...] = out.astype(o_ref.dtype)


MASK_BIG = 2.0 ** 30


def _nsa_kernel(q_ref, gl_ref, kc_ref, vc_ref, ks_ref, vs_ref, kw_ref, vw_ref,
                bd_ref, qg_ref, c_ref, sa_ref, sb_ref, ov_ref,
                o_ref,
                kc_a, vc_rep, ks_a, vs_rep, kw_a, vw_rep, s_scr, mx_scr, l_scr, acc_scr,
                *, tq, bk, nsb, scale):
    g = pl.program_id(1)
    i = pl.program_id(2)
    w = GROUP_LANES
    R = NSA_GROUP
    rows = R * tq
    lane = _iota((1, w), 1)
    lane_head = lane >> 6

    @pl.when(i == 0)
    def _():
        rr, cc = _iota((w, w), 0), _iota((w, w), 1)
        mine = ((rr >> 6) == g) & ((rr & 63) == (cc & 63))
        to_slot0 = jnp.where(mine & (cc < HEAD_DIM), 1.0, 0.0).astype(BF16)
        to_all = jnp.where(mine, 1.0, 0.0).astype(BF16)
        kc_a[...] = _dot(kc_ref[...], to_slot0).astype(BF16)
        kw_a[...] = _dot(kw_ref[...], to_slot0).astype(BF16)
        seq = ks_a.shape[0]
        tag = jnp.where(_iota((seq, w), 1) - HEAD_DIM == (_iota((seq, w), 0) >> 6), 1.0, 0.0)
        ks_a[...] = (_dot(ks_ref[...], to_slot0) + tag).astype(BF16)
        for src, dst in ((vc_ref, vc_rep), (vs_ref, vs_rep), (vw_ref, vw_rep)):
            dst[...] = _dot(src[...], to_all).astype(BF16)

    qn = _rope(_head_rmsnorm(q_ref[...], bd_ref[...], qg_ref[...]), c_ref[...], sa_ref[...], sb_ref[...])
    qn = qn * scale
    slot0 = lane < HEAD_DIM
    parts = []
    for r in range(R):
        qr = qn if r == 0 else pltpu.roll(qn, w - r * HEAD_DIM, 1)
        parts.append(jnp.where(slot0, qr, 0.0))
    qs_f32 = jnp.concatenate(parts, axis=0)
    qs = qs_f32.astype(BF16)
    t_row = i * tq + (_iota((rows, 1), 0) & (tq - 1))

    def unstack(x):
        out = jnp.where(lane_head == 0, x[0:tq], 0.0)
        for r in range(1, R):
            out = out + jnp.where(lane_head == r, x[r * tq:(r + 1) * tq], 0.0)
        return out

    def stack(x):
        return jnp.concatenate([x] * R, axis=0)

    tri_ok = _iota((tq, bk), 1) <= _iota((tq, bk), 0)
    diag_bias = stack(jnp.where(tri_ok, 0.0, NEG))

    far_bias = jnp.where(tri_ok, NEG, 0.0)
    scs, offs = [], []
    for d in (2, 1, 0):
        jblk = i - d
        off = pl.multiple_of(jnp.maximum(jblk, 0) * bk, bk)
        sc = _dot_nt(qs, kw_a[pl.ds(off, bk), :])
        missing = jnp.where(jblk >= 0, 0.0, NEG)
        if d == 2:
            sc = sc + stack(far_bias + missing)
        elif d == 1:
            sc = sc + missing
        else:
            sc = sc + diag_bias
        scs.append(sc)
        offs.append(off)

    ncp = kc_a.shape[0]
    s = _dot_nt(qs, kc_a[...])
    c_end = _iota((1, ncp), 1) * CMP_STRIDE + (CMP_BLOCK - 1)
    s = jnp.where(c_end <= t_row, s, NEG)
    e = jnp.exp(s - jnp.max(s, axis=1, keepdims=True))
    p = e / jnp.sum(e, axis=1, keepdims=True)
    p = jnp.where(t_row >= CMP_BLOCK - 1, p, 0.0)
    o_cmp = unstack(_dot(p.astype(BF16), vc_rep[...]))
    p_sum = p[0:tq]
    for r in range(1, R):
        p_sum = p_sum + p[r * tq:(r + 1) * tq]

    hi, lo = _split_bf16(p_sum)
    ov = ov_ref[...]
    imp = (_dot_nt(ov, hi) + _dot_nt(ov, lo))[0:nsb]
    jb = _iota((nsb, 1), 0)
    cur = (i * tq + _iota((1, tq), 1)) >> 6
    forced = (jb == 0) | ((cur - jb >= 0) & (cur - jb < N_LOCAL_SEL))
    score = jnp.where(forced, FORCE, jnp.where(jb <= cur, imp, -FORCE))
    cnt = jnp.zeros((nsb, tq), F32)
    for i2 in range(nsb):
        row = score[i2:i2 + 1, :]
        beats = (row > score) | ((row == score) & (jb > i2))
        cnt = cnt + jnp.where(beats, 1.0, 0.0)
    sel_t = jnp.where((cnt < float(min(N_SEL, nsb))) & (jb <= cur), 0.0, -MASK_BIG)
    pad_t = jnp.concatenate([jnp.zeros((HEAD_DIM, tq), F32), sel_t,
                             jnp.zeros((LANES - HEAD_DIM - nsb, tq), F32)], axis=0)
    aug = jnp.concatenate([pad_t.T, jnp.zeros((tq, w - LANES), F32)], axis=1)
    qsel = (qs_f32 + stack(aug)).astype(BF16)

    m = jnp.max(jnp.maximum(jnp.maximum(scs[0], scs[1]), scs[2]), axis=1, keepdims=True)
    mb = jnp.broadcast_to(m, (rows, bk))
    pes = [jnp.exp(sc - mb) for sc in scs]
    l_win = jnp.sum(pes[0] + pes[1] + pes[2], axis=1, keepdims=True)
    p_cat = jnp.concatenate([pe.astype(BF16) for pe in pes], axis=1)
    v_cat = jnp.concatenate([vw_rep[pl.ds(off, bk), :] for off in offs], axis=0)
    half_rows = rows // 2
    acc = jnp.concatenate([_dot(p_cat[0:half_rows], v_cat), _dot(p_cat[half_rows:rows], v_cat)], axis=0)
    o_win = unstack(acc / l_win)

    n_blk = i + 1
    n_step = (n_blk + 1) // 2
    zero_bias = jnp.zeros((rows, bk), F32)
    neg_bias = jnp.full((rows, bk), NEG, F32)

    def blk_off(x):
        return pl.multiple_of(jnp.minimum(x, i) * bk, bk)

    mx_scr[...] = neg_bias

    def pass1(t, carry):
        mx = mx_scr[...]
        for x in (2 * t, 2 * t + 1):
            sc = _dot_nt(qsel, ks_a[pl.ds(blk_off(x), bk), :])
            sc = sc + jnp.where(x == i, diag_bias, jnp.where(x > i, neg_bias, zero_bias))
            s_scr[x] = sc
            mx = jnp.maximum(mx, sc)
        mx_scr[...] = mx
        return carry

    lax.fori_loop(0, n_step, pass1, 0)
    m = jnp.max(mx_scr[...], axis=1, keepdims=True)
    mx_scr[...] = jnp.broadcast_to(m, (rows, bk))
    l_scr[...] = jnp.zeros_like(l_scr)
    acc_scr[...] = jnp.zeros_like(acc_scr)

    def pass2(t, carry):
        mb2 = mx_scr[...]
        pe0 = jnp.exp(s_scr[2 * t] - mb2)
        pe1 = jnp.exp(s_scr[2 * t + 1] - mb2)
        l_scr[...] += pe0 + pe1
        p_cat = jnp.concatenate([pe0.astype(BF16), pe1.astype(BF16)], axis=1)
        v_cat = jnp.concatenate([vs_rep[pl.ds(blk_off(2 * t), bk), :],
                                 vs_rep[pl.ds(blk_off(2 * t + 1), bk), :]], axis=0)
        acc_scr[...] += _dot(p_cat, v_cat)
        return carry

    lax.fori_loop(0, n_step, pass2, 0)
    o_sel = unstack(acc_scr[...] / jnp.sum(l_scr[...], axis=1, keepdims=True))

    ghi, glo = _split_bf16(_sigmoid(gl_ref[...]))
    col = _iota((LANES, w), 0)
    out = jnp.zeros((tq, w), F32)
    for br, o_br in enumerate((o_cmp, o_sel, o_win)):
        target = br * (NSA_KV_HEADS * R) + g * R + (_iota((LANES, w), 1) >> 6)
        spread = jnp.where(col == target, 1.0, 0.0).astype(BF16)
        out = out + (_dot(ghi, spread) + _dot(glo, spread)) * o_br
    o_ref[...] = out.astype(o_ref.dtype)


def _nsa_attention(proj, kc, vc, ksw, q_gain, coeffs, B, S):
    T = B * S
    w = GROUP_LANES
    G = NSA_KV_HEADS
    tq, bk = ATT_TQ, ATT_BK
    nq = S // tq
    nc = S // CMP_STRIDE
    nsb = S // SEL_BLOCK
    n_cmp = (S - CMP_BLOCK) // CMP_STRIDE + 1
    cs = np.arange(nc) * CMP_STRIDE
    js = np.arange(LANES) * SEL_BLOCK
    ov = ((cs[None, :] < js[:, None] + SEL_BLOCK) & (cs[None, :] + CMP_BLOCK > js[:, None])
          & (np.arange(nc)[None, :] < n_cmp) & (np.arange(LANES)[:, None] < nsb))
    ov = jnp.asarray(ov.astype(np.float32), dtype=BF16)
    gate_blk = (G * NSA_GROUP * HEAD_DIM) // LANES
    assert tq == bk and WINDOW == 2 * bk and HEAD_DIM + nsb <= LANES
    kern = functools.partial(_nsa_kernel, tq=tq, bk=bk, nsb=nsb, scale=HEAD_DIM ** -0.5)
    seq = lambda col: pl.BlockSpec((S, w), lambda b, g, i: (b, col))
    const = lambda shape: pl.BlockSpec(shape, lambda b, g, i: (0, 0))
    tab = pl.BlockSpec((tq, w), lambda b, g, i: (i, 0))
    rows = NSA_GROUP * tq
    return pl.pallas_call(
        kern,
        grid=(B, G, nq),
        in_specs=[
            pl.BlockSpec((tq, w), lambda b, g, i: (b * nq + i, g)),
            pl.BlockSpec((tq, LANES), lambda b, g, i: (b * nq + i, gate_blk)),
            pl.BlockSpec((nc, w), lambda b, g, i: (b, 0)),
            pl.BlockSpec((nc, w), lambda b, g, i: (b, 0)),
            seq(0), seq(1), seq(2), seq(3),
            const((w, w)), const((1, w)), tab, tab, tab, const((LANES, nc)),
        ],
        out_specs=pl.BlockSpec((tq, w), lambda b, g, i: (b * nq + i, g)),
        out_shape=jax.ShapeDtypeStruct((T, G * w), BF16),
        scratch_shapes=[
            pltpu.VMEM((nc, w), BF16), pltpu.VMEM((nc, w), BF16),
            pltpu.VMEM((S, w), BF16), pltpu.VMEM((S, w), BF16),
            pltpu.VMEM((S, w), BF16), pltpu.VMEM((S, w), BF16),
            pltpu.VMEM((S // bk, rows, bk), F32),
            pltpu.VMEM((rows, bk), F32), pltpu.VMEM((rows, bk), F32), pltpu.VMEM((rows, w), F32),
        ],
        compiler_params=_params("parallel", "arbitrary", "arbitrary"),
        name="nsa_attn",
    )(proj, proj, kc, vc, ksw, ksw, ksw, ksw,
      _block_mean_matrix(w), jnp.tile(q_gain, NSA_GROUP).reshape(1, w), *coeffs, ov)


def kernel(x, ffn1_norm, ffn1_w_in, ffn1_w_out, mix_norm, ffn2_norm, ffn2_w_in, ffn2_w_out,
           sb_w_qkv, sb_w_out, kv_norm, nsa_w_kv, nsa_k_norm, cmp_pos_k, cmp_pos_v,
           cmp_k_w1, cmp_k_w2, cmp_v_w1, cmp_v_w2, nsa_w_in, nsa_q_norm, nsa_w_out):
    B, S, D = x.shape
    depth = ffn1_norm.shape[0]
    n_a = sb_w_qkv.shape[0]
    n_sb_heads = sb_w_out.shape[1] // HEAD_DIM
    assert S % 512 == 0 and S // CMP_STRIDE == LANES, "sequence tiling assumes 16*128 tokens"
    h = x.reshape(B * S, D)
    bf = lambda a: a.astype(BF16)
    coeffs = _rope_coeffs(jnp.arange(S), NSA_KV_HEADS)
    n_q = nsa_w_in.shape[-1]
    n_q_pad = -(-n_q // LANES) * LANES
    kc = vc = ksw = None
    for layer in range(depth):
        h = _ffn(h, ffn1_norm[layer], bf(ffn1_w_in[layer]), bf(ffn1_w_out[layer]))
        if layer < n_a:
            qkv = _norm_proj(h, mix_norm[layer], bf(sb_w_qkv[layer]), 1024, BF16)
            o = _sb_attention(qkv, B, S, n_sb_heads)
            h = _out_proj(h, o, bf(sb_w_out[layer]))
        else:
            li = layer - n_a
            w_in = jnp.pad(bf(nsa_w_in[li]), ((0, 0), (0, n_q_pad - n_q)))
            proj = _norm_proj(h, mix_norm[layer], w_in, n_q_pad, F32)
            o = _nsa_attention(proj, kc, vc, ksw, nsa_q_norm[li], coeffs, B, S)
            h = _out_proj(h, o, bf(nsa_w_out[li]))
        h = _ffn(h, ffn2_norm[layer], bf(ffn2_w_in[layer]), bf(ffn2_w_out[layer]))
        if layer == n_a - 1:
            kv = _norm_proj(h, kv_norm, bf(nsa_w_kv), 768, F32)
            kc, vc = _compress(kv, B, S, cmp_pos_k, cmp_pos_v, cmp_k_w1, cmp_k_w2,
                               cmp_v_w1, cmp_v_w2, nsa_k_norm[0])
            ksw = _kv_prep(kv, nsa_k_norm, coeffs, S)
    return h.reshape(B, S, D)
```

```python
import functools

import jax
import jax.numpy as jnp
import numpy as np
from jax import lax
from jax.experimental import pallas as pl
from jax.experimental.pallas import tpu as pltpu

F32 = jnp.float32
BF16 = jnp.bfloat16

HEAD_DIM = 64
NSA_KV_HEADS = 4
NSA_GROUP = 4
N_BRANCHES = 3
ROPE_DIMS = 16
ROPE_THETA = 500000.0
CMP_BLOCK = 32
CMP_STRIDE = 16
SEL_BLOCK = 64
N_SEL = 8
N_LOCAL_SEL = 2
WINDOW = 512
EPS = 1e-6
NEG = -1e30
FORCE = 1e4

LANES = 128
GROUP_LANES = NSA_GROUP * HEAD_DIM
VMEM_LIMIT_BYTES = 56 * 1024 * 1024

FFN_TM = 512
PROJ_TM = 1024
ATT_TQ = 256
ATT_BK = 256


def _dot(a, b):
    return jnp.dot(a, b, preferred_element_type=F32)


def _dot_nt(a, b):
    return lax.dot_general(a, b, (((1,), (1,)), ((), ())), preferred_element_type=F32)


def _split_bf16(x):
    hi = x.astype(BF16)
    lo = (x - hi.astype(F32)).astype(BF16)
    return hi, lo


def _dot_split(x, w):
    hi, lo = _split_bf16(x)
    n = x.shape[0]
    r = _dot(jnp.concatenate([hi, lo], axis=0), w)
    return r[0:n] + r[n:2 * n]


def _sigmoid(x):
    return 1.0 / (1.0 + jnp.exp(-x))


def _iota(shape, dim):
    return lax.broadcasted_iota(jnp.int32, shape, dim)


def _params(*sem):
    return pltpu.CompilerParams(dimension_semantics=sem, vmem_limit_bytes=VMEM_LIMIT_BYTES)


def _ffn_kernel(h_ref, g_ref, wi_ref, wo_ref, o_ref, *, tf, nf):
    x = h_ref[...]
    ms = jnp.mean(x * x, axis=-1, keepdims=True)
    xn = (x * lax.rsqrt(ms + EPS) * g_ref[...]).astype(BF16)
    ff = tf * nf
    ab = [(_dot(xn, wi_ref[:, c * tf:(c + 1) * tf]), _dot(xn, wi_ref[:, ff + c * tf:ff + (c + 1) * tf]))
          for c in range(nf)]
    acc = None
    for c, (a, b) in enumerate(ab):
        act = (a * _sigmoid(a) * b).astype(BF16)
        part = _dot(act, wo_ref[c * tf:(c + 1) * tf, :])
        acc = part if acc is None else acc + part
    o_ref[...] = x + 0.5 * acc


def _ffn(h, gain, w_in, w_out, layer):
    T, D = h.shape
    F = w_out.shape[1]
    tm = min(FFN_TM, T)
    tf = F // 2 if (F // 2) % LANES == 0 else F
    nf = F // tf
    resident = dict(pipeline_mode=pl.Buffered(1))
    return pl.pallas_call(
        functools.partial(_ffn_kernel, tf=tf, nf=nf),
        grid=(T // tm,),
        in_specs=[
            pl.BlockSpec((tm, D), lambda i: (i, 0)),
            pl.BlockSpec((1, D), lambda i: (0, 0)),
            pl.BlockSpec((None, D, 2 * F), lambda i: (layer, 0, 0), **resident),
            pl.BlockSpec((None, F, D), lambda i: (layer, 0, 0), **resident),
        ],
        out_specs=pl.BlockSpec((tm, D), lambda i: (i, 0)),
        out_shape=jax.ShapeDtypeStruct((T, D), F32),
        compiler_params=_params("parallel"),
        name="ffn",
    )(h, gain.reshape(1, D), w_in, w_out)


def _norm_proj_kernel(h_ref, g_ref, w_ref, o_ref, xn_ref):
    @pl.when(pl.program_id(1) == 0)
    def _():
        x = h_ref[...]
        ms = jnp.mean(x * x, axis=-1, keepdims=True)
        xn_ref[...] = (x * lax.rsqrt(ms + EPS) * g_ref[...]).astype(BF16)

    o_ref[...] = _dot(xn_ref[...], w_ref[...]).astype(o_ref.dtype)


def _norm_proj(h, gain, w, layer, tn, out_dtype):
    T, D = h.shape
    N = w.shape[2]
    tm = min(PROJ_TM, T)
    return pl.pallas_call(
        _norm_proj_kernel,
        grid=(T // tm, N // tn),
        in_specs=[
            pl.BlockSpec((tm, D), lambda i, j: (i, 0)),
            pl.BlockSpec((1, D), lambda i, j: (0, 0)),
            pl.BlockSpec((None, D, tn), lambda i, j: (layer, 0, j)),
        ],
        out_specs=pl.BlockSpec((tm, tn), lambda i, j: (i, j)),
        out_shape=jax.ShapeDtypeStruct((T, N), out_dtype),
        scratch_shapes=[pltpu.VMEM((tm, D), BF16)],
        compiler_params=_params("parallel", "arbitrary"),
        name="norm_proj",
    )(h, gain.reshape(1, D), w)


def _out_proj_kernel(h_ref, x_ref, w_ref, o_ref):
    o_ref[...] = h_ref[...] + _dot(x_ref[...], w_ref[...])


def _out_proj(h, x, w, layer):
    T, D = h.shape
    K = x.shape[1]
    tm = min(PROJ_TM, T)
    return pl.pallas_call(
        _out_proj_kernel,
        grid=(T // tm,),
        in_specs=[
            pl.BlockSpec((tm, D), lambda i: (i, 0)),
            pl.BlockSpec((tm, K), lambda i: (i, 0)),
            pl.BlockSpec((None, K, D), lambda i: (layer, 0, 0)),
        ],
        out_specs=pl.BlockSpec((tm, D), lambda i: (i, 0)),
        out_shape=jax.ShapeDtypeStruct((T, D), F32),
        compiler_params=_params("parallel"),
        name="out_proj",
    )(h, x, w)


def _sb_kernel(q_ref, k_ref, v_ref, o_ref, acc_ref, c_ref, *, tq, bk, scale):
    i = pl.program_id(2)
    lane = _iota((1, LANES), 1)
    is_a = lane < HEAD_DIM
    q = q_ref[...] * scale
    zero = jnp.zeros_like(q)
    qh = (jnp.where(is_a, q, zero), jnp.where(is_a, zero, q))
    tri = jnp.where(_iota((bk, bk), 0) > _iota((bk, bk), 1), 1.0, 0.0).astype(BF16)
    acc_ref[...] = jnp.zeros_like(acc_ref)
    c_ref[...] = jnp.zeros_like(c_ref)

    def sweep(blocks, diagonal=False):
        chains = [(b, hd) for b in range(len(blocks)) for hd in range(2)]
        offs = [pl.multiple_of(j * bk, bk) for j in blocks]
        ks = [k_ref[pl.ds(off, bk), :] for off in offs]
        vs = [v_ref[pl.ds(off, bk), :] for off in offs]
        past = _iota((tq, bk), 1) < _iota((tq, bk), 0)
        z = {c: _dot_nt(qh[c[1]], ks[c[0]]) for c in chains}
        log_b, log_1m = {}, {}
        for c in chains:
            log_b[c] = jnp.minimum(z[c], 0.0) - jnp.log(1.0 + jnp.exp(-jnp.abs(z[c])))
            m = log_b[c] - z[c]
            log_1m[c] = jnp.where(past, m, 0.0) if (diagonal and c[0] == 0) else m
        within = {c: _dot(log_1m[c].astype(BF16), tri) for c in chains}
        carry = [c_ref[0:tq, :], c_ref[tq:2 * tq, :]]
        w = {}
        for c in chains:
            w[c] = jnp.exp(log_b[c] + (carry[c[1]] + within[c]))
            if diagonal and c[0] == 0:
                w[c] = jnp.where(past, w[c], 0.0)
            carry[c[1]] = carry[c[1]] + jnp.sum(log_1m[c], axis=1, keepdims=True)
        for hd in range(2):
            rws = slice(hd * tq, (hd + 1) * tq)
            upd = acc_ref[rws, :]
            for b in range(len(blocks)):
                upd = upd + _dot(w[(b, hd)].astype(BF16), vs[b])
            acc_ref[rws, :] = upd
            c_ref[rws, :] = carry[hd]

    odd_i = i % 2

    @pl.when(odd_i == 1)
    def _():
        sweep([i, i - 1], diagonal=True)

    @pl.when(odd_i == 0)
    def _():
        sweep([i], diagonal=True)

    def body(jj, carry):
        j = i - 1 - odd_i - 2 * jj
        sweep([j, j - 1])
        return carry

    lax.fori_loop(0, i // 2, body, 0)

    acc = acc_ref[...]
    o_ref[...] = jnp.where(is_a, acc[0:tq], acc[tq:2 * tq]).astype(o_ref.dtype)


def _sb_attention(qkv, B, S, n_heads):
    T = B * S
    tq, bk = ATT_TQ, ATT_BK
    nq = S // tq
    n_pairs = n_heads * HEAD_DIM // LANES
    kern = functools.partial(_sb_kernel, tq=tq, bk=bk, scale=HEAD_DIM ** -0.5)
    return pl.pallas_call(
        kern,
        grid=(B, n_pairs, nq),
        in_specs=[
            pl.BlockSpec((tq, LANES), lambda b, p, i: (b * nq + i, p)),
            pl.BlockSpec((S, LANES), lambda b, p, i: (b, n_pairs + p)),
            pl.BlockSpec((S, LANES), lambda b, p, i: (b, 2 * n_pairs + p)),
        ],
        out_specs=pl.BlockSpec((tq, LANES), lambda b, p, i: (b * nq + i, p)),
        out_shape=jax.ShapeDtypeStruct((T, n_heads * HEAD_DIM), BF16),
        scratch_shapes=[pltpu.VMEM((2 * tq, LANES), F32), pltpu.VMEM((2 * tq, 1), F32)],
        compiler_params=_params("parallel", "parallel", "arbitrary"),
        name="sb_attn",
    )(qkv, qkv, qkv)


def _head_rmsnorm(x, bd, gain):
    ms = _dot_split(x * x, bd)
    return x * lax.rsqrt(ms + EPS) * gain


def _rope(y, c, sa, sb):
    w = y.shape[-1]
    half = ROPE_DIMS // 2
    return y * c + pltpu.roll(y, w - half, 1) * sa + pltpu.roll(y, half, 1) * sb


def _rope_coeffs(positions, n_heads):
    inv_freq = jnp.power(ROPE_THETA, -jnp.arange(0, ROPE_DIMS, 2, dtype=F32) / ROPE_DIMS)
    ang = positions.astype(F32)[:, None] * inv_freq[None, :]
    cos, sin = jnp.cos(ang), jnp.sin(ang)
    p = positions.shape[0]
    rest = HEAD_DIM - ROPE_DIMS
    zeros_h = jnp.zeros((p, ROPE_DIMS // 2), F32)
    c = jnp.concatenate([cos, cos, jnp.ones((p, rest), F32)], axis=-1)
    sa = jnp.concatenate([-sin, zeros_h, jnp.zeros((p, rest), F32)], axis=-1)
    sb = jnp.concatenate([zeros_h, sin, jnp.zeros((p, rest), F32)], axis=-1)
    return tuple(jnp.tile(t, (1, n_heads)) for t in (c, sa, sb))


def _block_mean_matrix(width):
    idx = np.arange(width) // HEAD_DIM
    return jnp.asarray((idx[:, None] == idx[None, :]).astype(np.float32) / HEAD_DIM, dtype=BF16)


def _kv_prep_kernel(sel_ref, win_ref, bd_ref, gs_ref, gw_ref, c_ref, sa_ref, sb_ref, o_ref):
    w = GROUP_LANES
    bd = bd_ref[...]
    c, sa, sb = c_ref[...], sa_ref[...], sb_ref[...]
    ks = _rope(_head_rmsnorm(sel_ref[:, :w], bd, gs_ref[...]), c, sa, sb)
    kw = _rope(_head_rmsnorm(win_ref[:, :w], bd, gw_ref[...]), c, sa, sb)
    o_ref[:, 0 * w:1 * w] = ks.astype(BF16)
    o_ref[:, 1 * w:2 * w] = sel_ref[:, w:].astype(BF16)
    o_ref[:, 2 * w:3 * w] = kw.astype(BF16)
    o_ref[:, 3 * w:4 * w] = win_ref[:, w:].astype(BF16)


def _kv_prep(kv, k_norm, coeffs, S):
    T = kv.shape[0]
    w = GROUP_LANES
    tm = min(PROJ_TM, S)
    ns = S // tm
    gs = jnp.tile(k_norm[1], NSA_KV_HEADS).reshape(1, w)
    gw = jnp.tile(k_norm[2], NSA_KV_HEADS).reshape(1, w)
    tab = pl.BlockSpec((tm, w), lambda i: (i % ns, 0))
    one = pl.BlockSpec((1, w), lambda i: (0, 0))
    return pl.pallas_call(
        _kv_prep_kernel,
        grid=(T // tm,),
        in_specs=[
            pl.BlockSpec((tm, 2 * w), lambda i: (i, 1)),
            pl.BlockSpec((tm, 2 * w), lambda i: (i, 2)),
            pl.BlockSpec((w, w), lambda i: (0, 0)),
            one, one, tab, tab, tab,
        ],
        out_specs=pl.BlockSpec((tm, 4 * w), lambda i: (i, 0)),
        out_shape=jax.ShapeDtypeStruct((T, 4 * w), BF16),
        compiler_params=_params("parallel"),
        name="kv_prep",
    )(kv, kv, _block_mean_matrix(w), gs, gw, *coeffs)


def _compress_kernel(x_ref, pos_ref, w1_ref, w2_ref, bd_ref, g_ref, c_ref, sa_ref, sb_ref,
                     o_ref, top_ref, bot_ref):
    t = pl.program_id(0)
    l = pl.program_id(1)
    half = CMP_BLOCK // 2

    @pl.when(l == 0)
    def _():
        top_ref[...] = jnp.zeros_like(top_ref)
        bot_ref[...] = jnp.zeros_like(bot_ref)

    x = x_ref[0]
    top_ref[...] += _dot((x + pos_ref[0, 0]).astype(BF16), w1_ref[0, 0])
    bot_ref[...] += _dot((x + pos_ref[0, 1]).astype(BF16), w1_ref[0, 1])

    @pl.when(l == half - 1)
    def _():
        rows = bot_ref.shape[0]
        hid = top_ref[...] + pltpu.roll(bot_ref[...], rows - 1, 0)
        act = (hid * _sigmoid(hid)).astype(BF16)
        y = _dot(act, w2_ref[0])
        yk = _rope(_head_rmsnorm(y, bd_ref[...], g_ref[...]), c_ref[...], sa_ref[...], sb_ref[...])
        o_ref[0] = jnp.where(t == 0, yk, y).astype(BF16)


def _compress(kv, B, S, pos_k, pos_v, k_w1, k_w2, v_w1, v_w2, k_gain):
    w = GROUP_LANES
    G = NSA_KV_HEADS
    half = CMP_BLOCK // 2
    nc = S // CMP_STRIDE
    rows = B * nc
    hid = k_w1.shape[1]
    raw = kv[:, :2 * w].reshape(B, nc, half, 2, w)
    xc = jnp.transpose(raw, (3, 2, 0, 1, 4)).reshape(2, half, rows, w)
    eye = jnp.eye(G, dtype=F32)

    def blockdiag(wm):
        out = jnp.einsum('gh,...ab->...gahb', eye, wm)
        return out.reshape(*wm.shape[:-2], G * wm.shape[-2], G * wm.shape[-1])

    def prep_w1(w1):
        w1 = w1.reshape(2, half, HEAD_DIM, hid)
        return jnp.transpose(blockdiag(w1), (1, 0, 2, 3))

    w1 = jnp.stack([prep_w1(k_w1), prep_w1(v_w1)]).astype(BF16)
    w2 = jnp.stack([blockdiag(k_w2), blockdiag(v_w2)]).astype(BF16)

    def prep_pos(p):
        p = jnp.tile(p, (1, G)).reshape(2, half, 1, w)
        return jnp.transpose(p, (1, 0, 2, 3))

    pos = jnp.stack([prep_pos(pos_k), prep_pos(pos_v)])
    c_end = jnp.arange(nc) * CMP_STRIDE + (CMP_BLOCK - 1)
    coeffs = tuple(jnp.tile(tb, (B, 1)) for tb in _rope_coeffs(c_end, G))
    gain = jnp.tile(k_gain, G).reshape(1, w)
    const2 = lambda t, l: (0, 0)
    out = pl.pallas_call(
        _compress_kernel,
        grid=(2, half),
        in_specs=[
            pl.BlockSpec((None, 1, rows, w), lambda t, l: (t, l, 0, 0)),
            pl.BlockSpec((None, 1, 2, 1, w), lambda t, l: (t, l, 0, 0, 0)),
            pl.BlockSpec((None, 1, 2, w, G * hid), lambda t, l: (t, l, 0, 0, 0)),
            pl.BlockSpec((1, G * hid, w), lambda t, l: (t, 0, 0)),
            pl.BlockSpec((w, w), const2),
            pl.BlockSpec((1, w), const2),
            pl.BlockSpec((rows, w), const2),
            pl.BlockSpec((rows, w), const2),
            pl.BlockSpec((rows, w), const2),
        ],
        out_specs=pl.BlockSpec((1, rows, w), lambda t, l: (t, 0, 0)),
        out_shape=jax.ShapeDtypeStruct((2, rows, w), BF16),
        scratch_shapes=[pltpu.VMEM((rows, G * hid), F32), pltpu.VMEM((rows, G * hid), F32)],
        compiler_params=_params("arbitrary", "arbitrary"),
        name="nsa_compress",
    )(xc, pos, w1, w2, _block_mean_matrix(w), gain, *coeffs)
    return out[0], out[1]


def _nsa_kernel_old(q_ref, gl_ref, kc_ref, vc_ref, ks_ref, vs_ref, kw_ref, vw_ref,
                bd_ref, qg_ref, c_ref, sa_ref, sb_ref, ov_ref,
                o_ref,
                kc_rep, vc_rep, ks_rep, vs_rep, kw_rep, vw_rep, m_ref, l_ref, acc_ref,
                *, tq, bk, nsb, scale):
    g = pl.program_id(1)
    i = pl.program_id(2)
    w = GROUP_LANES
    R = NSA_GROUP
    rows = R * tq
    lane_head = _iota((1, w), 1) >> 6

    @pl.when(i == 0)
    def _():
        rr, cc = _iota((w, w), 0), _iota((w, w), 1)
        pick = jnp.where(((rr >> 6) == g) & ((rr & 63) == (cc & 63)), 1.0, 0.0).astype(BF16)
        for src, dst in ((kc_ref, kc_rep), (vc_ref, vc_rep), (ks_ref, ks_rep),
                         (vs_ref, vs_rep), (kw_ref, kw_rep), (vw_ref, vw_rep)):
            dst[...] = _dot(src[...], pick).astype(BF16)

    qn = _rope(_head_rmsnorm(q_ref[...], bd_ref[...], qg_ref[...]), c_ref[...], sa_ref[...], sb_ref[...])
    qn = qn.astype(BF16)
    qm = jnp.concatenate(
        [jnp.where(lane_head == r, qn, jnp.zeros_like(qn)) for r in range(R)], axis=0)
    t_row = i * tq + (_iota((rows, 1), 0) & (tq - 1))

    def unstack(x):
        out = jnp.where(lane_head == 0, x[0:tq], 0.0)
        for r in range(1, R):
            out = out + jnp.where(lane_head == r, x[r * tq:(r + 1) * tq], 0.0)
        return out

    ncp = kc_rep.shape[0]
    s = _dot_nt(qm, kc_rep[...]) * scale
    c_end = _iota((1, ncp), 1) * CMP_STRIDE + (CMP_BLOCK - 1)
    valid = c_end <= t_row
    s = jnp.where(valid, s, NEG)
    e = jnp.exp(s - jnp.max(s, axis=1, keepdims=True))
    p = e / jnp.sum(e, axis=1, keepdims=True)
    p = jnp.where(t_row >= CMP_BLOCK - 1, p, 0.0)
    o_cmp = unstack(_dot(p.astype(BF16), vc_rep[...]))
    p_sum = p[0:tq]
    for r in range(1, R):
        p_sum = p_sum + p[r * tq:(r + 1) * tq]

    hi, lo = _split_bf16(p_sum)
    ov = ov_ref[...]
    imp = (_dot_nt(ov, hi) + _dot_nt(ov, lo))[0:nsb]
    jb = _iota((nsb, 1), 0)
    cur = (i * tq + _iota((1, tq), 1)) >> 6
    forced = (jb == 0) | ((cur - jb >= 0) & (cur - jb < N_LOCAL_SEL))
    score = jnp.where(forced, FORCE, jnp.where(jb <= cur, imp, -FORCE))
    cnt = jnp.zeros((nsb, tq), F32)
    for i2 in range(nsb):
        row = score[i2:i2 + 1, :]
        beats = (row > score) | ((row == score) & (jb > i2))
        cnt = cnt + jnp.where(beats, 1.0, 0.0)
    sel_t = jnp.where(cnt < float(min(N_SEL, nsb)), 1.0, 0.0)
    if nsb < LANES:
        sel_t = jnp.concatenate([sel_t, jnp.zeros((LANES - nsb, tq), F32)], axis=0)
    sel = sel_t.T.astype(BF16)

    def flash_block(k_rep, v_rep, jblk, ok_fn):
        off = pl.multiple_of(jblk * bk, bk)
        kb = k_rep[pl.ds(off, bk), :]
        vb = v_rep[pl.ds(off, bk), :]
        s_pos = jblk * bk + _iota((1, bk), 1)
        ok = ok_fn(s_pos, jblk)
        sc = jnp.where(ok, _dot_nt(qm, kb) * scale, NEG)
        m_old = m_ref[...]
        m_new = jnp.maximum(m_old, jnp.max(sc, axis=1, keepdims=True))
        alpha = jnp.exp(m_old - m_new)
        pe = jnp.where(ok, jnp.exp(sc - m_new), 0.0)
        l_ref[...] = alpha * l_ref[...] + jnp.sum(pe, axis=1, keepdims=True)
        acc_ref[...] = alpha * acc_ref[...] + _dot(pe.astype(BF16), vb)
        m_ref[...] = m_new

    def reset():
        m_ref[...] = jnp.full_like(m_ref, NEG)
        l_ref[...] = jnp.zeros_like(l_ref)
        acc_ref[...] = jnp.zeros_like(acc_ref)

    def sel_ok(s_pos, jblk):
        blk = (jblk * bk + _iota((LANES, bk), 1)) >> 6
        expand = jnp.where(_iota((LANES, bk), 0) == blk, 1.0, 0.0).astype(BF16)
        md = _dot(sel, expand)
        md = jnp.concatenate([md] * R, axis=0)
        return (md > 0.5) & (s_pos <= t_row)

    reset()

    def sel_body(jblk, carry):
        flash_block(ks_rep, vs_rep, jblk, sel_ok)
        return carry

    lax.fori_loop(0, (i * tq + tq) // bk, sel_body, 0)
    o_sel = unstack(acc_ref[...] / l_ref[...])

    def win_ok(s_pos, jblk):
        return (s_pos <= t_row) & (t_row - s_pos < WINDOW)

    reset()

    def win_body(jblk, carry):
        flash_block(kw_rep, vw_rep, jblk, win_ok)
        return carry

    lax.fori_loop(jnp.maximum(i * tq - (WINDOW - 1), 0) // bk, (i * tq + tq) // bk, win_body, 0)
    o_win = unstack(acc_ref[...] / l_ref[...])

    ghi, glo = _split_bf16(_sigmoid(gl_ref[...]))
    col = _iota((LANES, w), 0)
    out = jnp.zeros((tq, w), F32)
    for br, o_br in enumerate((o_cmp, o_sel, o_win)):
        target = br * (NSA_KV_HEADS * R) + g * R + (_iota((LANES, w), 1) >> 6)
        spread = jnp.where(col == target, 1.0, 0.0).astype(BF16)
        out = out + (_dot(ghi, spread) + _dot(glo, spread)) * o_br
    o_ref[...] = out.astype(o_ref.dtype)


MASK_BIG = 2.0 ** 30


def _nsa_kernel(q_ref, gl_ref, kc_ref, vc_ref, ks_ref, vs_ref, kw_ref, vw_ref,
                bd_ref, qg_ref, c_ref, sa_ref, sb_ref, ov_ref,
                o_ref,
                kc_a, vc_rep, ks_a, vs_rep, kw_a, vw_rep, s_scr, mx_scr, l_scr, acc_scr,
                *, tq, bk, nsb, scale):
    g = pl.program_id(1)
    i = pl.program_id(2)
    w = GROUP_LANES
    R = NSA_GROUP
    rows = R * tq
    lane = _iota((1, w), 1)
    lane_head = lane >> 6

    @pl.when(i == 0)
    def _():
        rr, cc = _iota((w, w), 0), _iota((w, w), 1)
        mine = ((rr >> 6) == g) & ((rr & 63) == (cc & 63))
        to_slot0 = jnp.where(mine & (cc < HEAD_DIM), 1.0, 0.0).astype(BF16)
        to_all = jnp.where(mine, 1.0, 0.0).astype(BF16)
        kc_a[...] = _dot(kc_ref[...], to_slot0).astype(BF16)
        kw_a[...] = _dot(kw_ref[...], to_slot0).astype(BF16)
        seq = ks_a.shape[0]
        tag = jnp.where(_iota((seq, w), 1) - HEAD_DIM == (_iota((seq, w), 0) >> 6), 1.0, 0.0)
        ks_a[...] = (_dot(ks_ref[...], to_slot0) + tag).astype(BF16)
        for src, dst in ((vc_ref, vc_rep), (vs_ref, vs_rep), (vw_ref, vw_rep)):
            dst[...] = _dot(src[...], to_all).astype(BF16)

    qn = _rope(_head_rmsnorm(q_ref[...], bd_ref[...], qg_ref[...]), c_ref[...], sa_ref[...], sb_ref[...])
    qn = qn * scale
    slot0 = lane < HEAD_DIM
    parts = []
    for r in range(R):
        qr = qn if r == 0 else pltpu.roll(qn, w - r * HEAD_DIM, 1)
        parts.append(jnp.where(slot0, qr, 0.0))
    qs_f32 = jnp.concatenate(parts, axis=0)
    qs = qs_f32.astype(BF16)
    t_row = i * tq + (_iota((rows, 1), 0) & (tq - 1))

    def unstack(x):
        out = jnp.where(lane_head == 0, x[0:tq], 0.0)
        for r in range(1, R):
            out = out + jnp.where(lane_head == r, x[r * tq:(r + 1) * tq], 0.0)
        return out

    def stack(x):
        return jnp.concatenate([x] * R, axis=0)

    tri_ok = _iota((tq, bk), 1) <= _iota((tq, bk), 0)
    diag_bias = stack(jnp.where(tri_ok, 0.0, NEG))

    far_bias = jnp.where(tri_ok, NEG, 0.0)
    half_rows = rows // 2
    win_off, win_bias = [], []
    for d in (2, 1, 0):
        jblk = i - d
        win_off.append(pl.multiple_of(jnp.maximum(jblk, 0) * bk, bk))
        missing = jnp.where(jblk >= 0, 0.0, NEG)
        bias = (far_bias + missing) if d == 2 else (jnp.where(tri_ok, 0.0, NEG) if d == 0 else None)
        win_bias.append(missing if bias is None else jnp.concatenate([bias] * (R // 2), axis=0))
    win_sc = [[_dot_nt(qs[h * half_rows:(h + 1) * half_rows], kw_a[pl.ds(off, bk), :]) + bias
               for off, bias in zip(win_off, win_bias)] for h in range(2)]
    v_cat = jnp.concatenate([vw_rep[pl.ds(off, bk), :] for off in win_off], axis=0)

    def win_half(h):
        scs = win_sc[h]
        m_h = jnp.max(jnp.maximum(jnp.maximum(scs[0], scs[1]), scs[2]), axis=1, keepdims=True)
        mb = jnp.broadcast_to(m_h, (half_rows, bk))
        pes = [jnp.exp(sc - mb) for sc in scs]
        l_h = jnp.sum(pes[0] + pes[1] + pes[2], axis=1, keepdims=True)
        p_cat = jnp.concatenate([pe.astype(BF16) for pe in pes], axis=1)
        return _dot(p_cat, v_cat) / l_h

    ncp = kc_a.shape[0]
    s = _dot_nt(qs, kc_a[...])
    win_0 = win_half(0)
    c_end = _iota((1, ncp), 1) * CMP_STRIDE + (CMP_BLOCK - 1)
    s = jnp.where(c_end <= t_row, s, NEG)
    e = jnp.exp(s - jnp.max(s, axis=1, keepdims=True))
    p = e / jnp.sum(e, axis=1, keepdims=True)
    p = jnp.where(t_row >= CMP_BLOCK - 1, p, 0.0)
    o_cmp = unstack(_dot(p.astype(BF16), vc_rep[...]))
    p_sum = p[0:tq]
    for r in range(1, R):
        p_sum = p_sum + p[r * tq:(r + 1) * tq]

    hi, lo = _split_bf16(p_sum)
    ov = ov_ref[...]
    imp = (_dot_nt(ov, hi) + _dot_nt(ov, lo))[0:nsb]
    win_1 = win_half(1)
    o_win = unstack(jnp.concatenate([win_0, win_1], axis=0))
    jb = _iota((nsb, 1), 0)
    cur = (i * tq + _iota((1, tq), 1)) >> 6
    forced = (jb == 0) | ((cur - jb >= 0) & (cur - jb < N_LOCAL_SEL))
    score = jnp.where(forced, FORCE, jnp.where(jb <= cur, imp, -FORCE))
    cnt = jnp.zeros((nsb, tq), F32)
    for i2 in range(nsb):
        row = score[i2:i2 + 1, :]
        beats = (row > score) | ((row == score) & (jb > i2))
        cnt = cnt + jnp.where(beats, 1.0, 0.0)
    sel_t = jnp.where((cnt < float(min(N_SEL, nsb))) & (jb <= cur), 0.0, -MASK_BIG)
    pad_t = jnp.concatenate([jnp.zeros((HEAD_DIM, tq), F32), sel_t,
                             jnp.zeros((LANES - HEAD_DIM - nsb, tq), F32)], axis=0)
    aug = jnp.concatenate([pad_t.T, jnp.zeros((tq, w - LANES), F32)], axis=1)
    qsel = (qs_f32 + stack(aug)).astype(BF16)

    n_blk = i + 1
    n_step = (n_blk + 1) // 2
    zero_bias = jnp.zeros((rows, bk), F32)
    neg_bias = jnp.full((rows, bk), NEG, F32)

    def blk_off(x):
        return pl.multiple_of(jnp.minimum(x, i) * bk, bk)

    mx_scr[...] = neg_bias

    def pass1(t, carry):
        mx = mx_scr[...]
        for x in (2 * t, 2 * t + 1):
            sc = _dot_nt(qsel, ks_a[pl.ds(blk_off(x), bk), :])
            sc = sc + jnp.where(x == i, diag_bias, jnp.where(x > i, neg_bias, zero_bias))
            s_scr[x] = sc
            mx = jnp.maximum(mx, sc)
        mx_scr[...] = mx
        return carry

    lax.fori_loop(0, n_step, pass1, 0)
    m = jnp.max(mx_scr[...], axis=1, keepdims=True)
    mx_scr[...] = jnp.broadcast_to(m, (rows, bk))
    l_scr[...] = jnp.zeros_like(l_scr)
    acc_scr[...] = jnp.zeros_like(acc_scr)

    def pass2(t, carry):
        mb2 = mx_scr[...]
        pe0 = jnp.exp(s_scr[2 * t] - mb2)
        pe1 = jnp.exp(s_scr[2 * t + 1] - mb2)
        l_scr[...] += pe0 + pe1
        p_cat = jnp.concatenate([pe0.astype(BF16), pe1.astype(BF16)], axis=1)
        v_cat = jnp.concatenate([vs_rep[pl.ds(blk_off(2 * t), bk), :],
                                 vs_rep[pl.ds(blk_off(2 * t + 1), bk), :]], axis=0)
        acc_scr[...] += _dot(p_cat, v_cat)
        return carry

    lax.fori_loop(0, n_step, pass2, 0)
    o_sel = unstack(acc_scr[...] / jnp.sum(l_scr[...], axis=1, keepdims=True))

    ghi, glo = _split_bf16(_sigmoid(gl_ref[...]))
    col = _iota((LANES, w), 0)
    out = jnp.zeros((tq, w), F32)
    for br, o_br in enumerate((o_cmp, o_sel, o_win)):
        target = br * (NSA_KV_HEADS * R) + g * R + (_iota((LANES, w), 1) >> 6)
        spread = jnp.where(col == target, 1.0, 0.0).astype(BF16)
        out = out + (_dot(ghi, spread) + _dot(glo, spread)) * o_br
    o_ref[...] = out.astype(o_ref.dtype)


def _nsa_attention(proj, kc, vc, ksw, q_gain, coeffs, B, S):
    T = B * S
    w = GROUP_LANES
    G = NSA_KV_HEADS
    tq, bk = ATT_TQ, ATT_BK
    nq = S // tq
    nc = S // CMP_STRIDE
    nsb = S // SEL_BLOCK
    n_cmp = (S - CMP_BLOCK) // CMP_STRIDE + 1
    cs = np.arange(nc) * CMP_STRIDE
    js = np.arange(LANES) * SEL_BLOCK
    ov = ((cs[None, :] < js[:, None] + SEL_BLOCK) & (cs[None, :] + CMP_BLOCK > js[:, None])
          & (np.arange(nc)[None, :] < n_cmp) & (np.arange(LANES)[:, None] < nsb))
    ov = jnp.asarray(ov.astype(np.float32), dtype=BF16)
    gate_blk = (G * NSA_GROUP * HEAD_DIM) // LANES
    assert tq == bk and WINDOW == 2 * bk and HEAD_DIM + nsb <= LANES
    kern = functools.partial(_nsa_kernel, tq=tq, bk=bk, nsb=nsb, scale=HEAD_DIM ** -0.5)
    seq = lambda col: pl.BlockSpec((S, w), lambda b, g, i: (b, col))
    const = lambda shape: pl.BlockSpec(shape, lambda b, g, i: (0, 0))
    tab = pl.BlockSpec((tq, w), lambda b, g, i: (i, 0))
    rows = NSA_GROUP * tq
    return pl.pallas_call(
        kern,
        grid=(B, G, nq),
        in_specs=[
            pl.BlockSpec((tq, w), lambda b, g, i: (b * nq + i, g)),
            pl.BlockSpec((tq, LANES), lambda b, g, i: (b * nq + i, gate_blk)),
            pl.BlockSpec((nc, w), lambda b, g, i: (b, 0)),
            pl.BlockSpec((nc, w), lambda b, g, i: (b, 0)),
            seq(0), seq(1), seq(2), seq(3),
            const((w, w)), const((1, w)), tab, tab, tab, const((LANES, nc)),
        ],
        out_specs=pl.BlockSpec((tq, w), lambda b, g, i: (b * nq + i, g)),
        out_shape=jax.ShapeDtypeStruct((T, G * w), BF16),
        scratch_shapes=[
            pltpu.VMEM((nc, w), BF16), pltpu.VMEM((nc, w), BF16),
            pltpu.VMEM((S, w), BF16), pltpu.VMEM((S, w), BF16),
            pltpu.VMEM((S, w), BF16), pltpu.VMEM((S, w), BF16),
            pltpu.VMEM((S // bk, rows, bk), F32),
            pltpu.VMEM((rows, bk), F32), pltpu.VMEM((rows, bk), F32), pltpu.VMEM((rows, w), F32),
        ],
        compiler_params=_params("parallel", "arbitrary", "arbitrary"),
        name="nsa_attn",
    )(proj, proj, kc, vc, ksw, ksw, ksw, ksw,
      _block_mean_matrix(w), jnp.tile(q_gain, NSA_GROUP).reshape(1, w), *coeffs, ov)


def kernel(x, ffn1_norm, ffn1_w_in, ffn1_w_out, mix_norm, ffn2_norm, ffn2_w_in, ffn2_w_out,
           sb_w_qkv, sb_w_out, kv_norm, nsa_w_kv, nsa_k_norm, cmp_pos_k, cmp_pos_v,
           cmp_k_w1, cmp_k_w2, cmp_v_w1, cmp_v_w2, nsa_w_in, nsa_q_norm, nsa_w_out):
    B, S, D = x.shape
    depth = ffn1_norm.shape[0]
    n_a = sb_w_qkv.shape[0]
    n_sb_heads = sb_w_out.shape[1] // HEAD_DIM
    assert S % 512 == 0 and S // CMP_STRIDE == LANES, "sequence tiling assumes 16*128 tokens"
    h = x.reshape(B * S, D)
    bf = lambda a: a.astype(BF16)
    coeffs = _rope_coeffs(jnp.arange(S), NSA_KV_HEADS)
    n_q = nsa_w_in.shape[-1]
    n_q_pad = -(-n_q // LANES) * LANES
    kc = vc = ksw = None
    w1_in, w1_out, w2_in, w2_out = bf(ffn1_w_in), bf(ffn1_w_out), bf(ffn2_w_in), bf(ffn2_w_out)
    w_qkv, w_sb_out, w_nsa_out = bf(sb_w_qkv), bf(sb_w_out), bf(nsa_w_out)
    w_nsa_in = jnp.pad(bf(nsa_w_in), ((0, 0), (0, 0), (0, n_q_pad - n_q)))
    w_kv = bf(nsa_w_kv)[None]
    for layer in range(depth):
        h = _ffn(h, ffn1_norm[layer], w1_in, w1_out, layer)
        if layer < n_a:
            qkv = _norm_proj(h, mix_norm[layer], w_qkv, layer, 1024, BF16)
            o = _sb_attention(qkv, B, S, n_sb_heads)
            h = _out_proj(h, o, w_sb_out, layer)
        else:
            li = layer - n_a
            proj = _norm_proj(h, mix_norm[layer], w_nsa_in, li, n_q_pad, F32)
            o = _nsa_attention(proj, kc, vc, ksw, nsa_q_norm[li], coeffs, B, S)
            h = _out_proj(h, o, w_nsa_out, li)
        h = _ffn(h, ffn2_norm[layer], w2_in, w2_out, layer)
        if layer == n_a - 1:
            kv = _norm_proj(h, kv_norm, w_kv, 0, 768, F32)
            kc, vc = _compress(kv, B, S, cmp_pos_k, cmp_pos_v, cmp_k_w1, cmp_k_w2,
                               cmp_v_w1, cmp_v_w2, nsa_k_norm[0])
            ksw = _kv_prep(kv, nsa_k_norm, coeffs, S)
    return h.reshape(B, S, D)
```

```python
import functools

import jax
import jax.numpy as jnp
import numpy as np
from jax import lax
from jax.experimental import pallas as pl
from jax.experimental.pallas import tpu as pltpu

F32 = jnp.float32
BF16 = jnp.bfloat16

HEAD_DIM = 64
NSA_KV_HEADS = 4
NSA_GROUP = 4
N_BRANCHES = 3
ROPE_DIMS = 16
ROPE_THETA = 500000.0
CMP_BLOCK = 32
CMP_STRIDE = 16
SEL_BLOCK = 64
N_SEL = 8
N_LOCAL_SEL = 2
WINDOW = 512
EPS = 1e-6
NEG = -1e30
FORCE = 1e4

LANES = 128
GROUP_LANES = NSA_GROUP * HEAD_DIM
VMEM_LIMIT_BYTES = 56 * 1024 * 1024

FFN_TM = 512
PROJ_TM = 1024
ATT_TQ = 256
ATT_BK = 256


def _dot(a, b):
    return jnp.dot(a, b, preferred_element_type=F32)


def _dot_nt(a, b):
    return lax.dot_general(a, b, (((1,), (1,)), ((), ())), preferred_element_type=F32)


def _split_bf16(x):
    hi = x.astype(BF16)
    lo = (x - hi.astype(F32)).astype(BF16)
    return hi, lo


def _dot_split(x, w):
    hi, lo = _split_bf16(x)
    n = x.shape[0]
    r = _dot(jnp.concatenate([hi, lo], axis=0), w)
    return r[0:n] + r[n:2 * n]


def _sigmoid(x):
    return 1.0 / (1.0 + jnp.exp(-x))


def _iota(shape, dim):
    return lax.broadcasted_iota(jnp.int32, shape, dim)


def _params(*sem):
    return pltpu.CompilerParams(dimension_semantics=sem, vmem_limit_bytes=VMEM_LIMIT_BYTES)


def _ffn_kernel(h_ref, g_ref, wi_ref, wo_ref, o_ref, *, tf, nf):
    x = h_ref[...]
    ms = jnp.mean(x * x, axis=-1, keepdims=True)
    xn = (x * lax.rsqrt(ms + EPS) * g_ref[...]).astype(BF16)
    ff = tf * nf
    ab = [(_dot(xn, wi_ref[:, c * tf:(c + 1) * tf]), _dot(xn, wi_ref[:, ff + c * tf:ff + (c + 1) * tf]))
          for c in range(nf)]
    acc = None
    for c, (a, b) in enumerate(ab):
        act = (a * _sigmoid(a) * b).astype(BF16)
        part = _dot(act, wo_ref[c * tf:(c + 1) * tf, :])
        acc = part if acc is None else acc + part
    o_ref[...] = x + 0.5 * acc


def _ffn(h, gain, w_in, w_out, layer):
    T, D = h.shape
    F = w_out.shape[1]
    tm = min(FFN_TM, T)
    tf = F // 2 if (F // 2) % LANES == 0 else F
    nf = F // tf
    resident = dict(pipeline_mode=pl.Buffered(1))
    return pl.pallas_call(
        functools.partial(_ffn_kernel, tf=tf, nf=nf),
        grid=(T // tm,),
        in_specs=[
            pl.BlockSpec((tm, D), lambda i: (i, 0)),
            pl.BlockSpec((1, D), lambda i: (0, 0)),
            pl.BlockSpec((None, D, 2 * F), lambda i: (layer, 0, 0), **resident),
            pl.BlockSpec((None, F, D), lambda i: (layer, 0, 0), **resident),
        ],
        out_specs=pl.BlockSpec((tm, D), lambda i: (i, 0)),
        out_shape=jax.ShapeDtypeStruct((T, D), F32),
        compiler_params=_params("parallel"),
        name="ffn",
    )(h, gain.reshape(1, D), w_in, w_out)


def _norm_proj_kernel(h_ref, g_ref, w_ref, o_ref, xn_ref):
    @pl.when(pl.program_id(1) == 0)
    def _():
        x = h_ref[...]
        ms = jnp.mean(x * x, axis=-1, keepdims=True)
        xn_ref[...] = (x * lax.rsqrt(ms + EPS) * g_ref[...]).astype(BF16)

    o_ref[...] = _dot(xn_ref[...], w_ref[...]).astype(o_ref.dtype)


def _norm_proj(h, gain, w, layer, tn, out_dtype):
    T, D = h.shape
    N = w.shape[2]
    tm = min(PROJ_TM, T)
    return pl.pallas_call(
        _norm_proj_kernel,
        grid=(T // tm, N // tn),
        in_specs=[
            pl.BlockSpec((tm, D), lambda i, j: (i, 0)),
            pl.BlockSpec((1, D), lambda i, j: (0, 0)),
            pl.BlockSpec((None, D, tn), lambda i, j: (layer, 0, j)),
        ],
        out_specs=pl.BlockSpec((tm, tn), lambda i, j: (i, j)),
        out_shape=jax.ShapeDtypeStruct((T, N), out_dtype),
        scratch_shapes=[pltpu.VMEM((tm, D), BF16)],
        compiler_params=_params("parallel", "arbitrary"),
        name="norm_proj",
    )(h, gain.reshape(1, D), w)


def _out_proj_kernel(h_ref, x_ref, w_ref, o_ref):
    o_ref[...] = h_ref[...] + _dot(x_ref[...], w_ref[...])


def _out_proj(h, x, w, layer):
    T, D = h.shape
    K = x.shape[1]
    tm = min(PROJ_TM, T)
    return pl.pallas_call(
        _out_proj_kernel,
        grid=(T // tm,),
        in_specs=[
            pl.BlockSpec((tm, D), lambda i: (i, 0)),
            pl.BlockSpec((tm, K), lambda i: (i, 0)),
            pl.BlockSpec((None, K, D), lambda i: (layer, 0, 0)),
        ],
        out_specs=pl.BlockSpec((tm, D), lambda i: (i, 0)),
        out_shape=jax.ShapeDtypeStruct((T, D), F32),
        compiler_params=_params("parallel"),
        name="out_proj",
    )(h, x, w)


def _sb_kernel(q_ref, k_ref, v_ref, o_ref, acc_ref, c_ref, *, tq, bk, scale):
    i = pl.program_id(2)
    lane = _iota((1, LANES), 1)
    is_a = lane < HEAD_DIM
    q = q_ref[...] * scale
    zero = jnp.zeros_like(q)
    qh = (jnp.where(is_a, q, zero), jnp.where(is_a, zero, q))
    tri = jnp.where(_iota((bk, bk), 0) > _iota((bk, bk), 1), 1.0, 0.0).astype(BF16)
    acc_ref[...] = jnp.zeros_like(acc_ref)
    c_ref[...] = jnp.zeros_like(c_ref)

    def sweep(blocks, diagonal=False):
        chains = [(b, hd) for b in range(len(blocks)) for hd in range(2)]
        offs = [pl.multiple_of(j * bk, bk) for j in blocks]
        ks = [k_ref[pl.ds(off, bk), :] for off in offs]
        vs = [v_ref[pl.ds(off, bk), :] for off in offs]
        past = _iota((tq, bk), 1) < _iota((tq, bk), 0)
        z = {c: _dot_nt(qh[c[1]], ks[c[0]]) for c in chains}
        log_b, log_1m = {}, {}
        for c in chains:
            log_b[c] = jnp.minimum(z[c], 0.0) - jnp.log(1.0 + jnp.exp(-jnp.abs(z[c])))
            m = log_b[c] - z[c]
            log_1m[c] = jnp.where(past, m, 0.0) if (diagonal and c[0] == 0) else m
        within = {c: _dot(log_1m[c].astype(BF16), tri) for c in chains}
        carry = [c_ref[0:tq, :], c_ref[tq:2 * tq, :]]
        w = {}
        for c in chains:
            w[c] = jnp.exp(log_b[c] + (carry[c[1]] + within[c]))
            if diagonal and c[0] == 0:
                w[c] = jnp.where(past, w[c], 0.0)
            carry[c[1]] = carry[c[1]] + jnp.sum(log_1m[c], axis=1, keepdims=True)
        for hd in range(2):
            rws = slice(hd * tq, (hd + 1) * tq)
            upd = acc_ref[rws, :]
            for b in range(len(blocks)):
                upd = upd + _dot(w[(b, hd)].astype(BF16), vs[b])
            acc_ref[rws, :] = upd
            c_ref[rws, :] = carry[hd]

    odd_i = i % 2

    @pl.when(odd_i == 1)
    def _():
        sweep([i, i - 1], diagonal=True)

    @pl.when(odd_i == 0)
    def _():
        sweep([i], diagonal=True)

    def body(jj, carry):
        j = i - 1 - odd_i - 2 * jj
        sweep([j, j - 1])
        return carry

    lax.fori_loop(0, i // 2, body, 0)

    acc = acc_ref[...]
    o_ref[...] = jnp.where(is_a, acc[0:tq], acc[tq:2 * tq]).astype(o_ref.dtype)


def _sb_attention(qkv, B, S, n_heads):
    T = B * S
    tq, bk = ATT_TQ, ATT_BK
    nq = S // tq
    n_pairs = n_heads * HEAD_DIM // LANES
    kern = functools.partial(_sb_kernel, tq=tq, bk=bk, scale=HEAD_DIM ** -0.5)
    return pl.pallas_call(
        kern,
        grid=(B, n_pairs, nq),
        in_specs=[
            pl.BlockSpec((tq, LANES), lambda b, p, i: (b * nq + i, p)),
            pl.BlockSpec((S, LANES), lambda b, p, i: (b, n_pairs + p)),
            pl.BlockSpec((S, LANES), lambda b, p, i: (b, 2 * n_pairs + p)),
        ],
        out_specs=pl.BlockSpec((tq, LANES), lambda b, p, i: (b * nq + i, p)),
        out_shape=jax.ShapeDtypeStruct((T, n_heads * HEAD_DIM), BF16),
        scratch_shapes=[pltpu.VMEM((2 * tq, LANES), F32), pltpu.VMEM((2 * tq, 1), F32)],
        compiler_params=_params("parallel", "parallel", "arbitrary"),
        name="sb_attn",
    )(qkv, qkv, qkv)


def _head_rmsnorm(x, bd, gain):
    ms = _dot_split(x * x, bd)
    return x * lax.rsqrt(ms + EPS) * gain


def _rope(y, c, sa, sb):
    w = y.shape[-1]
    half = ROPE_DIMS // 2
    return y * c + pltpu.roll(y, w - half, 1) * sa + pltpu.roll(y, half, 1) * sb


def _rope_coeffs(positions, n_heads):
    inv_freq = jnp.power(ROPE_THETA, -jnp.arange(0, ROPE_DIMS, 2, dtype=F32) / ROPE_DIMS)
    ang = positions.astype(F32)[:, None] * inv_freq[None, :]
    cos, sin = jnp.cos(ang), jnp.sin(ang)
    p = positions.shape[0]
    rest = HEAD_DIM - ROPE_DIMS
    zeros_h = jnp.zeros((p, ROPE_DIMS // 2), F32)
    c = jnp.concatenate([cos, cos, jnp.ones((p, rest), F32)], axis=-1)
    sa = jnp.concatenate([-sin, zeros_h, jnp.zeros((p, rest), F32)], axis=-1)
    sb = jnp.concatenate([zeros_h, sin, jnp.zeros((p, rest), F32)], axis=-1)
    return tuple(jnp.tile(t, (1, n_heads)) for t in (c, sa, sb))


def _block_mean_matrix(width):
    idx = np.arange(width) // HEAD_DIM
    return jnp.asarray((idx[:, None] == idx[None, :]).astype(np.float32) / HEAD_DIM, dtype=BF16)


def _kv_prep_kernel(sel_ref, win_ref, bd_ref, gs_ref, gw_ref, c_ref, sa_ref, sb_ref, o_ref):
    w = GROUP_LANES
    bd = bd_ref[...]
    c, sa, sb = c_ref[...], sa_ref[...], sb_ref[...]
    ks = _rope(_head_rmsnorm(sel_ref[:, :w], bd, gs_ref[...]), c, sa, sb)
    kw = _rope(_head_rmsnorm(win_ref[:, :w], bd, gw_ref[...]), c, sa, sb)
    o_ref[:, 0 * w:1 * w] = ks.astype(BF16)
    o_ref[:, 1 * w:2 * w] = sel_ref[:, w:].astype(BF16)
    o_ref[:, 2 * w:3 * w] = kw.astype(BF16)
    o_ref[:, 3 * w:4 * w] = win_ref[:, w:].astype(BF16)


def _kv_prep(kv, k_norm, coeffs, S):
    T = kv.shape[0]
    w = GROUP_LANES
    tm = min(PROJ_TM, S)
    ns = S // tm
    gs = jnp.tile(k_norm[1], NSA_KV_HEADS).reshape(1, w)
    gw = jnp.tile(k_norm[2], NSA_KV_HEADS).reshape(1, w)
    tab = pl.BlockSpec((tm, w), lambda i: (i % ns, 0))
    one = pl.BlockSpec((1, w), lambda i: (0, 0))
    return pl.pallas_call(
        _kv_prep_kernel,
        grid=(T // tm,),
        in_specs=[
            pl.BlockSpec((tm, 2 * w), lambda i: (i, 1)),
            pl.BlockSpec((tm, 2 * w), lambda i: (i, 2)),
            pl.BlockSpec((w, w), lambda i: (0, 0)),
            one, one, tab, tab, tab,
        ],
        out_specs=pl.BlockSpec((tm, 4 * w), lambda i: (i, 0)),
        out_shape=jax.ShapeDtypeStruct((T, 4 * w), BF16),
        compiler_params=_params("parallel"),
        name="kv_prep",
    )(kv, kv, _block_mean_matrix(w), gs, gw, *coeffs)


def _compress_kernel(x_ref, pos_ref, w1_ref, w2_ref, bd_ref, g_ref, c_ref, sa_ref, sb_ref,
                     o_ref, top_ref, bot_ref):
    t = pl.program_id(0)
    l = pl.program_id(1)
    half = CMP_BLOCK // 2

    @pl.when(l == 0)
    def _():
        top_ref[...] = jnp.zeros_like(top_ref)
        bot_ref[...] = jnp.zeros_like(bot_ref)

    x = x_ref[0]
    top_ref[...] += _dot((x + pos_ref[0, 0]).astype(BF16), w1_ref[0, 0])
    bot_ref[...] += _dot((x + pos_ref[0, 1]).astype(BF16), w1_ref[0, 1])

    @pl.when(l == half - 1)
    def _():
        rows = bot_ref.shape[0]
        hid = top_ref[...] + pltpu.roll(bot_ref[...], rows - 1, 0)
        act = (hid * _sigmoid(hid)).astype(BF16)
        y = _dot(act, w2_ref[0])
        yk = _rope(_head_rmsnorm(y, bd_ref[...], g_ref[...]), c_ref[...], sa_ref[...], sb_ref[...])
        o_ref[0] = jnp.where(t == 0, yk, y).astype(BF16)


def _compress(kv, B, S, pos_k, pos_v, k_w1, k_w2, v_w1, v_w2, k_gain):
    w = GROUP_LANES
    G = NSA_KV_HEADS
    half = CMP_BLOCK // 2
    nc = S // CMP_STRIDE
    rows = B * nc
    hid = k_w1.shape[1]
    raw = kv[:, :2 * w].reshape(B, nc, half, 2, w)
    xc = jnp.transpose(raw, (3, 2, 0, 1, 4)).reshape(2, half, rows, w)
    eye = jnp.eye(G, dtype=F32)

    def blockdiag(wm):
        out = jnp.einsum('gh,...ab->...gahb', eye, wm)
        return out.reshape(*wm.shape[:-2], G * wm.shape[-2], G * wm.shape[-1])

    def prep_w1(w1):
        w1 = w1.reshape(2, half, HEAD_DIM, hid)
        return jnp.transpose(blockdiag(w1), (1, 0, 2, 3))

    w1 = jnp.stack([prep_w1(k_w1), prep_w1(v_w1)]).astype(BF16)
    w2 = jnp.stack([blockdiag(k_w2), blockdiag(v_w2)]).astype(BF16)

    def prep_pos(p):
        p = jnp.tile(p, (1, G)).reshape(2, half, 1, w)
        return jnp.transpose(p, (1, 0, 2, 3))

    pos = jnp.stack([prep_pos(pos_k), prep_pos(pos_v)])
    c_end = jnp.arange(nc) * CMP_STRIDE + (CMP_BLOCK - 1)
    coeffs = tuple(jnp.tile(tb, (B, 1)) for tb in _rope_coeffs(c_end, G))
    gain = jnp.tile(k_gain, G).reshape(1, w)
    const2 = lambda t, l: (0, 0)
    out = pl.pallas_call(
        _compress_kernel,
        grid=(2, half),
        in_specs=[
            pl.BlockSpec((None, 1, rows, w), lambda t, l: (t, l, 0, 0)),
            pl.BlockSpec((None, 1, 2, 1, w), lambda t, l: (t, l, 0, 0, 0)),
            pl.BlockSpec((None, 1, 2, w, G * hid), lambda t, l: (t, l, 0, 0, 0)),
            pl.BlockSpec((1, G * hid, w), lambda t, l: (t, 0, 0)),
            pl.BlockSpec((w, w), const2),
            pl.BlockSpec((1, w), const2),
            pl.BlockSpec((rows, w), const2),
            pl.BlockSpec((rows, w), const2),
            pl.BlockSpec((rows, w), const2),
        ],
        out_specs=pl.BlockSpec((1, rows, w), lambda t, l: (t, 0, 0)),
        out_shape=jax.ShapeDtypeStruct((2, rows, w), BF16),
        scratch_shapes=[pltpu.VMEM((rows, G * hid), F32), pltpu.VMEM((rows, G * hid), F32)],
        compiler_params=_params("arbitrary", "arbitrary"),
        name="nsa_compress",
    )(xc, pos, w1, w2, _block_mean_matrix(w), gain, *coeffs)
    return out[0], out[1]


def _nsa_kernel_old(q_ref, gl_ref, kc_ref, vc_ref, ks_ref, vs_ref, kw_ref, vw_ref,
                bd_ref, qg_ref, c_ref, sa_ref, sb_ref, ov_ref,
                o_ref,
                kc_rep, vc_rep, ks_rep, vs_rep, kw_rep, vw_rep, m_ref, l_ref, acc_ref,
                *, tq, bk, nsb, scale):
    g = pl.program_id(1)
    i = pl.program_id(2)
    w = GROUP_LANES
    R = NSA_GROUP
    rows = R * tq
    lane_head = _iota((1, w), 1) >> 6

    @pl.when(i == 0)
    def _():
        rr, cc = _iota((w, w), 0), _iota((w, w), 1)
        pick = jnp.where(((rr >> 6) == g) & ((rr & 63) == (cc & 63)), 1.0, 0.0).astype(BF16)
        for src, dst in ((kc_ref, kc_rep), (vc_ref, vc_rep), (ks_ref, ks_rep),
                         (vs_ref, vs_rep), (kw_ref, kw_rep), (vw_ref, vw_rep)):
            dst[...] = _dot(src[...], pick).astype(BF16)

    qn = _rope(_head_rmsnorm(q_ref[...], bd_ref[...], qg_ref[...]), c_ref[...], sa_ref[...], sb_ref[...])
    qn = qn.astype(BF16)
    qm = jnp.concatenate(
        [jnp.where(lane_head == r, qn, jnp.zeros_like(qn)) for r in range(R)], axis=0)
    t_row = i * tq + (_iota((rows, 1), 0) & (tq - 1))

    def unstack(x):
        out = jnp.where(lane_head == 0, x[0:tq], 0.0)
        for r in range(1, R):
            out = out + jnp.where(lane_head == r, x[r * tq:(r + 1) * tq], 0.0)
        return out

    ncp = kc_rep.shape[0]
    s = _dot_nt(qm, kc_rep[...]) * scale
    c_end = _iota((1, ncp), 1) * CMP_STRIDE + (CMP_BLOCK - 1)
    valid = c_end <= t_row
    s = jnp.where(valid, s, NEG)
    e = jnp.exp(s - jnp.max(s, axis=1, keepdims=True))
    p = e / jnp.sum(e, axis=1, keepdims=True)
    p = jnp.where(t_row >= CMP_BLOCK - 1, p, 0.0)
    o_cmp = unstack(_dot(p.astype(BF16), vc_rep[...]))
    p_sum = p[0:tq]
    for r in range(1, R):
        p_sum = p_sum + p[r * tq:(r + 1) * tq]

    hi, lo = _split_bf16(p_sum)
    ov = ov_ref[...]
    imp = (_dot_nt(ov, hi) + _dot_nt(ov, lo))[0:nsb]
    jb = _iota((nsb, 1), 0)
    cur = (i * tq + _iota((1, tq), 1)) >> 6
    forced = (jb == 0) | ((cur - jb >= 0) & (cur - jb < N_LOCAL_SEL))
    score = jnp.where(forced, FORCE, jnp.where(jb <= cur, imp, -FORCE))
    cnt = jnp.zeros((nsb, tq), F32)
    for i2 in range(nsb):
        row = score[i2:i2 + 1, :]
        beats = (row > score) | ((row == score) & (jb > i2))
        cnt = cnt + jnp.where(beats, 1.0, 0.0)
    sel_t = jnp.where(cnt < float(min(N_SEL, nsb)), 1.0, 0.0)
    if nsb < LANES:
        sel_t = jnp.concatenate([sel_t, jnp.zeros((LANES - nsb, tq), F32)], axis=0)
    sel = sel_t.T.astype(BF16)

    def flash_block(k_rep, v_rep, jblk, ok_fn):
        off = pl.multiple_of(jblk * bk, bk)
        kb = k_rep[pl.ds(off, bk), :]
        vb = v_rep[pl.ds(off, bk), :]
        s_pos = jblk * bk + _iota((1, bk), 1)
        ok = ok_fn(s_pos, jblk)
        sc = jnp.where(ok, _dot_nt(qm, kb) * scale, NEG)
        m_old = m_ref[...]
        m_new = jnp.maximum(m_old, jnp.max(sc, axis=1, keepdims=True))
        alpha = jnp.exp(m_old - m_new)
        pe = jnp.where(ok, jnp.exp(sc - m_new), 0.0)
        l_ref[...] = alpha * l_ref[...] + jnp.sum(pe, axis=1, keepdims=True)
        acc_ref[...] = alpha * acc_ref[...] + _dot(pe.astype(BF16), vb)
        m_ref[...] = m_new

    def reset():
        m_ref[...] = jnp.full_like(m_ref, NEG)
        l_ref[...] = jnp.zeros_like(l_ref)
        acc_ref[...] = jnp.zeros_like(acc_ref)

    def sel_ok(s_pos, jblk):
        blk = (jblk * bk + _iota((LANES, bk), 1)) >> 6
        expand = jnp.where(_iota((LANES, bk), 0) == blk, 1.0, 0.0).astype(BF16)
        md = _dot(sel, expand)
        md = jnp.concatenate([md] * R, axis=0)
        return (md > 0.5) & (s_pos <= t_row)

    reset()

    def sel_body(jblk, carry):
        flash_block(ks_rep, vs_rep, jblk, sel_ok)
        return carry

    lax.fori_loop(0, (i * tq + tq) // bk, sel_body, 0)
    o_sel = unstack(acc_ref[...] / l_ref[...])

    def win_ok(s_pos, jblk):
        return (s_pos <= t_row) & (t_row - s_pos < WINDOW)

    reset()

    def win_body(jblk, carry):
        flash_block(kw_rep, vw_rep, jblk, win_ok)
        return carry

    lax.fori_loop(jnp.maximum(i * tq - (WINDOW - 1), 0) // bk, (i * tq + tq) // bk, win_body, 0)
    o_win = unstack(acc_ref[...] / l_ref[...])

    ghi, glo = _split_bf16(_sigmoid(gl_ref[...]))
    col = _iota((LANES, w), 0)
    out = jnp.zeros((tq, w), F32)
    for br, o_br in enumerate((o_cmp, o_sel, o_win)):
        target = br * (NSA_KV_HEADS * R) + g * R + (_iota((LANES, w), 1) >> 6)
        spread = jnp.where(col == target, 1.0, 0.0).astype(BF16)
        out = out + (_dot(ghi, spread) + _dot(glo, spread)) * o_br
    o_ref[...] = out.astype(o_ref.dtype)


MASK_BIG = 2.0 ** 30
V_ROWS = HEAD_DIM + 16


def _nsa_kernel(q_ref, gl_ref, kc_ref, vc_ref, ks_ref, vs_ref, kw_ref, vw_ref,
                bd_ref, qg_ref, c_ref, sa_ref, sb_ref, ov_ref,
                o_ref,
                kc_a, vc_rep, ks_a, vs_rep, kw_a, vw_rep, s_scr, mx_scr, l_scr, acc_scr,
                *, tq, bk, nsb, scale):
    g = pl.program_id(1)
    i = pl.program_id(2)
    w = GROUP_LANES
    R = NSA_GROUP
    rows = R * tq
    lane = _iota((1, w), 1)
    lane_head = lane >> 6

    @pl.when(i == 0)
    def _():
        rr, cc = _iota((w, w), 0), _iota((w, w), 1)
        mine = ((rr >> 6) == g) & ((rr & 63) == (cc & 63))
        to_slot0 = jnp.where(mine & (cc < HEAD_DIM), 1.0, 0.0).astype(BF16)
        to_all = jnp.where(mine, 1.0, 0.0).astype(BF16)
        kc_a[...] = _dot(kc_ref[...], to_slot0).astype(BF16)
        kw_a[...] = _dot(kw_ref[...], to_slot0).astype(BF16)
        seq = ks_a.shape[0]
        tag = jnp.where(_iota((seq, w), 1) - HEAD_DIM == (_iota((seq, w), 0) >> 6), 1.0, 0.0)
        ks_a[...] = (_dot(ks_ref[...], to_slot0) + tag).astype(BF16)
        for src, dst in ((vc_ref, vc_rep), (vs_ref, vs_rep), (vw_ref, vw_rep)):
            dst[...] = _dot(src[...], to_all).astype(BF16)

    qn = _rope(_head_rmsnorm(q_ref[...], bd_ref[...], qg_ref[...]), c_ref[...], sa_ref[...], sb_ref[...])
    qn = qn * scale
    slot0 = lane < HEAD_DIM
    parts = []
    for r in range(R):
        qr = qn if r == 0 else pltpu.roll(qn, w - r * HEAD_DIM, 1)
        parts.append(jnp.where(slot0, qr, 0.0))
    qs_f32 = jnp.concatenate(parts, axis=0)
    qs = qs_f32.astype(BF16)
    t_row = i * tq + (_iota((rows, 1), 0) & (tq - 1))

    def unstack(x):
        out = jnp.where(lane_head == 0, x[0:tq], 0.0)
        for r in range(1, R):
            out = out + jnp.where(lane_head == r, x[r * tq:(r + 1) * tq], 0.0)
        return out

    def stack(x):
        return jnp.concatenate([x] * R, axis=0)

    tri_ok = _iota((tq, bk), 1) <= _iota((tq, bk), 0)
    diag_bias = stack(jnp.where(tri_ok, 0.0, NEG))

    far_bias = jnp.where(tri_ok, NEG, 0.0)
    half_rows = rows // 2
    win_off, win_bias = [], []
    for d in (2, 1, 0):
        jblk = i - d
        win_off.append(pl.multiple_of(jnp.maximum(jblk, 0) * bk, bk))
        missing = jnp.where(jblk >= 0, 0.0, NEG)
        bias = (far_bias + missing) if d == 2 else (jnp.where(tri_ok, 0.0, NEG) if d == 0 else None)
        win_bias.append(missing if bias is None else jnp.concatenate([bias] * (R // 2), axis=0))
    win_sc = [[_dot_nt(qs[h * half_rows:(h + 1) * half_rows], kw_a[pl.ds(off, bk), :]) + bias
               for off, bias in zip(win_off, win_bias)] for h in range(2)]
    v_cat = jnp.concatenate([vw_rep[pl.ds(off, bk), :] for off in win_off], axis=0)

    def win_half(h):
        scs = win_sc[h]
        m_h = jnp.max(jnp.maximum(jnp.maximum(scs[0], scs[1]), scs[2]), axis=1, keepdims=True)
        mb = jnp.broadcast_to(m_h, (half_rows, bk))
        pes = [jnp.exp(sc - mb) for sc in scs]
        l_h = jnp.sum(pes[0] + pes[1] + pes[2], axis=1, keepdims=True)
        p_cat = jnp.concatenate([pe.astype(BF16) for pe in pes], axis=1)
        return _dot(p_cat, v_cat) / l_h

    ncp = kc_a.shape[0]
    s = _dot_nt(qs, kc_a[...])
    win_0 = win_half(0)
    c_end = _iota((1, ncp), 1) * CMP_STRIDE + (CMP_BLOCK - 1)
    s = jnp.where(c_end <= t_row, s, NEG)
    e = jnp.exp(s - jnp.max(s, axis=1, keepdims=True))
    p = e / jnp.sum(e, axis=1, keepdims=True)
    p = jnp.where(t_row >= CMP_BLOCK - 1, p, 0.0)
    o_cmp = unstack(_dot(p.astype(BF16), vc_rep[...]))
    p_sum = p[0:tq]
    for r in range(1, R):
        p_sum = p_sum + p[r * tq:(r + 1) * tq]

    hi, lo = _split_bf16(p_sum)
    ov = ov_ref[...]
    imp = (_dot_nt(ov, hi) + _dot_nt(ov, lo))[0:nsb]
    win_1 = win_half(1)
    o_win = unstack(jnp.concatenate([win_0, win_1], axis=0))
    jb = _iota((nsb, 1), 0)
    cur = (i * tq + _iota((1, tq), 1)) >> 6
    forced = (jb == 0) | ((cur - jb >= 0) & (cur - jb < N_LOCAL_SEL))
    score = jnp.where(forced, FORCE, jnp.where(jb <= cur, imp, -FORCE))
    cnt = jnp.zeros((nsb, tq), F32)
    for i2 in range(nsb):
        row = score[i2:i2 + 1, :]
        beats = (row > score) | ((row == score) & (jb > i2))
        cnt = cnt + jnp.where(beats, 1.0, 0.0)
    sel_t = jnp.where((cnt < float(min(N_SEL, nsb))) & (jb <= cur), 0.0, -MASK_BIG)
    pad_t = jnp.concatenate([jnp.zeros((HEAD_DIM, tq), F32), sel_t,
                             jnp.zeros((LANES - HEAD_DIM - nsb, tq), F32)], axis=0)
    aug = jnp.concatenate([pad_t.T, jnp.zeros((tq, w - LANES), F32)], axis=1)
    qsel = (qs_f32 + stack(aug)).astype(BF16)

    n_blk = i + 1
    n_step = (n_blk + 1) // 2
    zero_bias = jnp.zeros((rows, bk), F32)
    neg_bias = jnp.full((rows, bk), NEG, F32)

    def blk_off(x):
        return pl.multiple_of(jnp.minimum(x, i) * bk, bk)

    mx_scr[...] = neg_bias

    def pass1(t, carry):
        mx = mx_scr[...]
        for x in (2 * t, 2 * t + 1):
            sc = _dot_nt(qsel, ks_a[pl.ds(blk_off(x), bk), :])
            sc = sc + jnp.where(x == i, diag_bias, jnp.where(x > i, neg_bias, zero_bias))
            s_scr[x] = sc
            mx = jnp.maximum(mx, sc)
        mx_scr[...] = mx
        return carry

    lax.fori_loop(0, n_step, pass1, 0)
    m = jnp.max(mx_scr[...], axis=1, keepdims=True)
    mx_scr[...] = jnp.broadcast_to(m, (rows, bk))
    l_scr[...] = jnp.zeros_like(l_scr)
    acc_scr[...] = jnp.zeros_like(acc_scr)

    def pass2(t, carry):
        mb2 = mx_scr[...]
        pe0 = jnp.exp(s_scr[2 * t] - mb2)
        pe1 = jnp.exp(s_scr[2 * t + 1] - mb2)
        l_scr[...] += pe0 + pe1
        p_cat = jnp.concatenate([pe0.astype(BF16), pe1.astype(BF16)], axis=1)
        v_cat = jnp.concatenate([vs_rep[pl.ds(blk_off(2 * t), bk), :],
                                 vs_rep[pl.ds(blk_off(2 * t + 1), bk), :]], axis=0)
        acc_scr[...] += _dot(p_cat, v_cat)
        return carry

    lax.fori_loop(0, n_step, pass2, 0)
    o_sel = unstack(acc_scr[...] / jnp.sum(l_scr[...], axis=1, keepdims=True))

    ghi, glo = _split_bf16(_sigmoid(gl_ref[...]))
    col = _iota((LANES, w), 0)
    out = jnp.zeros((tq, w), F32)
    for br, o_br in enumerate((o_cmp, o_sel, o_win)):
        target = br * (NSA_KV_HEADS * R) + g * R + (_iota((LANES, w), 1) >> 6)
        spread = jnp.where(col == target, 1.0, 0.0).astype(BF16)
        out = out + (_dot(ghi, spread) + _dot(glo, spread)) * o_br
    o_ref[...] = out.astype(o_ref.dtype)


def _nsa_kernel_t(q_ref, gl_ref, kc_ref, vc_ref, ks_ref, vs_ref, kw_ref, vw_ref,
                  bd_ref, qg_ref, c_ref, sa_ref, sb_ref, ov_ref,
                  o_ref,
                  kc_a, ks_a, kw_a, vct, vst, vwt, s_scr, acc_scr,
                  *, tq, bk, nsb, scale):
    g = pl.program_id(1)
    i = pl.program_id(2)
    w = GROUP_LANES
    R = NSA_GROUP
    rows = R * tq
    lane = _iota((1, w), 1)
    n_tiles = ks_a.shape[0] // bk

    @pl.when(i == 0)
    def _():
        rr, cc = _iota((w, w), 0), _iota((w, w), 1)
        mine = ((rr >> 6) == g) & ((rr & 63) == (cc & 63))
        to_slot0 = jnp.where(mine & (cc < HEAD_DIM), 1.0, 0.0).astype(BF16)
        kc_a[...] = _dot(kc_ref[...], to_slot0).astype(BF16)
        kw_a[...] = _dot(kw_ref[...], to_slot0).astype(BF16)
        seq = ks_a.shape[0]
        tag = jnp.where(_iota((seq, w), 1) - HEAD_DIM == (_iota((seq, w), 0) >> 6), 1.0, 0.0)
        ks_a[...] = (_dot(ks_ref[...], to_slot0) + tag).astype(BF16)
        pick_t = jnp.where(_iota((HEAD_DIM, w), 1) == g * HEAD_DIM + _iota((HEAD_DIM, w), 0),
                           1.0, 0.0).astype(BF16)
        vct[...] = _dot_nt(pick_t, vc_ref[...]).astype(BF16)
        pick_1 = jnp.where(_iota((V_ROWS, w), 1) == g * HEAD_DIM + _iota((V_ROWS, w), 0), 1.0, 0.0)
        pick_1 = jnp.where(_iota((V_ROWS, w), 0) < HEAD_DIM, pick_1, 0.0).astype(BF16)
        ones_row = jnp.where(_iota((V_ROWS, bk), 0) == HEAD_DIM, 1.0, 0.0)
        for j in range(n_tiles):
            vst[j] = (_dot_nt(pick_1, vs_ref[j * bk:(j + 1) * bk, :]) + ones_row).astype(BF16)
            vwt[j] = (_dot_nt(pick_1, vw_ref[j * bk:(j + 1) * bk, :]) + ones_row).astype(BF16)

    qn = _rope(_head_rmsnorm(q_ref[...], bd_ref[...], qg_ref[...]), c_ref[...], sa_ref[...], sb_ref[...])
    qn = qn * scale
    slot0 = lane < HEAD_DIM
    parts = []
    for r in range(R):
        qr = qn if r == 0 else pltpu.roll(qn, w - r * HEAD_DIM, 1)
        parts.append(jnp.where(slot0, qr, 0.0))
    qs_f32 = jnp.concatenate(parts, axis=0)
    qs = qs_f32.astype(BF16)
    t_col = i * tq + (_iota((1, rows), 1) & (tq - 1))

    def lanes4(x):
        return jnp.concatenate([x] * R, axis=1)

    def part(x, op):
        x = x.reshape(x.shape[0] // 8, 8, rows)
        return jnp.max(x, axis=0) if op == "max" else jnp.sum(x, axis=0)

    key_le_query = _iota((bk, tq), 0) <= _iota((bk, tq), 1)
    diag_bias = lanes4(jnp.where(key_le_query, 0.0, NEG))

    ncp = kc_a.shape[0]
    s = _dot_nt(kc_a[...], qs)

    far_bias = jnp.where(key_le_query, NEG, 0.0)

    def win_tile(d):
        jblk = i - d
        blk = jnp.maximum(jblk, 0)
        sc = _dot_nt(kw_a[pl.ds(pl.multiple_of(blk * bk, bk), bk), :], qs)
        missing = jnp.where(jblk >= 0, 0.0, NEG)
        if d == 2:
            sc = sc + lanes4(far_bias + missing)
        elif d == 1:
            sc = sc + missing
        else:
            sc = sc + diag_bias
        return sc, blk

    win = [win_tile(2), win_tile(1)]

    c_end = _iota((ncp, 1), 0) * CMP_STRIDE + (CMP_BLOCK - 1)
    s = jnp.where(c_end <= t_col, s, NEG)
    e = jnp.exp(s - jnp.max(s, axis=0, keepdims=True))
    p = e / jnp.sum(e, axis=0, keepdims=True)
    p = jnp.where(t_col >= CMP_BLOCK - 1, p, 0.0)
    o_cmp = _dot(vct[...], p.astype(BF16))
    p_sum = p[:, 0:tq]
    for r in range(1, R):
        p_sum = p_sum + p[:, r * tq:(r + 1) * tq]

    hi, lo = _split_bf16(p_sum)
    ov = ov_ref[...]
    imp = (_dot(ov, hi) + _dot(ov, lo))[0:nsb]
    win.append(win_tile(0))
    jb = _iota((nsb, 1), 0)
    cur = (i * tq + _iota((1, tq), 1)) >> 6
    forced = (jb == 0) | ((cur - jb >= 0) & (cur - jb < N_LOCAL_SEL))
    score = jnp.where(forced, FORCE, jnp.where(jb <= cur, imp, -FORCE))
    cnt = jnp.zeros((nsb, tq), F32)
    for i2 in range(nsb):
        row = score[i2:i2 + 1, :]
        beats = (row > score) | ((row == score) & (jb > i2))
        cnt = cnt + jnp.where(beats, 1.0, 0.0)
    sel_t = jnp.where((cnt < float(min(N_SEL, nsb))) & (jb <= cur), 0.0, -MASK_BIG)
    pad_t = jnp.concatenate([jnp.zeros((HEAD_DIM, tq), F32), sel_t,
                             jnp.zeros((LANES - HEAD_DIM - nsb, tq), F32)], axis=0)
    aug = jnp.concatenate([pad_t.T, jnp.zeros((tq, w - LANES), F32)], axis=1)
    qsel = (qs_f32 + jnp.concatenate([aug] * R, axis=0)).astype(BF16)

    win_sc = [sc for sc, _ in win]
    m_w = jnp.max(jnp.maximum(jnp.maximum(win_sc[0], win_sc[1]), win_sc[2]), axis=0, keepdims=True)
    pes = [jnp.exp(sc - m_w) for sc in win_sc]
    p_cat = jnp.concatenate([pe.astype(BF16) for pe in pes], axis=0)
    v_cat = jnp.concatenate([vwt[blk] for _, blk in win], axis=1)
    acc_w = _dot(v_cat, p_cat)
    o_win = acc_w[0:HEAD_DIM] / acc_w[HEAD_DIM:HEAD_DIM + 1]

    n_step = (i + 2) // 2

    def clamp(x):
        return jnp.minimum(x, i)

    def qk(x):
        return _dot_nt(ks_a[pl.ds(pl.multiple_of(clamp(x) * bk, bk), bk), :], qsel)

    def pass1(t, mx8):
        for x in (2 * t, 2 * t + 1):
            sc = qk(x)
            s_scr[x] = sc
            mx8 = jnp.maximum(mx8, part(sc, "max"))
        return mx8

    mx8 = lax.fori_loop(0, n_step - 1, pass1, jnp.full((8, rows), NEG, F32))
    for x in (2 * (n_step - 1), 2 * (n_step - 1) + 1):
        sc = qk(x) + jnp.where(x == i, diag_bias, jnp.where(x > i, NEG, 0.0))
        s_scr[x] = sc
        mx8 = jnp.maximum(mx8, part(sc, "max"))
    m_s = jnp.max(mx8, axis=0, keepdims=True)
    acc_scr[...] = jnp.zeros_like(acc_scr)

    def pass2(t, carry):
        p0 = jnp.exp(s_scr[2 * t] - m_s)
        p1 = jnp.exp(s_scr[2 * t + 1] - m_s)
        p2 = jnp.concatenate([p0.astype(BF16), p1.astype(BF16)], axis=0)
        v2 = jnp.concatenate([vst[clamp(2 * t)], vst[clamp(2 * t + 1)]], axis=1)
        acc_scr[...] += _dot(v2, p2)
        return carry

    lax.fori_loop(0, n_step, pass2, 0)
    o_sel = acc_scr[0:HEAD_DIM, :] / acc_scr[HEAD_DIM:HEAD_DIM + 1, :]

    def to_rows(o_t):
        return jnp.concatenate([o_t[:, r * tq:(r + 1) * tq] for r in range(R)], axis=0)

    ghi, glo = _split_bf16(_sigmoid(gl_ref[...]))
    row_head = _iota((w, LANES), 0) >> 6
    col = _iota((w, LANES), 1)
    out_t = jnp.zeros((w, tq), F32)
    for br, o_br in enumerate((o_cmp, o_sel, o_win)):
        spread = jnp.where(col == br * (NSA_KV_HEADS * R) + g * R + row_head, 1.0, 0.0).astype(BF16)
        out_t = out_t + (_dot_nt(spread, ghi) + _dot_nt(spread, glo)) * to_rows(o_br)
    o_ref[...] = out_t.T.astype(o_ref.dtype)


def _nsa_attention(proj, kc, vc, ksw, q_gain, coeffs, B, S):
    T = B * S
    w = GROUP_LANES
    G = NSA_KV_HEADS
    tq, bk = ATT_TQ, ATT_BK
    nq = S // tq
    nc = S // CMP_STRIDE
    nsb = S // SEL_BLOCK
    n_cmp = (S - CMP_BLOCK) // CMP_STRIDE + 1
    cs = np.arange(nc) * CMP_STRIDE
    js = np.arange(LANES) * SEL_BLOCK
    ov = ((cs[None, :] < js[:, None] + SEL_BLOCK) & (cs[None, :] + CMP_BLOCK > js[:, None])
          & (np.arange(nc)[None, :] < n_cmp) & (np.arange(LANES)[:, None] < nsb))
    ov = jnp.asarray(ov.astype(np.float32), dtype=BF16)
    gate_blk = (G * NSA_GROUP * HEAD_DIM) // LANES
    assert tq == bk and WINDOW == 2 * bk and HEAD_DIM + nsb <= LANES
    kern = functools.partial(_nsa_kernel_t, tq=tq, bk=bk, nsb=nsb, scale=HEAD_DIM ** -0.5)
    seq = lambda col: pl.BlockSpec((S, w), lambda b, g, i: (b, col))
    const = lambda shape: pl.BlockSpec(shape, lambda b, g, i: (0, 0))
    tab = pl.BlockSpec((tq, w), lambda b, g, i: (i, 0))
    rows = NSA_GROUP * tq
    n_tiles = S // bk
    return pl.pallas_call(
        kern,
        grid=(B, G, nq),
        in_specs=[
            pl.BlockSpec((tq, w), lambda b, g, i: (b * nq + i, g)),
            pl.BlockSpec((tq, LANES), lambda b, g, i: (b * nq + i, gate_blk)),
            pl.BlockSpec((nc, w), lambda b, g, i: (b, 0)),
            pl.BlockSpec((nc, w), lambda b, g, i: (b, 0)),
            seq(0), seq(1), seq(2), seq(3),
            const((w, w)), const((1, w)), tab, tab, tab, const((LANES, nc)),
        ],
        out_specs=pl.BlockSpec((tq, w), lambda b, g, i: (b * nq + i, g)),
        out_shape=jax.ShapeDtypeStruct((T, G * w), BF16),
        scratch_shapes=[
            pltpu.VMEM((nc, w), BF16), pltpu.VMEM((S, w), BF16), pltpu.VMEM((S, w), BF16),
            pltpu.VMEM((HEAD_DIM, nc), BF16),
            pltpu.VMEM((n_tiles, V_ROWS, bk), BF16), pltpu.VMEM((n_tiles, V_ROWS, bk), BF16),
            pltpu.VMEM((n_tiles, bk, rows), F32),
            pltpu.VMEM((V_ROWS, rows), F32),
        ],
        compiler_params=_params("parallel", "arbitrary", "arbitrary"),
        name="nsa_attn",
    )(proj, proj, kc, vc, ksw, ksw, ksw, ksw,
      _block_mean_matrix(w), jnp.tile(q_gain, NSA_GROUP).reshape(1, w), *coeffs, ov)


def kernel(x, ffn1_norm, ffn1_w_in, ffn1_w_out, mix_norm, ffn2_norm, ffn2_w_in, ffn2_w_out,
           sb_w_qkv, sb_w_out, kv_norm, nsa_w_kv, nsa_k_norm, cmp_pos_k, cmp_pos_v,
           cmp_k_w1, cmp_k_w2, cmp_v_w1, cmp_v_w2, nsa_w_in, nsa_q_norm, nsa_w_out):
    B, S, D = x.shape
    depth = ffn1_norm.shape[0]
    n_a = sb_w_qkv.shape[0]
    n_sb_heads = sb_w_out.shape[1] // HEAD_DIM
    assert S % 512 == 0 and S // CMP_STRIDE == LANES, "sequence tiling assumes 16*128 tokens"
    h = x.reshape(B * S, D)
    bf = lambda a: a.astype(BF16)
    coeffs = _rope_coeffs(jnp.arange(S), NSA_KV_HEADS)
    n_q = nsa_w_in.shape[-1]
    n_q_pad = -(-n_q // LANES) * LANES
    kc = vc = ksw = None
    w1_in, w1_out, w2_in, w2_out = bf(ffn1_w_in), bf(ffn1_w_out), bf(ffn2_w_in), bf(ffn2_w_out)
    w_qkv, w_sb_out, w_nsa_out = bf(sb_w_qkv), bf(sb_w_out), bf(nsa_w_out)
    w_nsa_in = jnp.pad(bf(nsa_w_in), ((0, 0), (0, 0), (0, n_q_pad - n_q)))
    w_kv = bf(nsa_w_kv)[None]
    for layer in range(depth):
        h = _ffn(h, ffn1_norm[layer], w1_in, w1_out, layer)
        if layer < n_a:
            qkv = _norm_proj(h, mix_norm[layer], w_qkv, layer, 1024, BF16)
            o = _sb_attention(qkv, B, S, n_sb_heads)
            h = _out_proj(h, o, w_sb_out, layer)
        else:
            li = layer - n_a
            proj = _norm_proj(h, mix_norm[layer], w_nsa_in, li, n_q_pad, F32)
            o = _nsa_attention(proj, kc, vc, ksw, nsa_q_norm[li], coeffs, B, S)
            h = _out_proj(h, o, w_nsa_out, li)
        h = _ffn(h, ffn2_norm[layer], w2_in, w2_out, layer)
        if layer == n_a - 1:
            kv = _norm_proj(h, kv_norm, w_kv, 0, 768, F32)
            kc, vc = _compress(kv, B, S, cmp_pos_k, cmp_pos_v, cmp_k_w1, cmp_k_w2,
                               cmp_v_w1, cmp_v_w2, nsa_k_norm[0])
            ksw = _kv_prep(kv, nsa_k_norm, coeffs, S)
    return h.reshape(B, S, D)
```

```python
import functools

import jax
import jax.numpy as jnp
import numpy as np
from jax import lax
from jax.experimental import pallas as pl
from jax.experimental.pallas import tpu as pltpu

F32 = jnp.float32
BF16 = jnp.bfloat16

HEAD_DIM = 64
NSA_KV_HEADS = 4
NSA_GROUP = 4
N_BRANCHES = 3
ROPE_DIMS = 16
ROPE_THETA = 500000.0
CMP_BLOCK = 32
CMP_STRIDE = 16
SEL_BLOCK = 64
N_SEL = 8
N_LOCAL_SEL = 2
WINDOW = 512
EPS = 1e-6
NEG = -1e30
FORCE = 1e4

LANES = 128
GROUP_LANES = NSA_GROUP * HEAD_DIM
VMEM_LIMIT_BYTES = 56 * 1024 * 1024

FFN_TM = 512
PROJ_TM = 1024
ATT_TQ = 256
ATT_BK = 256


def _dot(a, b):
    return jnp.dot(a, b, preferred_element_type=F32)


def _dot_nt(a, b):
    return lax.dot_general(a, b, (((1,), (1,)), ((), ())), preferred_element_type=F32)


def _split_bf16(x):
    hi = x.astype(BF16)
    lo = (x - hi.astype(F32)).astype(BF16)
    return hi, lo


def _dot_split(x, w):
    hi, lo = _split_bf16(x)
    n = x.shape[0]
    r = _dot(jnp.concatenate([hi, lo], axis=0), w)
    return r[0:n] + r[n:2 * n]


def _sigmoid(x):
    return 1.0 / (1.0 + jnp.exp(-x))


def _iota(shape, dim):
    return lax.broadcasted_iota(jnp.int32, shape, dim)


def _params(*sem):
    return pltpu.CompilerParams(dimension_semantics=sem, vmem_limit_bytes=VMEM_LIMIT_BYTES)


def _ffn_kernel(h_ref, g_ref, wi_ref, wo_ref, o_ref, *, tf, nf):
    x = h_ref[...]
    ms = jnp.mean(x * x, axis=-1, keepdims=True)
    xn = (x * lax.rsqrt(ms + EPS) * g_ref[...]).astype(BF16)
    ff = tf * nf
    ab = [(_dot(xn, wi_ref[:, c * tf:(c + 1) * tf]), _dot(xn, wi_ref[:, ff + c * tf:ff + (c + 1) * tf]))
          for c in range(nf)]
    acc = None
    for c, (a, b) in enumerate(ab):
        act = (a * _sigmoid(a) * b).astype(BF16)
        part = _dot(act, wo_ref[c * tf:(c + 1) * tf, :])
        acc = part if acc is None else acc + part
    o_ref[...] = x + 0.5 * acc


def _ffn(h, gain, w_in, w_out, layer):
    T, D = h.shape
    F = w_out.shape[1]
    tm = min(FFN_TM, T)
    tf = F // 2 if (F // 2) % LANES == 0 else F
    nf = F // tf
    resident = dict(pipeline_mode=pl.Buffered(1))
    return pl.pallas_call(
        functools.partial(_ffn_kernel, tf=tf, nf=nf),
        grid=(T // tm,),
        in_specs=[
            pl.BlockSpec((tm, D), lambda i: (i, 0)),
            pl.BlockSpec((1, D), lambda i: (0, 0)),
            pl.BlockSpec((None, D, 2 * F), lambda i: (layer, 0, 0), **resident),
            pl.BlockSpec((None, F, D), lambda i: (layer, 0, 0), **resident),
        ],
        out_specs=pl.BlockSpec((tm, D), lambda i: (i, 0)),
        out_shape=jax.ShapeDtypeStruct((T, D), F32),
        compiler_params=_params("parallel"),
        name="ffn",
    )(h, gain.reshape(1, D), w_in, w_out)


def _norm_proj_kernel(h_ref, g_ref, w_ref, o_ref, xn_ref):
    @pl.when(pl.program_id(1) == 0)
    def _():
        x = h_ref[...]
        ms = jnp.mean(x * x, axis=-1, keepdims=True)
        xn_ref[...] = (x * lax.rsqrt(ms + EPS) * g_ref[...]).astype(BF16)

    o_ref[...] = _dot(xn_ref[...], w_ref[...]).astype(o_ref.dtype)


def _norm_proj(h, gain, w, layer, tn, out_dtype):
    T, D = h.shape
    N = w.shape[2]
    tm = min(PROJ_TM, T)
    return pl.pallas_call(
        _norm_proj_kernel,
        grid=(T // tm, N // tn),
        in_specs=[
            pl.BlockSpec((tm, D), lambda i, j: (i, 0)),
            pl.BlockSpec((1, D), lambda i, j: (0, 0)),
            pl.BlockSpec((None, D, tn), lambda i, j: (layer, 0, j)),
        ],
        out_specs=pl.BlockSpec((tm, tn), lambda i, j: (i, j)),
        out_shape=jax.ShapeDtypeStruct((T, N), out_dtype),
        scratch_shapes=[pltpu.VMEM((tm, D), BF16)],
        compiler_params=_params("parallel", "arbitrary"),
        name="norm_proj",
    )(h, gain.reshape(1, D), w)


def _out_proj_kernel(h_ref, x_ref, w_ref, o_ref):
    o_ref[...] = h_ref[...] + _dot(x_ref[...], w_ref[...])


def _out_proj(h, x, w, layer):
    T, D = h.shape
    K = x.shape[1]
    tm = min(PROJ_TM, T)
    return pl.pallas_call(
        _out_proj_kernel,
        grid=(T // tm,),
        in_specs=[
            pl.BlockSpec((tm, D), lambda i: (i, 0)),
            pl.BlockSpec((tm, K), lambda i: (i, 0)),
            pl.BlockSpec((None, K, D), lambda i: (layer, 0, 0)),
        ],
        out_specs=pl.BlockSpec((tm, D), lambda i: (i, 0)),
        out_shape=jax.ShapeDtypeStruct((T, D), F32),
        compiler_params=_params("parallel"),
        name="out_proj",
    )(h, x, w)


def _sb_kernel(q_ref, k_ref, v_ref, o_ref, *, tq, bk, scale, q_tiles):
    i = pl.program_id(2)
    lane = _iota((1, LANES), 1)
    is_a = lane < HEAD_DIM
    q = q_ref[...] * scale
    zero = jnp.zeros_like(q)
    qh = (jnp.where(is_a, q, zero), jnp.where(is_a, zero, q))
    tri = jnp.where(_iota((bk, bk), 0) > _iota((bk, bk), 1), 1.0, 0.0).astype(BF16)

    def sweep_static(n_blocks):
        past = _iota((tq, bk), 1) < _iota((tq, bk), 0)
        chains = [(j, hd) for j in range(n_blocks - 1, -1, -1) for hd in range(2)]
        n = len(chains)
        z, log_b, log_1m, within, w = {}, {}, {}, {}, {}
        carry = [jnp.zeros((tq, 1), F32), jnp.zeros((tq, 1), F32)]
        acc = [jnp.zeros((tq, LANES), F32), jnp.zeros((tq, LANES), F32)]
        for k in range(n + 2):
            if k < n:
                j, hd = chains[k]
                z[k] = _dot_nt(qh[hd], k_ref[j * bk:(j + 1) * bk, :])
            if 0 <= k - 1 < n:
                c = k - 1
                j, hd = chains[c]
                log_b[c] = jnp.minimum(z[c], 0.0) - jnp.log(1.0 + jnp.exp(-jnp.abs(z[c])))
                m = log_b[c] - z.pop(c)
                log_1m[c] = jnp.where(past, m, 0.0) if j == n_blocks - 1 else m
                within[c] = _dot(log_1m[c].astype(BF16), tri)
            if 0 <= k - 2 < n:
                c = k - 2
                j, hd = chains[c]
                wc = jnp.exp(log_b.pop(c) + (carry[hd] + within.pop(c)))
                if j == n_blocks - 1:
                    wc = jnp.where(past, wc, 0.0)
                carry[hd] = carry[hd] + jnp.sum(log_1m.pop(c), axis=1, keepdims=True)
                acc[hd] = acc[hd] + _dot(wc.astype(BF16), v_ref[j * bk:(j + 1) * bk, :])
        return acc

    for n_blocks in range(1, q_tiles + 1):
        @pl.when(i == n_blocks - 1)
        def _(n_blocks=n_blocks):
            acc_a, acc_b = sweep_static(n_blocks)
            o_ref[...] = jnp.where(is_a, acc_a, acc_b).astype(o_ref.dtype)


def _sb_attention(qkv, B, S, n_heads):
    T = B * S
    tq, bk = ATT_TQ, ATT_BK
    nq = S // tq
    n_pairs = n_heads * HEAD_DIM // LANES
    kern = functools.partial(_sb_kernel, tq=tq, bk=bk, scale=HEAD_DIM ** -0.5, q_tiles=nq)
    return pl.pallas_call(
        kern,
        grid=(B, n_pairs, nq),
        in_specs=[
            pl.BlockSpec((tq, LANES), lambda b, p, i: (b * nq + i, p)),
            pl.BlockSpec((S, LANES), lambda b, p, i: (b, n_pairs + p)),
            pl.BlockSpec((S, LANES), lambda b, p, i: (b, 2 * n_pairs + p)),
        ],
        out_specs=pl.BlockSpec((tq, LANES), lambda b, p, i: (b * nq + i, p)),
        out_shape=jax.ShapeDtypeStruct((T, n_heads * HEAD_DIM), BF16),
        compiler_params=_params("parallel", "parallel", "arbitrary"),
        name="sb_attn",
    )(qkv, qkv, qkv)


def _head_rmsnorm(x, bd, gain):
    ms = _dot_split(x * x, bd)
    return x * lax.rsqrt(ms + EPS) * gain


def _rope(y, c, sa, sb):
    w = y.shape[-1]
    half = ROPE_DIMS // 2
    return y * c + pltpu.roll(y, w - half, 1) * sa + pltpu.roll(y, half, 1) * sb


def _rope_coeffs(positions, n_heads):
    inv_freq = jnp.power(ROPE_THETA, -jnp.arange(0, ROPE_DIMS, 2, dtype=F32) / ROPE_DIMS)
    ang = positions.astype(F32)[:, None] * inv_freq[None, :]
    cos, sin = jnp.cos(ang), jnp.sin(ang)
    p = positions.shape[0]
    rest = HEAD_DIM - ROPE_DIMS
    zeros_h = jnp.zeros((p, ROPE_DIMS // 2), F32)
    c = jnp.concatenate([cos, cos, jnp.ones((p, rest), F32)], axis=-1)
    sa = jnp.concatenate([-sin, zeros_h, jnp.zeros((p, rest), F32)], axis=-1)
    sb = jnp.concatenate([zeros_h, sin, jnp.zeros((p, rest), F32)], axis=-1)
    return tuple(jnp.tile(t, (1, n_heads)) for t in (c, sa, sb))


def _block_mean_matrix(width):
    idx = np.arange(width) // HEAD_DIM
    return jnp.asarray((idx[:, None] == idx[None, :]).astype(np.float32) / HEAD_DIM, dtype=BF16)


def _kv_prep_kernel(sel_ref, win_ref, bd_ref, gs_ref, gw_ref, c_ref, sa_ref, sb_ref, o_ref):
    w = GROUP_LANES
    bd = bd_ref[...]
    c, sa, sb = c_ref[...], sa_ref[...], sb_ref[...]
    ks = _rope(_head_rmsnorm(sel_ref[:, :w], bd, gs_ref[...]), c, sa, sb)
    kw = _rope(_head_rmsnorm(win_ref[:, :w], bd, gw_ref[...]), c, sa, sb)
    o_ref[:, 0 * w:1 * w] = ks.astype(BF16)
    o_ref[:, 1 * w:2 * w] = sel_ref[:, w:].astype(BF16)
    o_ref[:, 2 * w:3 * w] = kw.astype(BF16)
    o_ref[:, 3 * w:4 * w] = win_ref[:, w:].astype(BF16)


def _kv_prep(kv, k_norm, coeffs, S):
    T = kv.shape[0]
    w = GROUP_LANES
    tm = min(PROJ_TM, S)
    ns = S // tm
    gs = jnp.tile(k_norm[1], NSA_KV_HEADS).reshape(1, w)
    gw = jnp.tile(k_norm[2], NSA_KV_HEADS).reshape(1, w)
    tab = pl.BlockSpec((tm, w), lambda i: (i % ns, 0))
    one = pl.BlockSpec((1, w), lambda i: (0, 0))
    return pl.pallas_call(
        _kv_prep_kernel,
        grid=(T // tm,),
        in_specs=[
            pl.BlockSpec((tm, 2 * w), lambda i: (i, 1)),
            pl.BlockSpec((tm, 2 * w), lambda i: (i, 2)),
            pl.BlockSpec((w, w), lambda i: (0, 0)),
            one, one, tab, tab, tab,
        ],
        out_specs=pl.BlockSpec((tm, 4 * w), lambda i: (i, 0)),
        out_shape=jax.ShapeDtypeStruct((T, 4 * w), BF16),
        compiler_params=_params("parallel"),
        name="kv_prep",
    )(kv, kv, _block_mean_matrix(w), gs, gw, *coeffs)


def _compress_kernel(x_ref, pos_ref, w1_ref, w2_ref, bd_ref, g_ref, c_ref, sa_ref, sb_ref,
                     o_ref, top_ref, bot_ref):
    t = pl.program_id(0)
    l = pl.program_id(1)
    half = CMP_BLOCK // 2

    @pl.when(l == 0)
    def _():
        top_ref[...] = jnp.zeros_like(top_ref)
        bot_ref[...] = jnp.zeros_like(bot_ref)

    x = x_ref[0]
    top_ref[...] += _dot((x + pos_ref[0, 0]).astype(BF16), w1_ref[0, 0])
    bot_ref[...] += _dot((x + pos_ref[0, 1]).astype(BF16), w1_ref[0, 1])

    @pl.when(l == half - 1)
    def _():
        rows = bot_ref.shape[0]
        hid = top_ref[...] + pltpu.roll(bot_ref[...], rows - 1, 0)
        act = (hid * _sigmoid(hid)).astype(BF16)
        y = _dot(act, w2_ref[0])
        yk = _rope(_head_rmsnorm(y, bd_ref[...], g_ref[...]), c_ref[...], sa_ref[...], sb_ref[...])
        o_ref[0] = jnp.where(t == 0, yk, y).astype(BF16)


def _compress(kv, B, S, pos_k, pos_v, k_w1, k_w2, v_w1, v_w2, k_gain):
    w = GROUP_LANES
    G = NSA_KV_HEADS
    half = CMP_BLOCK // 2
    nc = S // CMP_STRIDE
    rows = B * nc
    hid = k_w1.shape[1]
    raw = kv[:, :2 * w].reshape(B, nc, half, 2, w)
    xc = jnp.transpose(raw, (3, 2, 0, 1, 4)).reshape(2, half, rows, w)
    eye = jnp.eye(G, dtype=F32)

    def blockdiag(wm):
        out = jnp.einsum('gh,...ab->...gahb', eye, wm)
        return out.reshape(*wm.shape[:-2], G * wm.shape[-2], G * wm.shape[-1])

    def prep_w1(w1):
        w1 = w1.reshape(2, half, HEAD_DIM, hid)
        return jnp.transpose(blockdiag(w1), (1, 0, 2, 3))

    w1 = jnp.stack([prep_w1(k_w1), prep_w1(v_w1)]).astype(BF16)
    w2 = jnp.stack([blockdiag(k_w2), blockdiag(v_w2)]).astype(BF16)

    def prep_pos(p):
        p = jnp.tile(p, (1, G)).reshape(2, half, 1, w)
        return jnp.transpose(p, (1, 0, 2, 3))

    pos = jnp.stack([prep_pos(pos_k), prep_pos(pos_v)])
    c_end = jnp.arange(nc) * CMP_STRIDE + (CMP_BLOCK - 1)
    coeffs = tuple(jnp.tile(tb, (B, 1)) for tb in _rope_coeffs(c_end, G))
    gain = jnp.tile(k_gain, G).reshape(1, w)
    const2 = lambda t, l: (0, 0)
    out = pl.pallas_call(
        _compress_kernel,
        grid=(2, half),
        in_specs=[
            pl.BlockSpec((None, 1, rows, w), lambda t, l: (t, l, 0, 0)),
            pl.BlockSpec((None, 1, 2, 1, w), lambda t, l: (t, l, 0, 0, 0)),
            pl.BlockSpec((None, 1, 2, w, G * hid), lambda t, l: (t, l, 0, 0, 0)),
            pl.BlockSpec((1, G * hid, w), lambda t, l: (t, 0, 0)),
            pl.BlockSpec((w, w), const2),
            pl.BlockSpec((1, w), const2),
            pl.BlockSpec((rows, w), const2),
            pl.BlockSpec((rows, w), const2),
            pl.BlockSpec((rows, w), const2),
        ],
        out_specs=pl.BlockSpec((1, rows, w), lambda t, l: (t, 0, 0)),
        out_shape=jax.ShapeDtypeStruct((2, rows, w), BF16),
        scratch_shapes=[pltpu.VMEM((rows, G * hid), F32), pltpu.VMEM((rows, G * hid), F32)],
        compiler_params=_params("arbitrary", "arbitrary"),
        name="nsa_compress",
    )(xc, pos, w1, w2, _block_mean_matrix(w), gain, *coeffs)
    return out[0], out[1]


def _nsa_kernel_old(q_ref, gl_ref, kc_ref, vc_ref, ks_ref, vs_ref, kw_ref, vw_ref,
                bd_ref, qg_ref, c_ref, sa_ref, sb_ref, ov_ref,
                o_ref,
                kc_rep, vc_rep, ks_rep, vs_rep, kw_rep, vw_rep, m_ref, l_ref, acc_ref,
                *, tq, bk, nsb, scale):
    g = pl.program_id(1)
    i = pl.program_id(2)
    w = GROUP_LANES
    R = NSA_GROUP
    rows = R * tq
    lane_head = _iota((1, w), 1) >> 6

    @pl.when(i == 0)
    def _():
        rr, cc = _iota((w, w), 0), _iota((w, w), 1)
        pick = jnp.where(((rr >> 6) == g) & ((rr & 63) == (cc & 63)), 1.0, 0.0).astype(BF16)
        for src, dst in ((kc_ref, kc_rep), (vc_ref, vc_rep), (ks_ref, ks_rep),
                         (vs_ref, vs_rep), (kw_ref, kw_rep), (vw_ref, vw_rep)):
            dst[...] = _dot(src[...], pick).astype(BF16)

    qn = _rope(_head_rmsnorm(q_ref[...], bd_ref[...], qg_ref[...]), c_ref[...], sa_ref[...], sb_ref[...])
    qn = qn.astype(BF16)
    qm = jnp.concatenate(
        [jnp.where(lane_head == r, qn, jnp.zeros_like(qn)) for r in range(R)], axis=0)
    t_row = i * tq + (_iota((rows, 1), 0) & (tq - 1))

    def unstack(x):
        out = jnp.where(lane_head == 0, x[0:tq], 0.0)
        for r in range(1, R):
            out = out + jnp.where(lane_head == r, x[r * tq:(r + 1) * tq], 0.0)
        return out

    ncp = kc_rep.shape[0]
    s = _dot_nt(qm, kc_rep[...]) * scale
    c_end = _iota((1, ncp), 1) * CMP_STRIDE + (CMP_BLOCK - 1)
    valid = c_end <= t_row
    s = jnp.where(valid, s, NEG)
    e = jnp.exp(s - jnp.max(s, axis=1, keepdims=True))
    p = e / jnp.sum(e, axis=1, keepdims=True)
    p = jnp.where(t_row >= CMP_BLOCK - 1, p, 0.0)
    o_cmp = unstack(_dot(p.astype(BF16), vc_rep[...]))
    p_sum = p[0:tq]
    for r in range(1, R):
        p_sum = p_sum + p[r * tq:(r + 1) * tq]

    hi, lo = _split_bf16(p_sum)
    ov = ov_ref[...]
    imp = (_dot_nt(ov, hi) + _dot_nt(ov, lo))[0:nsb]
    jb = _iota((nsb, 1), 0)
    cur = (i * tq + _iota((1, tq), 1)) >> 6
    forced = (jb == 0) | ((cur - jb >= 0) & (cur - jb < N_LOCAL_SEL))
    score = jnp.where(forced, FORCE, jnp.where(jb <= cur, imp, -FORCE))
    cnt = jnp.zeros((nsb, tq), F32)
    for i2 in range(nsb):
        row = score[i2:i2 + 1, :]
        beats = (row > score) | ((row == score) & (jb > i2))
        cnt = cnt + jnp.where(beats, 1.0, 0.0)
    sel_t = jnp.where(cnt < float(min(N_SEL, nsb)), 1.0, 0.0)
    if nsb < LANES:
        sel_t = jnp.concatenate([sel_t, jnp.zeros((LANES - nsb, tq), F32)], axis=0)
    sel = sel_t.T.astype(BF16)

    def flash_block(k_rep, v_rep, jblk, ok_fn):
        off = pl.multiple_of(jblk * bk, bk)
        kb = k_rep[pl.ds(off, bk), :]
        vb = v_rep[pl.ds(off, bk), :]
        s_pos = jblk * bk + _iota((1, bk), 1)
        ok = ok_fn(s_pos, jblk)
        sc = jnp.where(ok, _dot_nt(qm, kb) * scale, NEG)
        m_old = m_ref[...]
        m_new = jnp.maximum(m_old, jnp.max(sc, axis=1, keepdims=True))
        alpha = jnp.exp(m_old - m_new)
        pe = jnp.where(ok, jnp.exp(sc - m_new), 0.0)
        l_ref[...] = alpha * l_ref[...] + jnp.sum(pe, axis=1, keepdims=True)
        acc_ref[...] = alpha * acc_ref[...] + _dot(pe.astype(BF16), vb)
        m_ref[...] = m_new

    def reset():
        m_ref[...] = jnp.full_like(m_ref, NEG)
        l_ref[...] = jnp.zeros_like(l_ref)
        acc_ref[...] = jnp.zeros_like(acc_ref)

    def sel_ok(s_pos, jblk):
        blk = (jblk * bk + _iota((LANES, bk), 1)) >> 6
        expand = jnp.where(_iota((LANES, bk), 0) == blk, 1.0, 0.0).astype(BF16)
        md = _dot(sel, expand)
        md = jnp.concatenate([md] * R, axis=0)
        return (md > 0.5) & (s_pos <= t_row)

    reset()

    def sel_body(jblk, carry):
        flash_block(ks_rep, vs_rep, jblk, sel_ok)
        return carry

    lax.fori_loop(0, (i * tq + tq) // bk, sel_body, 0)
    o_sel = unstack(acc_ref[...] / l_ref[...])

    def win_ok(s_pos, jblk):
        return (s_pos <= t_row) & (t_row - s_pos < WINDOW)

    reset()

    def win_body(jblk, carry):
        flash_block(kw_rep, vw_rep, jblk, win_ok)
        return carry

    lax.fori_loop(jnp.maximum(i * tq - (WINDOW - 1), 0) // bk, (i * tq + tq) // bk, win_body, 0)
    o_win = unstack(acc_ref[...] / l_ref[...])

    ghi, glo = _split_bf16(_sigmoid(gl_ref[...]))
    col = _iota((LANES, w), 0)
    out = jnp.zeros((tq, w), F32)
    for br, o_br in enumerate((o_cmp, o_sel, o_win)):
        target = br * (NSA_KV_HEADS * R) + g * R + (_iota((LANES, w), 1) >> 6)
        spread = jnp.where(col == target, 1.0, 0.0).astype(BF16)
        out = out + (_dot(ghi, spread) + _dot(glo, spread)) * o_br
    o_ref[...] = out.astype(o_ref.dtype)


MASK_BIG = 2.0 ** 30
V_ROWS = HEAD_DIM + 16


def _nsa_kernel(q_ref, gl_ref, kc_ref, vc_ref, ks_ref, vs_ref, kw_ref, vw_ref,
                bd_ref, qg_ref, c_ref, sa_ref, sb_ref, ov_ref,
                o_ref,
                kc_a, vc_rep, ks_a, vs_rep, kw_a, vw_rep, s_scr, mx_scr, l_scr, acc_scr,
                *, tq, bk, nsb, scale):
    g = pl.program_id(1)
    i = pl.program_id(2)
    w = GROUP_LANES
    R = NSA_GROUP
    rows = R * tq
    lane = _iota((1, w), 1)
    lane_head = lane >> 6

    @pl.when(i == 0)
    def _():
        rr, cc = _iota((w, w), 0), _iota((w, w), 1)
        mine = ((rr >> 6) == g) & ((rr & 63) == (cc & 63))
        to_slot0 = jnp.where(mine & (cc < HEAD_DIM), 1.0, 0.0).astype(BF16)
        to_all = jnp.where(mine, 1.0, 0.0).astype(BF16)
        kc_a[...] = _dot(kc_ref[...], to_slot0).astype(BF16)
        kw_a[...] = _dot(kw_ref[...], to_slot0).astype(BF16)
        seq = ks_a.shape[0]
        tag = jnp.where(_iota((seq, w), 1) - HEAD_DIM == (_iota((seq, w), 0) >> 6), 1.0, 0.0)
        ks_a[...] = (_dot(ks_ref[...], to_slot0) + tag).astype(BF16)
        for src, dst in ((vc_ref, vc_rep), (vs_ref, vs_rep), (vw_ref, vw_rep)):
            dst[...] = _dot(src[...], to_all).astype(BF16)

    qn = _rope(_head_rmsnorm(q_ref[...], bd_ref[...], qg_ref[...]), c_ref[...], sa_ref[...], sb_ref[...])
    qn = qn * scale
    slot0 = lane < HEAD_DIM
    parts = []
    for r in range(R):
        qr = qn if r == 0 else pltpu.roll(qn, w - r * HEAD_DIM, 1)
        parts.append(jnp.where(slot0, qr, 0.0))
    qs_f32 = jnp.concatenate(parts, axis=0)
    qs = qs_f32.astype(BF16)
    t_row = i * tq + (_iota((rows, 1), 0) & (tq - 1))

    def unstack(x):
        out = jnp.where(lane_head == 0, x[0:tq], 0.0)
        for r in range(1, R):
            out = out + jnp.where(lane_head == r, x[r * tq:(r + 1) * tq], 0.0)
        return out

    def stack(x):
        return jnp.concatenate([x] * R, axis=0)

    tri_ok = _iota((tq, bk), 1) <= _iota((tq, bk), 0)
    diag_bias = stack(jnp.where(tri_ok, 0.0, NEG))

    far_bias = jnp.where(tri_ok, NEG, 0.0)
    half_rows = rows // 2
    win_off, win_bias = [], []
    for d in (2, 1, 0):
        jblk = i - d
        win_off.append(pl.multiple_of(jnp.maximum(jblk, 0) * bk, bk))
        missing = jnp.where(jblk >= 0, 0.0, NEG)
        bias = (far_bias + missing) if d == 2 else (jnp.where(tri_ok, 0.0, NEG) if d == 0 else None)
        win_bias.append(missing if bias is None else jnp.concatenate([bias] * (R // 2), axis=0))
    win_sc = [[_dot_nt(qs[h * half_rows:(h + 1) * half_rows], kw_a[pl.ds(off, bk), :]) + bias
               for off, bias in zip(win_off, win_bias)] for h in range(2)]
    v_cat = jnp.concatenate([vw_rep[pl.ds(off, bk), :] for off in win_off], axis=0)

    def win_half(h):
        scs = win_sc[h]
        m_h = jnp.max(jnp.maximum(jnp.maximum(scs[0], scs[1]), scs[2]), axis=1, keepdims=True)
        mb = jnp.broadcast_to(m_h, (half_rows, bk))
        pes = [jnp.exp(sc - mb) for sc in scs]
        l_h = jnp.sum(pes[0] + pes[1] + pes[2], axis=1, keepdims=True)
        p_cat = jnp.concatenate([pe.astype(BF16) for pe in pes], axis=1)
        return _dot(p_cat, v_cat) / l_h

    ncp = kc_a.shape[0]
    s = _dot_nt(qs, kc_a[...])
    win_0 = win_half(0)
    c_end = _iota((1, ncp), 1) * CMP_STRIDE + (CMP_BLOCK - 1)
    s = jnp.where(c_end <= t_row, s, NEG)
    e = jnp.exp(s - jnp.max(s, axis=1, keepdims=True))
    p = e / jnp.sum(e, axis=1, keepdims=True)
    p = jnp.where(t_row >= CMP_BLOCK - 1, p, 0.0)
    o_cmp = unstack(_dot(p.astype(BF16), vc_rep[...]))
    p_sum = p[0:tq]
    for r in range(1, R):
        p_sum = p_sum + p[r * tq:(r + 1) * tq]

    hi, lo = _split_bf16(p_sum)
    ov = ov_ref[...]
    imp = (_dot_nt(ov, hi) + _dot_nt(ov, lo))[0:nsb]
    win_1 = win_half(1)
    o_win = unstack(jnp.concatenate([win_0, win_1], axis=0))
    jb = _iota((nsb, 1), 0)
    cur = (i * tq + _iota((1, tq), 1)) >> 6
    forced = (jb == 0) | ((cur - jb >= 0) & (cur - jb < N_LOCAL_SEL))
    score = jnp.where(forced, FORCE, jnp.where(jb <= cur, imp, -FORCE))
    cnt = jnp.zeros((nsb, tq), F32)
    for i2 in range(nsb):
        row = score[i2:i2 + 1, :]
        beats = (row > score) | ((row == score) & (jb > i2))
        cnt = cnt + jnp.where(beats, 1.0, 0.0)
    sel_t = jnp.where((cnt < float(min(N_SEL, nsb))) & (jb <= cur), 0.0, -MASK_BIG)
    pad_t = jnp.concatenate([jnp.zeros((HEAD_DIM, tq), F32), sel_t,
                             jnp.zeros((LANES - HEAD_DIM - nsb, tq), F32)], axis=0)
    aug = jnp.concatenate([pad_t.T, jnp.zeros((tq, w - LANES), F32)], axis=1)
    qsel = (qs_f32 + stack(aug)).astype(BF16)

    n_blk = i + 1
    n_step = (n_blk + 1) // 2
    zero_bias = jnp.zeros((rows, bk), F32)
    neg_bias = jnp.full((rows, bk), NEG, F32)

    def blk_off(x):
        return pl.multiple_of(jnp.minimum(x, i) * bk, bk)

    mx_scr[...] = neg_bias

    def pass1(t, carry):
        mx = mx_scr[...]
        for x in (2 * t, 2 * t + 1):
            sc = _dot_nt(qsel, ks_a[pl.ds(blk_off(x), bk), :])
            sc = sc + jnp.where(x == i, diag_bias, jnp.where(x > i, neg_bias, zero_bias))
            s_scr[x] = sc
            mx = jnp.maximum(mx, sc)
        mx_scr[...] = mx
        return carry

    lax.fori_loop(0, n_step, pass1, 0)
    m = jnp.max(mx_scr[...], axis=1, keepdims=True)
    mx_scr[...] = jnp.broadcast_to(m, (rows, bk))
    l_scr[...] = jnp.zeros_like(l_scr)
    acc_scr[...] = jnp.zeros_like(acc_scr)

    def pass2(t, carry):
        mb2 = mx_scr[...]
        pe0 = jnp.exp(s_scr[2 * t] - mb2)
        pe1 = jnp.exp(s_scr[2 * t + 1] - mb2)
        l_scr[...] += pe0 + pe1
        p_cat = jnp.concatenate([pe0.astype(BF16), pe1.astype(BF16)], axis=1)
        v_cat = jnp.concatenate([vs_rep[pl.ds(blk_off(2 * t), bk), :],
                                 vs_rep[pl.ds(blk_off(2 * t + 1), bk), :]], axis=0)
        acc_scr[...] += _dot(p_cat, v_cat)
        return carry

    lax.fori_loop(0, n_step, pass2, 0)
    o_sel = unstack(acc_scr[...] / jnp.sum(l_scr[...], axis=1, keepdims=True))

    ghi, glo = _split_bf16(_sigmoid(gl_ref[...]))
    col = _iota((LANES, w), 0)
    out = jnp.zeros((tq, w), F32)
    for br, o_br in enumerate((o_cmp, o_sel, o_win)):
        target = br * (NSA_KV_HEADS * R) + g * R + (_iota((LANES, w), 1) >> 6)
        spread = jnp.where(col == target, 1.0, 0.0).astype(BF16)
        out = out + (_dot(ghi, spread) + _dot(glo, spread)) * o_br
    o_ref[...] = out.astype(o_ref.dtype)


def _nsa_kernel_t(q_ref, gl_ref, kc_ref, vc_ref, ks_ref, vs_ref, kw_ref, vw_ref,
                  bd_ref, qg_ref, c_ref, sa_ref, sb_ref, ov_ref,
                  o_ref,
                  kc_a, ks_a, kw_a, vct, vst, vwt, s_scr, acc_scr,
                  *, tq, bk, nsb, scale):
    g = pl.program_id(1)
    i = pl.program_id(2)
    w = GROUP_LANES
    R = NSA_GROUP
    rows = R * tq
    lane = _iota((1, w), 1)
    n_tiles = ks_a.shape[0] // bk

    @pl.when(i == 0)
    def _():
        rr, cc = _iota((w, w), 0), _iota((w, w), 1)
        mine = ((rr >> 6) == g) & ((rr & 63) == (cc & 63))
        to_slot0 = jnp.where(mine & (cc < HEAD_DIM), 1.0, 0.0).astype(BF16)
        kc_a[...] = _dot(kc_ref[...], to_slot0).astype(BF16)
        kw_a[...] = _dot(kw_ref[...], to_slot0).astype(BF16)
        seq = ks_a.shape[0]
        tag = jnp.where(_iota((seq, w), 1) - HEAD_DIM == (_iota((seq, w), 0) >> 6), 1.0, 0.0)
        ks_a[...] = (_dot(ks_ref[...], to_slot0) + tag).astype(BF16)
        pick_t = jnp.where(_iota((HEAD_DIM, w), 1) == g * HEAD_DIM + _iota((HEAD_DIM, w), 0),
                           1.0, 0.0).astype(BF16)
        vct[...] = _dot_nt(pick_t, vc_ref[...]).astype(BF16)
        pick_1 = jnp.where(_iota((V_ROWS, w), 1) == g * HEAD_DIM + _iota((V_ROWS, w), 0), 1.0, 0.0)
        pick_1 = jnp.where(_iota((V_ROWS, w), 0) < HEAD_DIM, pick_1, 0.0).astype(BF16)
        ones_row = jnp.where(_iota((V_ROWS, bk), 0) == HEAD_DIM, 1.0, 0.0)
        for j in range(n_tiles):
            vst[j] = (_dot_nt(pick_1, vs_ref[j * bk:(j + 1) * bk, :]) + ones_row).astype(BF16)
            vwt[j] = (_dot_nt(pick_1, vw_ref[j * bk:(j + 1) * bk, :]) + ones_row).astype(BF16)

    qn = _rope(_head_rmsnorm(q_ref[...], bd_ref[...], qg_ref[...]), c_ref[...], sa_ref[...], sb_ref[...])
    qn = qn * scale
    slot0 = lane < HEAD_DIM
    parts = []
    for r in range(R):
        qr = qn if r == 0 else pltpu.roll(qn, w - r * HEAD_DIM, 1)
        parts.append(jnp.where(slot0, qr, 0.0))
    qs_f32 = jnp.concatenate(parts, axis=0)
    qs = qs_f32.astype(BF16)
    t_col = i * tq + (_iota((1, rows), 1) & (tq - 1))

    def lanes4(x):
        return jnp.concatenate([x] * R, axis=1)

    def part(x, op):
        x = x.reshape(x.shape[0] // 8, 8, rows)
        return jnp.max(x, axis=0) if op == "max" else jnp.sum(x, axis=0)

    key_le_query = _iota((bk, tq), 0) <= _iota((bk, tq), 1)
    diag_bias = lanes4(jnp.where(key_le_query, 0.0, NEG))

    ncp = kc_a.shape[0]
    s = _dot_nt(kc_a[...], qs)

    far_bias = jnp.where(key_le_query, NEG, 0.0)

    def win_tile(d):
        jblk = i - d
        blk = jnp.maximum(jblk, 0)
        sc = _dot_nt(kw_a[pl.ds(pl.multiple_of(blk * bk, bk), bk), :], qs)
        missing = jnp.where(jblk >= 0, 0.0, NEG)
        if d == 2:
            sc = sc + lanes4(far_bias + missing)
        elif d == 1:
            sc = sc + missing
        else:
            sc = sc + diag_bias
        return sc, blk

    win = [win_tile(2), win_tile(1)]

    c_end = _iota((ncp, 1), 0) * CMP_STRIDE + (CMP_BLOCK - 1)
    s = jnp.where(c_end <= t_col, s, NEG)
    e = jnp.exp(s - jnp.max(s, axis=0, keepdims=True))
    p = e / jnp.sum(e, axis=0, keepdims=True)
    p = jnp.where(t_col >= CMP_BLOCK - 1, p, 0.0)
    o_cmp = _dot(vct[...], p.astype(BF16))
    p_sum = p[:, 0:tq]
    for r in range(1, R):
        p_sum = p_sum + p[:, r * tq:(r + 1) * tq]

    hi, lo = _split_bf16(p_sum)
    ov = ov_ref[...]
    imp = (_dot(ov, hi) + _dot(ov, lo))[0:nsb]
    win.append(win_tile(0))
    jb = _iota((nsb, 1), 0)
    cur = (i * tq + _iota((1, tq), 1)) >> 6
    forced = (jb == 0) | ((cur - jb >= 0) & (cur - jb < N_LOCAL_SEL))
    score = jnp.where(forced, FORCE, jnp.where(jb <= cur, imp, -FORCE))
    cnt = jnp.zeros((nsb, tq), F32)
    for i2 in range(nsb):
        row = score[i2:i2 + 1, :]
        beats = (row > score) | ((row == score) & (jb > i2))
        cnt = cnt + jnp.where(beats, 1.0, 0.0)
    sel_t = jnp.where((cnt < float(min(N_SEL, nsb))) & (jb <= cur), 0.0, -MASK_BIG)
    pad_t = jnp.concatenate([jnp.zeros((HEAD_DIM, tq), F32), sel_t,
                             jnp.zeros((LANES - HEAD_DIM - nsb, tq), F32)], axis=0)
    aug = jnp.concatenate([pad_t.T, jnp.zeros((tq, w - LANES), F32)], axis=1)
    qsel = (qs_f32 + jnp.concatenate([aug] * R, axis=0)).astype(BF16)

    win_sc = [sc for sc, _ in win]
    m_w = jnp.max(jnp.maximum(jnp.maximum(win_sc[0], win_sc[1]), win_sc[2]), axis=0, keepdims=True)
    pes = [jnp.exp(sc - m_w) for sc in win_sc]
    p_cat = jnp.concatenate([pe.astype(BF16) for pe in pes], axis=0)
    v_cat = jnp.concatenate([vwt[blk] for _, blk in win], axis=1)
    acc_w = _dot(v_cat, p_cat)
    o_win = acc_w[0:HEAD_DIM] / acc_w[HEAD_DIM:HEAD_DIM + 1]

    n_step = (i + 2) // 2

    def clamp(x):
        return jnp.minimum(x, i)

    def qk(x):
        return _dot_nt(ks_a[pl.ds(pl.multiple_of(clamp(x) * bk, bk), bk), :], qsel)

    def pass1(t, mx8):
        for x in (2 * t, 2 * t + 1):
            sc = qk(x)
            s_scr[x] = sc
            mx8 = jnp.maximum(mx8, part(sc, "max"))
        return mx8

    mx8 = lax.fori_loop(0, n_step - 1, pass1, jnp.full((8, rows), NEG, F32))
    for x in (2 * (n_step - 1), 2 * (n_step - 1) + 1):
        sc = qk(x) + jnp.where(x == i, diag_bias, jnp.where(x > i, NEG, 0.0))
        s_scr[x] = sc
        mx8 = jnp.maximum(mx8, part(sc, "max"))
    m_s = jnp.max(mx8, axis=0, keepdims=True)
    acc_scr[...] = jnp.zeros_like(acc_scr)

    def pass2(t, carry):
        p0 = jnp.exp(s_scr[2 * t] - m_s)
        p1 = jnp.exp(s_scr[2 * t + 1] - m_s)
        p2 = jnp.concatenate([p0.astype(BF16), p1.astype(BF16)], axis=0)
        v2 = jnp.concatenate([vst[clamp(2 * t)], vst[clamp(2 * t + 1)]], axis=1)
        acc_scr[...] += _dot(v2, p2)
        return carry

    lax.fori_loop(0, n_step, pass2, 0)
    o_sel = acc_scr[0:HEAD_DIM, :] / acc_scr[HEAD_DIM:HEAD_DIM + 1, :]

    def to_rows(o_t):
        return jnp.concatenate([o_t[:, r * tq:(r + 1) * tq] for r in range(R)], axis=0)

    ghi, glo = _split_bf16(_sigmoid(gl_ref[...]))
    row_head = _iota((w, LANES), 0) >> 6
    col = _iota((w, LANES), 1)
    out_t = jnp.zeros((w, tq), F32)
    for br, o_br in enumerate((o_cmp, o_sel, o_win)):
        spread = jnp.where(col == br * (NSA_KV_HEADS * R) + g * R + row_head, 1.0, 0.0).astype(BF16)
        out_t = out_t + (_dot_nt(spread, ghi) + _dot_nt(spread, glo)) * to_rows(o_br)
    o_ref[...] = out_t.T.astype(o_ref.dtype)


def _nsa_attention(proj, kc, vc, ksw, q_gain, coeffs, B, S):
    T = B * S
    w = GROUP_LANES
    G = NSA_KV_HEADS
    tq, bk = ATT_TQ, ATT_BK
    nq = S // tq
    nc = S // CMP_STRIDE
    nsb = S // SEL_BLOCK
    n_cmp = (S - CMP_BLOCK) // CMP_STRIDE + 1
    cs = np.arange(nc) * CMP_STRIDE
    js = np.arange(LANES) * SEL_BLOCK
    ov = ((cs[None, :] < js[:, None] + SEL_BLOCK) & (cs[None, :] + CMP_BLOCK > js[:, None])
          & (np.arange(nc)[None, :] < n_cmp) & (np.arange(LANES)[:, None] < nsb))
    ov = jnp.asarray(ov.astype(np.float32), dtype=BF16)
    gate_blk = (G * NSA_GROUP * HEAD_DIM) // LANES
    assert tq == bk and WINDOW == 2 * bk and HEAD_DIM + nsb <= LANES
    kern = functools.partial(_nsa_kernel_t, tq=tq, bk=bk, nsb=nsb, scale=HEAD_DIM ** -0.5)
    seq = lambda col: pl.BlockSpec((S, w), lambda b, g, i: (b, col))
    const = lambda shape: pl.BlockSpec(shape, lambda b, g, i: (0, 0))
    tab = pl.BlockSpec((tq, w), lambda b, g, i: (i, 0))
    rows = NSA_GROUP * tq
    n_tiles = S // bk
    return pl.pallas_call(
        kern,
        grid=(B, G, nq),
        in_specs=[
            pl.BlockSpec((tq, w), lambda b, g, i: (b * nq + i, g)),
            pl.BlockSpec((tq, LANES), lambda b, g, i: (b * nq + i, gate_blk)),
            pl.BlockSpec((nc, w), lambda b, g, i: (b, 0)),
            pl.BlockSpec((nc, w), lambda b, g, i: (b, 0)),
            seq(0), seq(1), seq(2), seq(3),
            const((w, w)), const((1, w)), tab, tab, tab, const((LANES, nc)),
        ],
        out_specs=pl.BlockSpec((tq, w), lambda b, g, i: (b * nq + i, g)),
        out_shape=jax.ShapeDtypeStruct((T, G * w), BF16),
        scratch_shapes=[
            pltpu.VMEM((nc, w), BF16), pltpu.VMEM((S, w), BF16), pltpu.VMEM((S, w), BF16),
            pltpu.VMEM((HEAD_DIM, nc), BF16),
            pltpu.VMEM((n_tiles, V_ROWS, bk), BF16), pltpu.VMEM((n_tiles, V_ROWS, bk), BF16),
            pltpu.VMEM((n_tiles, bk, rows), F32),
            pltpu.VMEM((V_ROWS, rows), F32),
        ],
        compiler_params=_params("parallel", "arbitrary", "arbitrary"),
        name="nsa_attn",
    )(proj, proj, kc, vc, ksw, ksw, ksw, ksw,
      _block_mean_matrix(w), jnp.tile(q_gain, NSA_GROUP).reshape(1, w), *coeffs, ov)


def kernel(x, ffn1_norm, ffn1_w_in, ffn1_w_out, mix_norm, ffn2_norm, ffn2_w_in, ffn2_w_out,
           sb_w_qkv, sb_w_out, kv_norm, nsa_w_kv, nsa_k_norm, cmp_pos_k, cmp_pos_v,
           cmp_k_w1, cmp_k_w2, cmp_v_w1, cmp_v_w2, nsa_w_in, nsa_q_norm, nsa_w_out):
    B, S, D = x.shape
    depth = ffn1_norm.shape[0]
    n_a = sb_w_qkv.shape[0]
    n_sb_heads = sb_w_out.shape[1] // HEAD_DIM
    assert S % 512 == 0 and S // CMP_STRIDE == LANES, "sequence tiling assumes 16*128 tokens"
    h = x.reshape(B * S, D)
    bf = lambda a: a.astype(BF16)
    coeffs = _rope_coeffs(jnp.arange(S), NSA_KV_HEADS)
    n_q = nsa_w_in.shape[-1]
    n_q_pad = -(-n_q // LANES) * LANES
    kc = vc = ksw = None
    w1_in, w1_out, w2_in, w2_out = bf(ffn1_w_in), bf(ffn1_w_out), bf(ffn2_w_in), bf(ffn2_w_out)
    w_qkv, w_sb_out, w_nsa_out = bf(sb_w_qkv), bf(sb_w_out), bf(nsa_w_out)
    w_nsa_in = jnp.pad(bf(nsa_w_in), ((0, 0), (0, 0), (0, n_q_pad - n_q)))
    w_kv = bf(nsa_w_kv)[None]
    for layer in range(depth):
        h = _ffn(h, ffn1_norm[layer], w1_in, w1_out, layer)
        if layer < n_a:
            qkv = _norm_proj(h, mix_norm[layer], w_qkv, layer, 1024, BF16)
            o = _sb_attention(qkv, B, S, n_sb_heads)
            h = _out_proj(h, o, w_sb_out, layer)
        else:
            li = layer - n_a
            proj = _norm_proj(h, mix_norm[layer], w_nsa_in, li, n_q_pad, F32)
            o = _nsa_attention(proj, kc, vc, ksw, nsa_q_norm[li], coeffs, B, S)
            h = _out_proj(h, o, w_nsa_out, li)
        h = _ffn(h, ffn2_norm[layer], w2_in, w2_out, layer)
        if layer == n_a - 1:
            kv = _norm_proj(h, kv_norm, w_kv, 0, 768, F32)
            kc, vc = _compress(kv, B, S, cmp_pos_k, cmp_pos_v, cmp_k_w1, cmp_k_w2,
                               cmp_v_w1, cmp_v_w2, nsa_k_norm[0])
            ksw = _kv_prep(kv, nsa_k_norm, coeffs, S)
    return h.reshape(B, S, D)
```

```python
import functools

import jax
import jax.numpy as jnp
import numpy as np
from jax import lax
from jax.experimental import pallas as pl
from jax.experimental.pallas import tpu as pltpu

F32 = jnp.float32
BF16 = jnp.bfloat16

HEAD_DIM = 64
NSA_KV_HEADS = 4
NSA_GROUP = 4
N_BRANCHES = 3
ROPE_DIMS = 16
ROPE_THETA = 500000.0
CMP_BLOCK = 32
CMP_STRIDE = 16
SEL_BLOCK = 64
N_SEL = 8
N_LOCAL_SEL = 2
WINDOW = 512
EPS = 1e-6
NEG = -1e30
FORCE = 1e4

LANES = 128
GROUP_LANES = NSA_GROUP * HEAD_DIM
VMEM_LIMIT_BYTES = 56 * 1024 * 1024

FFN_TM = 512
PROJ_TM = 1024
ATT_TQ = 256
ATT_BK = 256


def _dot(a, b):
    return jnp.dot(a, b, preferred_element_type=F32)


def _dot_nt(a, b):
    return lax.dot_general(a, b, (((1,), (1,)), ((), ())), preferred_element_type=F32)


def _split_bf16(x):
    hi = x.astype(BF16)
    lo = (x - hi.astype(F32)).astype(BF16)
    return hi, lo


def _dot_split(x, w):
    hi, lo = _split_bf16(x)
    n = x.shape[0]
    r = _dot(jnp.concatenate([hi, lo], axis=0), w)
    return r[0:n] + r[n:2 * n]


def _sigmoid(x):
    return 1.0 / (1.0 + jnp.exp(-x))


def _iota(shape, dim):
    return lax.broadcasted_iota(jnp.int32, shape, dim)


def _params(*sem):
    return pltpu.CompilerParams(dimension_semantics=sem, vmem_limit_bytes=VMEM_LIMIT_BYTES)


def _ffn_kernel(h_ref, g_ref, wi_ref, wo_ref, o_ref, *, tf, nf):
    x = h_ref[...]
    ms = jnp.mean(x * x, axis=-1, keepdims=True)
    xn = (x * lax.rsqrt(ms + EPS) * g_ref[...]).astype(BF16)
    ff = tf * nf
    ab = [(_dot(xn, wi_ref[:, c * tf:(c + 1) * tf]), _dot(xn, wi_ref[:, ff + c * tf:ff + (c + 1) * tf]))
          for c in range(nf)]
    acc = None
    for c, (a, b) in enumerate(ab):
        act = (a * _sigmoid(a) * b).astype(BF16)
        part = _dot(act, wo_ref[c * tf:(c + 1) * tf, :])
        acc = part if acc is None else acc + part
    o_ref[...] = x + 0.5 * acc


def _ffn(h, gain, w_in, w_out, layer):
    T, D = h.shape
    F = w_out.shape[1]
    tm = min(FFN_TM, T)
    tf = F // 2 if (F // 2) % LANES == 0 else F
    nf = F // tf
    resident = dict(pipeline_mode=pl.Buffered(1))
    return pl.pallas_call(
        functools.partial(_ffn_kernel, tf=tf, nf=nf),
        grid=(T // tm,),
        in_specs=[
            pl.BlockSpec((tm, D), lambda i: (i, 0)),
            pl.BlockSpec((1, D), lambda i: (0, 0)),
            pl.BlockSpec((None, D, 2 * F), lambda i: (layer, 0, 0), **resident),
            pl.BlockSpec((None, F, D), lambda i: (layer, 0, 0), **resident),
        ],
        out_specs=pl.BlockSpec((tm, D), lambda i: (i, 0)),
        out_shape=jax.ShapeDtypeStruct((T, D), F32),
        compiler_params=_params("parallel"),
        name="ffn",
    )(h, gain.reshape(1, D), w_in, w_out)


def _norm_proj_kernel(h_ref, g_ref, w_ref, o_ref, xn_ref):
    @pl.when(pl.program_id(1) == 0)
    def _():
        x = h_ref[...]
        ms = jnp.mean(x * x, axis=-1, keepdims=True)
        xn_ref[...] = (x * lax.rsqrt(ms + EPS) * g_ref[...]).astype(BF16)

    o_ref[...] = _dot(xn_ref[...], w_ref[...]).astype(o_ref.dtype)


def _norm_proj(h, gain, w, layer, tn, out_dtype):
    T, D = h.shape
    N = w.shape[2]
    tm = min(PROJ_TM, T)
    return pl.pallas_call(
        _norm_proj_kernel,
        grid=(T // tm, N // tn),
        in_specs=[
            pl.BlockSpec((tm, D), lambda i, j: (i, 0)),
            pl.BlockSpec((1, D), lambda i, j: (0, 0)),
            pl.BlockSpec((None, D, tn), lambda i, j: (layer, 0, j)),
        ],
        out_specs=pl.BlockSpec((tm, tn), lambda i, j: (i, j)),
        out_shape=jax.ShapeDtypeStruct((T, N), out_dtype),
        scratch_shapes=[pltpu.VMEM((tm, D), BF16)],
        compiler_params=_params("parallel", "arbitrary"),
        name="norm_proj",
    )(h, gain.reshape(1, D), w)


def _out_proj_kernel(h_ref, x_ref, w_ref, o_ref):
    o_ref[...] = h_ref[...] + _dot(x_ref[...], w_ref[...])


def _out_proj(h, x, w, layer):
    T, D = h.shape
    K = x.shape[1]
    tm = min(PROJ_TM, T)
    return pl.pallas_call(
        _out_proj_kernel,
        grid=(T // tm,),
        in_specs=[
            pl.BlockSpec((tm, D), lambda i: (i, 0)),
            pl.BlockSpec((tm, K), lambda i: (i, 0)),
            pl.BlockSpec((None, K, D), lambda i: (layer, 0, 0)),
        ],
        out_specs=pl.BlockSpec((tm, D), lambda i: (i, 0)),
        out_shape=jax.ShapeDtypeStruct((T, D), F32),
        compiler_params=_params("parallel"),
        name="out_proj",
    )(h, x, w)


def _sb_kernel(q_ref, k_ref, v_ref, o_ref, *, tq, bk, scale, q_tiles):
    i = pl.program_id(2)
    lane = _iota((1, LANES), 1)
    is_a = lane < HEAD_DIM
    q = q_ref[...] * scale
    zero = jnp.zeros_like(q)
    qh = (jnp.where(is_a, q, zero), jnp.where(is_a, zero, q))
    tri = jnp.where(_iota((bk, bk), 0) > _iota((bk, bk), 1), 1.0, 0.0).astype(BF16)

    def sweep_static(n_blocks):
        past = _iota((tq, bk), 1) < _iota((tq, bk), 0)
        chains = [(j, hd) for j in range(n_blocks - 1, -1, -1) for hd in range(2)]
        n = len(chains)
        z, log_b, log_1m, within, w = {}, {}, {}, {}, {}
        carry = [jnp.zeros((tq, 1), F32), jnp.zeros((tq, 1), F32)]
        acc = [jnp.zeros((tq, LANES), F32), jnp.zeros((tq, LANES), F32)]
        for k in range(n + 2):
            if k < n:
                j, hd = chains[k]
                z[k] = _dot_nt(qh[hd], k_ref[j * bk:(j + 1) * bk, :])
            if 0 <= k - 1 < n:
                c = k - 1
                j, hd = chains[c]
                log_b[c] = jnp.minimum(z[c], 0.0) - jnp.log(1.0 + jnp.exp(-jnp.abs(z[c])))
                m = log_b[c] - z.pop(c)
                log_1m[c] = jnp.where(past, m, 0.0) if j == n_blocks - 1 else m
                within[c] = _dot(log_1m[c].astype(BF16), tri)
            if 0 <= k - 2 < n:
                c = k - 2
                j, hd = chains[c]
                wc = jnp.exp(log_b.pop(c) + (carry[hd] + within.pop(c)))
                if j == n_blocks - 1:
                    wc = jnp.where(past, wc, 0.0)
                carry[hd] = carry[hd] + jnp.sum(log_1m.pop(c), axis=1, keepdims=True)
                acc[hd] = acc[hd] + _dot(wc.astype(BF16), v_ref[j * bk:(j + 1) * bk, :])
        return acc

    for n_blocks in range(1, q_tiles + 1):
        @pl.when(i == n_blocks - 1)
        def _(n_blocks=n_blocks):
            acc_a, acc_b = sweep_static(n_blocks)
            o_ref[...] = jnp.where(is_a, acc_a, acc_b).astype(o_ref.dtype)


def _sb_attention(qkv, B, S, n_heads):
    T = B * S
    tq, bk = ATT_TQ, ATT_BK
    nq = S // tq
    n_pairs = n_heads * HEAD_DIM // LANES
    kern = functools.partial(_sb_kernel, tq=tq, bk=bk, scale=HEAD_DIM ** -0.5, q_tiles=nq)
    return pl.pallas_call(
        kern,
        grid=(B, n_pairs, nq),
        in_specs=[
            pl.BlockSpec((tq, LANES), lambda b, p, i: (b * nq + i, p)),
            pl.BlockSpec((S, LANES), lambda b, p, i: (b, n_pairs + p)),
            pl.BlockSpec((S, LANES), lambda b, p, i: (b, 2 * n_pairs + p)),
        ],
        out_specs=pl.BlockSpec((tq, LANES), lambda b, p, i: (b * nq + i, p)),
        out_shape=jax.ShapeDtypeStruct((T, n_heads * HEAD_DIM), BF16),
        compiler_params=_params("parallel", "parallel", "arbitrary"),
        name="sb_attn",
    )(qkv, qkv, qkv)


def _head_rmsnorm(x, bd, gain):
    ms = _dot_split(x * x, bd)
    return x * lax.rsqrt(ms + EPS) * gain


def _rope(y, c, sa, sb):
    w = y.shape[-1]
    half = ROPE_DIMS // 2
    return y * c + pltpu.roll(y, w - half, 1) * sa + pltpu.roll(y, half, 1) * sb


def _rope_coeffs(positions, n_heads):
    inv_freq = jnp.power(ROPE_THETA, -jnp.arange(0, ROPE_DIMS, 2, dtype=F32) / ROPE_DIMS)
    ang = positions.astype(F32)[:, None] * inv_freq[None, :]
    cos, sin = jnp.cos(ang), jnp.sin(ang)
    p = positions.shape[0]
    rest = HEAD_DIM - ROPE_DIMS
    zeros_h = jnp.zeros((p, ROPE_DIMS // 2), F32)
    c = jnp.concatenate([cos, cos, jnp.ones((p, rest), F32)], axis=-1)
    sa = jnp.concatenate([-sin, zeros_h, jnp.zeros((p, rest), F32)], axis=-1)
    sb = jnp.concatenate([zeros_h, sin, jnp.zeros((p, rest), F32)], axis=-1)
    return tuple(jnp.tile(t, (1, n_heads)) for t in (c, sa, sb))


def _block_mean_matrix(width):
    idx = np.arange(width) // HEAD_DIM
    return jnp.asarray((idx[:, None] == idx[None, :]).astype(np.float32) / HEAD_DIM, dtype=BF16)


def _kv_prep_kernel(sel_ref, win_ref, bd_ref, gs_ref, gw_ref, c_ref, sa_ref, sb_ref, o_ref):
    w = GROUP_LANES
    bd = bd_ref[...]
    c, sa, sb = c_ref[...], sa_ref[...], sb_ref[...]
    ks = _rope(_head_rmsnorm(sel_ref[:, :w], bd, gs_ref[...]), c, sa, sb)
    kw = _rope(_head_rmsnorm(win_ref[:, :w], bd, gw_ref[...]), c, sa, sb)
    o_ref[:, 0 * w:1 * w] = ks.astype(BF16)
    o_ref[:, 1 * w:2 * w] = sel_ref[:, w:].astype(BF16)
    o_ref[:, 2 * w:3 * w] = kw.astype(BF16)
    o_ref[:, 3 * w:4 * w] = win_ref[:, w:].astype(BF16)


def _kv_prep(kv, k_norm, coeffs, S):
    T = kv.shape[0]
    w = GROUP_LANES
    tm = min(PROJ_TM, S)
    ns = S // tm
    gs = jnp.tile(k_norm[1], NSA_KV_HEADS).reshape(1, w)
    gw = jnp.tile(k_norm[2], NSA_KV_HEADS).reshape(1, w)
    tab = pl.BlockSpec((tm, w), lambda i: (i % ns, 0))
    one = pl.BlockSpec((1, w), lambda i: (0, 0))
    return pl.pallas_call(
        _kv_prep_kernel,
        grid=(T // tm,),
        in_specs=[
            pl.BlockSpec((tm, 2 * w), lambda i: (i, 1)),
            pl.BlockSpec((tm, 2 * w), lambda i: (i, 2)),
            pl.BlockSpec((w, w), lambda i: (0, 0)),
            one, one, tab, tab, tab,
        ],
        out_specs=pl.BlockSpec((tm, 4 * w), lambda i: (i, 0)),
        out_shape=jax.ShapeDtypeStruct((T, 4 * w), BF16),
        compiler_params=_params("parallel"),
        name="kv_prep",
    )(kv, kv, _block_mean_matrix(w), gs, gw, *coeffs)


def _compress_kernel(x_ref, pos_ref, w1_ref, w2_ref, bd_ref, g_ref, c_ref, sa_ref, sb_ref,
                     o_ref, top_ref, bot_ref):
    t = pl.program_id(0)
    l = pl.program_id(1)
    half = CMP_BLOCK // 2

    @pl.when(l == 0)
    def _():
        top_ref[...] = jnp.zeros_like(top_ref)
        bot_ref[...] = jnp.zeros_like(bot_ref)

    x = x_ref[0]
    top_ref[...] += _dot((x + pos_ref[0, 0]).astype(BF16), w1_ref[0, 0])
    bot_ref[...] += _dot((x + pos_ref[0, 1]).astype(BF16), w1_ref[0, 1])

    @pl.when(l == half - 1)
    def _():
        rows = bot_ref.shape[0]
        hid = top_ref[...] + pltpu.roll(bot_ref[...], rows - 1, 0)
        act = (hid * _sigmoid(hid)).astype(BF16)
        y = _dot(act, w2_ref[0])
        yk = _rope(_head_rmsnorm(y, bd_ref[...], g_ref[...]), c_ref[...], sa_ref[...], sb_ref[...])
        o_ref[0] = jnp.where(t == 0, yk, y).astype(BF16)


def _compress(kv, B, S, pos_k, pos_v, k_w1, k_w2, v_w1, v_w2, k_gain):
    w = GROUP_LANES
    G = NSA_KV_HEADS
    half = CMP_BLOCK // 2
    nc = S // CMP_STRIDE
    rows = B * nc
    hid = k_w1.shape[1]
    raw = kv[:, :2 * w].reshape(B, nc, half, 2, w)
    xc = jnp.transpose(raw, (3, 2, 0, 1, 4)).reshape(2, half, rows, w)
    eye = jnp.eye(G, dtype=F32)

    def blockdiag(wm):
        out = jnp.einsum('gh,...ab->...gahb', eye, wm)
        return out.reshape(*wm.shape[:-2], G * wm.shape[-2], G * wm.shape[-1])

    def prep_w1(w1):
        w1 = w1.reshape(2, half, HEAD_DIM, hid)
        return jnp.transpose(blockdiag(w1), (1, 0, 2, 3))

    w1 = jnp.stack([prep_w1(k_w1), prep_w1(v_w1)]).astype(BF16)
    w2 = jnp.stack([blockdiag(k_w2), blockdiag(v_w2)]).astype(BF16)

    def prep_pos(p):
        p = jnp.tile(p, (1, G)).reshape(2, half, 1, w)
        return jnp.transpose(p, (1, 0, 2, 3))

    pos = jnp.stack([prep_pos(pos_k), prep_pos(pos_v)])
    c_end = jnp.arange(nc) * CMP_STRIDE + (CMP_BLOCK - 1)
    coeffs = tuple(jnp.tile(tb, (B, 1)) for tb in _rope_coeffs(c_end, G))
    gain = jnp.tile(k_gain, G).reshape(1, w)
    const2 = lambda t, l: (0, 0)
    out = pl.pallas_call(
        _compress_kernel,
        grid=(2, half),
        in_specs=[
            pl.BlockSpec((None, 1, rows, w), lambda t, l: (t, l, 0, 0)),
            pl.BlockSpec((None, 1, 2, 1, w), lambda t, l: (t, l, 0, 0, 0)),
            pl.BlockSpec((None, 1, 2, w, G * hid), lambda t, l: (t, l, 0, 0, 0)),
            pl.BlockSpec((1, G * hid, w), lambda t, l: (t, 0, 0)),
            pl.BlockSpec((w, w), const2),
            pl.BlockSpec((1, w), const2),
            pl.BlockSpec((rows, w), const2),
            pl.BlockSpec((rows, w), const2),
            pl.BlockSpec((rows, w), const2),
        ],
        out_specs=pl.BlockSpec((1, rows, w), lambda t, l: (t, 0, 0)),
        out_shape=jax.ShapeDtypeStruct((2, rows, w), BF16),
        scratch_shapes=[pltpu.VMEM((rows, G * hid), F32), pltpu.VMEM((rows, G * hid), F32)],
        compiler_params=_params("arbitrary", "arbitrary"),
        name="nsa_compress",
    )(xc, pos, w1, w2, _block_mean_matrix(w), gain, *coeffs)
    return out[0], out[1]


def _nsa_kernel_old(q_ref, gl_ref, kc_ref, vc_ref, ks_ref, vs_ref, kw_ref, vw_ref,
                bd_ref, qg_ref, c_ref, sa_ref, sb_ref, ov_ref,
                o_ref,
                kc_rep, vc_rep, ks_rep, vs_rep, kw_rep, vw_rep, m_ref, l_ref, acc_ref,
                *, tq, bk, nsb, scale):
    g = pl.program_id(1)
    i = pl.program_id(2)
    w = GROUP_LANES
    R = NSA_GROUP
    rows = R * tq
    lane_head = _iota((1, w), 1) >> 6

    @pl.when(i == 0)
    def _():
        rr, cc = _iota((w, w), 0), _iota((w, w), 1)
        pick = jnp.where(((rr >> 6) == g) & ((rr & 63) == (cc & 63)), 1.0, 0.0).astype(BF16)
        for src, dst in ((kc_ref, kc_rep), (vc_ref, vc_rep), (ks_ref, ks_rep),
                         (vs_ref, vs_rep), (kw_ref, kw_rep), (vw_ref, vw_rep)):
            dst[...] = _dot(src[...], pick).astype(BF16)

    qn = _rope(_head_rmsnorm(q_ref[...], bd_ref[...], qg_ref[...]), c_ref[...], sa_ref[...], sb_ref[...])
    qn = qn.astype(BF16)
    qm = jnp.concatenate(
        [jnp.where(lane_head == r, qn, jnp.zeros_like(qn)) for r in range(R)], axis=0)
    t_row = i * tq + (_iota((rows, 1), 0) & (tq - 1))

    def unstack(x):
        out = jnp.where(lane_head == 0, x[0:tq], 0.0)
        for r in range(1, R):
            out = out + jnp.where(lane_head == r, x[r * tq:(r + 1) * tq], 0.0)
        return out

    ncp = kc_rep.shape[0]
    s = _dot_nt(qm, kc_rep[...]) * scale
    c_end = _iota((1, ncp), 1) * CMP_STRIDE + (CMP_BLOCK - 1)
    valid = c_end <= t_row
    s = jnp.where(valid, s, NEG)
    e = jnp.exp(s - jnp.max(s, axis=1, keepdims=True))
    p = e / jnp.sum(e, axis=1, keepdims=True)
    p = jnp.where(t_row >= CMP_BLOCK - 1, p, 0.0)
    o_cmp = unstack(_dot(p.astype(BF16), vc_rep[...]))
    p_sum = p[0:tq]
    for r in range(1, R):
        p_sum = p_sum + p[r * tq:(r + 1) * tq]

    hi, lo = _split_bf16(p_sum)
    ov = ov_ref[...]
    imp = (_dot_nt(ov, hi) + _dot_nt(ov, lo))[0:nsb]
    jb = _iota((nsb, 1), 0)
    cur = (i * tq + _iota((1, tq), 1)) >> 6
    forced = (jb == 0) | ((cur - jb >= 0) & (cur - jb < N_LOCAL_SEL))
    score = jnp.where(forced, FORCE, jnp.where(jb <= cur, imp, -FORCE))
    cnt = jnp.zeros((nsb, tq), F32)
    for i2 in range(nsb):
        row = score[i2:i2 + 1, :]
        beats = (row > score) | ((row == score) & (jb > i2))
        cnt = cnt + jnp.where(beats, 1.0, 0.0)
    sel_t = jnp.where(cnt < float(min(N_SEL, nsb)), 1.0, 0.0)
    if nsb < LANES:
        sel_t = jnp.concatenate([sel_t, jnp.zeros((LANES - nsb, tq), F32)], axis=0)
    sel = sel_t.T.astype(BF16)

    def flash_block(k_rep, v_rep, jblk, ok_fn):
        off = pl.multiple_of(jblk * bk, bk)
        kb = k_rep[pl.ds(off, bk), :]
        vb = v_rep[pl.ds(off, bk), :]
        s_pos = jblk * bk + _iota((1, bk), 1)
        ok = ok_fn(s_pos, jblk)
        sc = jnp.where(ok, _dot_nt(qm, kb) * scale, NEG)
        m_old = m_ref[...]
        m_new = jnp.maximum(m_old, jnp.max(sc, axis=1, keepdims=True))
        alpha = jnp.exp(m_old - m_new)
        pe = jnp.where(ok, jnp.exp(sc - m_new), 0.0)
        l_ref[...] = alpha * l_ref[...] + jnp.sum(pe, axis=1, keepdims=True)
        acc_ref[...] = alpha * acc_ref[...] + _dot(pe.astype(BF16), vb)
        m_ref[...] = m_new

    def reset():
        m_ref[...] = jnp.full_like(m_ref, NEG)
        l_ref[...] = jnp.zeros_like(l_ref)
        acc_ref[...] = jnp.zeros_like(acc_ref)

    def sel_ok(s_pos, jblk):
        blk = (jblk * bk + _iota((LANES, bk), 1)) >> 6
        expand = jnp.where(_iota((LANES, bk), 0) == blk, 1.0, 0.0).astype(BF16)
        md = _dot(sel, expand)
        md = jnp.concatenate([md] * R, axis=0)
        return (md > 0.5) & (s_pos <= t_row)

    reset()

    def sel_body(jblk, carry):
        flash_block(ks_rep, vs_rep, jblk, sel_ok)
        return carry

    lax.fori_loop(0, (i * tq + tq) // bk, sel_body, 0)
    o_sel = unstack(acc_ref[...] / l_ref[...])

    def win_ok(s_pos, jblk):
        return (s_pos <= t_row) & (t_row - s_pos < WINDOW)

    reset()

    def win_body(jblk, carry):
        flash_block(kw_rep, vw_rep, jblk, win_ok)
        return carry

    lax.fori_loop(jnp.maximum(i * tq - (WINDOW - 1), 0) // bk, (i * tq + tq) // bk, win_body, 0)
    o_win = unstack(acc_ref[...] / l_ref[...])

    ghi, glo = _split_bf16(_sigmoid(gl_ref[...]))
    col = _iota((LANES, w), 0)
    out = jnp.zeros((tq, w), F32)
    for br, o_br in enumerate((o_cmp, o_sel, o_win)):
        target = br * (NSA_KV_HEADS * R) + g * R + (_iota((LANES, w), 1) >> 6)
        spread = jnp.where(col == target, 1.0, 0.0).astype(BF16)
        out = out + (_dot(ghi, spread) + _dot(glo, spread)) * o_br
    o_ref[...] = out.astype(o_ref.dtype)


MASK_BIG = 2.0 ** 30
V_ROWS = HEAD_DIM + 16


def _nsa_kernel(q_ref, gl_ref, kc_ref, vc_ref, ks_ref, vs_ref, kw_ref, vw_ref,
                bd_ref, qg_ref, c_ref, sa_ref, sb_ref, ov_ref,
                o_ref,
                kc_a, vc_rep, ks_a, vs_rep, kw_a, vw_rep, s_scr, mx_scr, l_scr, acc_scr,
                *, tq, bk, nsb, scale):
    g = pl.program_id(1)
    i = pl.program_id(2)
    w = GROUP_LANES
    R = NSA_GROUP
    rows = R * tq
    lane = _iota((1, w), 1)
    lane_head = lane >> 6

    @pl.when(i == 0)
    def _():
        rr, cc = _iota((w, w), 0), _iota((w, w), 1)
        mine = ((rr >> 6) == g) & ((rr & 63) == (cc & 63))
        to_slot0 = jnp.where(mine & (cc < HEAD_DIM), 1.0, 0.0).astype(BF16)
        to_all = jnp.where(mine, 1.0, 0.0).astype(BF16)
        kc_a[...] = _dot(kc_ref[...], to_slot0).astype(BF16)
        kw_a[...] = _dot(kw_ref[...], to_slot0).astype(BF16)
        seq = ks_a.shape[0]
        tag = jnp.where(_iota((seq, w), 1) - HEAD_DIM == (_iota((seq, w), 0) >> 6), 1.0, 0.0)
        ks_a[...] = (_dot(ks_ref[...], to_slot0) + tag).astype(BF16)
        for src, dst in ((vc_ref, vc_rep), (vs_ref, vs_rep), (vw_ref, vw_rep)):
            dst[...] = _dot(src[...], to_all).astype(BF16)

    qn = _rope(_head_rmsnorm(q_ref[...], bd_ref[...], qg_ref[...]), c_ref[...], sa_ref[...], sb_ref[...])
    qn = qn * scale
    slot0 = lane < HEAD_DIM
    parts = []
    for r in range(R):
        qr = qn if r == 0 else pltpu.roll(qn, w - r * HEAD_DIM, 1)
        parts.append(jnp.where(slot0, qr, 0.0))
    qs_f32 = jnp.concatenate(parts, axis=0)
    qs = qs_f32.astype(BF16)
    t_row = i * tq + (_iota((rows, 1), 0) & (tq - 1))

    def unstack(x):
        out = jnp.where(lane_head == 0, x[0:tq], 0.0)
        for r in range(1, R):
            out = out + jnp.where(lane_head == r, x[r * tq:(r + 1) * tq], 0.0)
        return out

    def stack(x):
        return jnp.concatenate([x] * R, axis=0)

    tri_ok = _iota((tq, bk), 1) <= _iota((tq, bk), 0)
    diag_bias = stack(jnp.where(tri_ok, 0.0, NEG))

    far_bias = jnp.where(tri_ok, NEG, 0.0)
    half_rows = rows // 2
    win_off, win_bias = [], []
    for d in (2, 1, 0):
        jblk = i - d
        win_off.append(pl.multiple_of(jnp.maximum(jblk, 0) * bk, bk))
        missing = jnp.where(jblk >= 0, 0.0, NEG)
        bias = (far_bias + missing) if d == 2 else (jnp.where(tri_ok, 0.0, NEG) if d == 0 else None)
        win_bias.append(missing if bias is None else jnp.concatenate([bias] * (R // 2), axis=0))
    win_sc = [[_dot_nt(qs[h * half_rows:(h + 1) * half_rows], kw_a[pl.ds(off, bk), :]) + bias
               for off, bias in zip(win_off, win_bias)] for h in range(2)]
    v_cat = jnp.concatenate([vw_rep[pl.ds(off, bk), :] for off in win_off], axis=0)

    def win_half(h):
        scs = win_sc[h]
        m_h = jnp.max(jnp.maximum(jnp.maximum(scs[0], scs[1]), scs[2]), axis=1, keepdims=True)
        mb = jnp.broadcast_to(m_h, (half_rows, bk))
        pes = [jnp.exp(sc - mb) for sc in scs]
        l_h = jnp.sum(pes[0] + pes[1] + pes[2], axis=1, keepdims=True)
        p_cat = jnp.concatenate([pe.astype(BF16) for pe in pes], axis=1)
        return _dot(p_cat, v_cat) / l_h

    ncp = kc_a.shape[0]
    s = _dot_nt(qs, kc_a[...])
    win_0 = win_half(0)
    c_end = _iota((1, ncp), 1) * CMP_STRIDE + (CMP_BLOCK - 1)
    s = jnp.where(c_end <= t_row, s, NEG)
    e = jnp.exp(s - jnp.max(s, axis=1, keepdims=True))
    p = e / jnp.sum(e, axis=1, keepdims=True)
    p = jnp.where(t_row >= CMP_BLOCK - 1, p, 0.0)
    o_cmp = unstack(_dot(p.astype(BF16), vc_rep[...]))
    p_sum = p[0:tq]
    for r in range(1, R):
        p_sum = p_sum + p[r * tq:(r + 1) * tq]

    hi, lo = _split_bf16(p_sum)
    ov = ov_ref[...]
    imp = (_dot_nt(ov, hi) + _dot_nt(ov, lo))[0:nsb]
    win_1 = win_half(1)
    o_win = unstack(jnp.concatenate([win_0, win_1], axis=0))
    jb = _iota((nsb, 1), 0)
    cur = (i * tq + _iota((1, tq), 1)) >> 6
    forced = (jb == 0) | ((cur - jb >= 0) & (cur - jb < N_LOCAL_SEL))
    score = jnp.where(forced, FORCE, jnp.where(jb <= cur, imp, -FORCE))
    cnt = jnp.zeros((nsb, tq), F32)
    for i2 in range(nsb):
        row = score[i2:i2 + 1, :]
        beats = (row > score) | ((row == score) & (jb > i2))
        cnt = cnt + jnp.where(beats, 1.0, 0.0)
    sel_t = jnp.where((cnt < float(min(N_SEL, nsb))) & (jb <= cur), 0.0, -MASK_BIG)
    pad_t = jnp.concatenate([jnp.zeros((HEAD_DIM, tq), F32), sel_t,
                             jnp.zeros((LANES - HEAD_DIM - nsb, tq), F32)], axis=0)
    aug = jnp.concatenate([pad_t.T, jnp.zeros((tq, w - LANES), F32)], axis=1)
    qsel = (qs_f32 + stack(aug)).astype(BF16)

    n_blk = i + 1
    n_step = (n_blk + 1) // 2
    zero_bias = jnp.zeros((rows, bk), F32)
    neg_bias = jnp.full((rows, bk), NEG, F32)

    def blk_off(x):
        return pl.multiple_of(jnp.minimum(x, i) * bk, bk)

    mx_scr[...] = neg_bias

    def pass1(t, carry):
        mx = mx_scr[...]
        for x in (2 * t, 2 * t + 1):
            sc = _dot_nt(qsel, ks_a[pl.ds(blk_off(x), bk), :])
            sc = sc + jnp.where(x == i, diag_bias, jnp.where(x > i, neg_bias, zero_bias))
            s_scr[x] = sc
            mx = jnp.maximum(mx, sc)
        mx_scr[...] = mx
        return carry

    lax.fori_loop(0, n_step, pass1, 0)
    m = jnp.max(mx_scr[...], axis=1, keepdims=True)
    mx_scr[...] = jnp.broadcast_to(m, (rows, bk))
    l_scr[...] = jnp.zeros_like(l_scr)
    acc_scr[...] = jnp.zeros_like(acc_scr)

    def pass2(t, carry):
        mb2 = mx_scr[...]
        pe0 = jnp.exp(s_scr[2 * t] - mb2)
        pe1 = jnp.exp(s_scr[2 * t + 1] - mb2)
        l_scr[...] += pe0 + pe1
        p_cat = jnp.concatenate([pe0.astype(BF16), pe1.astype(BF16)], axis=1)
        v_cat = jnp.concatenate([vs_rep[pl.ds(blk_off(2 * t), bk), :],
                                 vs_rep[pl.ds(blk_off(2 * t + 1), bk), :]], axis=0)
        acc_scr[...] += _dot(p_cat, v_cat)
        return carry

    lax.fori_loop(0, n_step, pass2, 0)
    o_sel = unstack(acc_scr[...] / jnp.sum(l_scr[...], axis=1, keepdims=True))

    ghi, glo = _split_bf16(_sigmoid(gl_ref[...]))
    col = _iota((LANES, w), 0)
    out = jnp.zeros((tq, w), F32)
    for br, o_br in enumerate((o_cmp, o_sel, o_win)):
        target = br * (NSA_KV_HEADS * R) + g * R + (_iota((LANES, w), 1) >> 6)
        spread = jnp.where(col == target, 1.0, 0.0).astype(BF16)
        out = out + (_dot(ghi, spread) + _dot(glo, spread)) * o_br
    o_ref[...] = out.astype(o_ref.dtype)


def _nsa_kernel_t(q_ref, gl_ref, kc_ref, vc_ref, ks_ref, vs_ref, kw_ref, vw_ref,
                  bd_ref, qg_ref, c_ref, sa_ref, sb_ref, ov_ref,
                  o_ref,
                  kc_a, ks_a, kw_a, vct, vst, vwt, s_scr, acc_scr,
                  *, tq, bk, nsb, scale):
    g = pl.program_id(1)
    i = pl.program_id(2)
    w = GROUP_LANES
    R = NSA_GROUP
    rows = R * tq
    lane = _iota((1, w), 1)
    n_tiles = ks_a.shape[0] // bk

    @pl.when(i == 0)
    def _():
        rr, cc = _iota((w, w), 0), _iota((w, w), 1)
        mine = ((rr >> 6) == g) & ((rr & 63) == (cc & 63))
        to_slot0 = jnp.where(mine & (cc < HEAD_DIM), 1.0, 0.0).astype(BF16)
        kc_a[...] = _dot(kc_ref[...], to_slot0).astype(BF16)
        kw_a[...] = _dot(kw_ref[...], to_slot0).astype(BF16)
        seq = ks_a.shape[0]
        tag = jnp.where(_iota((seq, w), 1) - HEAD_DIM == (_iota((seq, w), 0) >> 6), 1.0, 0.0)
        ks_a[...] = (_dot(ks_ref[...], to_slot0) + tag).astype(BF16)
        pick_t = jnp.where(_iota((HEAD_DIM, w), 1) == g * HEAD_DIM + _iota((HEAD_DIM, w), 0),
                           1.0, 0.0).astype(BF16)
        vct[...] = _dot_nt(pick_t, vc_ref[...]).astype(BF16)
        pick_1 = jnp.where(_iota((V_ROWS, w), 1) == g * HEAD_DIM + _iota((V_ROWS, w), 0), 1.0, 0.0)
        pick_1 = jnp.where(_iota((V_ROWS, w), 0) < HEAD_DIM, pick_1, 0.0).astype(BF16)
        ones_row = jnp.where(_iota((V_ROWS, bk), 0) == HEAD_DIM, 1.0, 0.0)
        for j in range(n_tiles):
            vst[j] = (_dot_nt(pick_1, vs_ref[j * bk:(j + 1) * bk, :]) + ones_row).astype(BF16)
            vwt[j] = (_dot_nt(pick_1, vw_ref[j * bk:(j + 1) * bk, :]) + ones_row).astype(BF16)

    qn = _rope(_head_rmsnorm(q_ref[...], bd_ref[...], qg_ref[...]), c_ref[...], sa_ref[...], sb_ref[...])
    qn = qn * scale
    slot0 = lane < HEAD_DIM
    parts = []
    for r in range(R):
        qr = qn if r == 0 else pltpu.roll(qn, w - r * HEAD_DIM, 1)
        parts.append(jnp.where(slot0, qr, 0.0))
    qs_f32 = jnp.concatenate(parts, axis=0)
    qs = qs_f32.astype(BF16)
    t_col = i * tq + (_iota((1, rows), 1) & (tq - 1))

    def lanes4(x):
        return jnp.concatenate([x] * R, axis=1)

    def part(x, op):
        x = x.reshape(x.shape[0] // 8, 8, rows)
        return jnp.max(x, axis=0) if op == "max" else jnp.sum(x, axis=0)

    key_le_query = _iota((bk, tq), 0) <= _iota((bk, tq), 1)
    diag_bias = lanes4(jnp.where(key_le_query, 0.0, NEG))

    ncp = kc_a.shape[0]
    s = _dot_nt(kc_a[...], qs)

    far_bias = jnp.where(key_le_query, NEG, 0.0)

    def win_tile(d):
        jblk = i - d
        blk = jnp.maximum(jblk, 0)
        sc = _dot_nt(kw_a[pl.ds(pl.multiple_of(blk * bk, bk), bk), :], qs)
        missing = jnp.where(jblk >= 0, 0.0, NEG)
        if d == 2:
            sc = sc + lanes4(far_bias + missing)
        elif d == 1:
            sc = sc + missing
        else:
            sc = sc + diag_bias
        return sc, blk

    win = [win_tile(2), win_tile(1)]

    c_end = _iota((ncp, 1), 0) * CMP_STRIDE + (CMP_BLOCK - 1)
    s = jnp.where(c_end <= t_col, s, NEG)
    e = jnp.exp(s - jnp.max(s, axis=0, keepdims=True))
    p = e / jnp.sum(e, axis=0, keepdims=True)
    p = jnp.where(t_col >= CMP_BLOCK - 1, p, 0.0)
    o_cmp = _dot(vct[...], p.astype(BF16))
    p_sum = p[:, 0:tq]
    for r in range(1, R):
        p_sum = p_sum + p[:, r * tq:(r + 1) * tq]

    hi, lo = _split_bf16(p_sum)
    ov = ov_ref[...]
    imp = (_dot(ov, hi) + _dot(ov, lo))[0:nsb]
    win.append(win_tile(0))
    jb = _iota((nsb, 1), 0)
    cur = (i * tq + _iota((1, tq), 1)) >> 6
    forced = (jb == 0) | ((cur - jb >= 0) & (cur - jb < N_LOCAL_SEL))
    score = jnp.where(forced, FORCE, jnp.where(jb <= cur, imp, -FORCE))
    cnt = jnp.zeros((nsb, tq), F32)
    for i2 in range(nsb):
        row = score[i2:i2 + 1, :]
        beats = (row > score) | ((row == score) & (jb > i2))
        cnt = cnt + jnp.where(beats, 1.0, 0.0)
    sel_t = jnp.where((cnt < float(min(N_SEL, nsb))) & (jb <= cur), 0.0, -MASK_BIG)
    pad_t = jnp.concatenate([jnp.zeros((HEAD_DIM, tq), F32), sel_t,
                             jnp.zeros((LANES - HEAD_DIM - nsb, tq), F32)], axis=0)
    aug = jnp.concatenate([pad_t.T, jnp.zeros((tq, w - LANES), F32)], axis=1)
    qsel = (qs_f32 + jnp.concatenate([aug] * R, axis=0)).astype(BF16)

    win_sc = [sc for sc, _ in win]
    m_w = jnp.max(jnp.maximum(jnp.maximum(win_sc[0], win_sc[1]), win_sc[2]), axis=0, keepdims=True)
    pes = [jnp.exp(sc - m_w) for sc in win_sc]
    p_cat = jnp.concatenate([pe.astype(BF16) for pe in pes], axis=0)
    v_cat = jnp.concatenate([vwt[blk] for _, blk in win], axis=1)
    acc_w = _dot(v_cat, p_cat)
    o_win = acc_w[0:HEAD_DIM] / acc_w[HEAD_DIM:HEAD_DIM + 1]

    n_step = (i + 2) // 2

    def clamp(x):
        return jnp.minimum(x, i)

    def qk(x):
        return _dot_nt(ks_a[pl.ds(pl.multiple_of(clamp(x) * bk, bk), bk), :], qsel)

    def pass1(t, mx8):
        for x in (2 * t, 2 * t + 1):
            sc = qk(x)
            s_scr[x] = sc
            mx8 = jnp.maximum(mx8, part(sc, "max"))
        return mx8

    mx8 = lax.fori_loop(0, n_step - 1, pass1, jnp.full((8, rows), NEG, F32))
    for x in (2 * (n_step - 1), 2 * (n_step - 1) + 1):
        sc = qk(x) + jnp.where(x == i, diag_bias, jnp.where(x > i, NEG, 0.0))
        s_scr[x] = sc
        mx8 = jnp.maximum(mx8, part(sc, "max"))
    m_s = jnp.max(mx8, axis=0, keepdims=True)
    acc_scr[...] = jnp.zeros_like(acc_scr)

    def pass2(t, carry):
        p0 = jnp.exp(s_scr[2 * t] - m_s)
        p1 = jnp.exp(s_scr[2 * t + 1] - m_s)
        p2 = jnp.concatenate([p0.astype(BF16), p1.astype(BF16)], axis=0)
        v2 = jnp.concatenate([vst[clamp(2 * t)], vst[clamp(2 * t + 1)]], axis=1)
        acc_scr[...] += _dot(v2, p2)
        return carry

    lax.fori_loop(0, n_step, pass2, 0)
    o_sel = acc_scr[0:HEAD_DIM, :] / acc_scr[HEAD_DIM:HEAD_DIM + 1, :]

    def to_rows(o_t):
        return jnp.concatenate([o_t[:, r * tq:(r + 1) * tq] for r in range(R)], axis=0)

    ghi, glo = _split_bf16(_sigmoid(gl_ref[...]))
    row_head = _iota((w, LANES), 0) >> 6
    col = _iota((w, LANES), 1)
    out_t = jnp.zeros((w, tq), F32)
    for br, o_br in enumerate((o_cmp, o_sel, o_win)):
        spread = jnp.where(col == br * (NSA_KV_HEADS * R) + g * R + row_head, 1.0, 0.0).astype(BF16)
        out_t = out_t + (_dot_nt(spread, ghi) + _dot_nt(spread, glo)) * to_rows(o_br)
    o_ref[...] = out_t.T.astype(o_ref.dtype)


def _nsa_kernel_s(q_ref, gl_ref, kc_ref, vc_ref, ks_ref, vs_ref, kw_ref, vw_ref,
                  bd_ref, qg_ref, c_ref, sa_ref, sb_ref, ov_ref,
                  o_ref,
                  kc_a, ks_a, kw_a, vct, vst, vwt, s_scr,
                  *, tq, bk, nsb, scale, q_tiles):
    g = pl.program_id(1)
    w = GROUP_LANES
    R = NSA_GROUP
    rows = R * tq
    lane = _iota((1, w), 1)
    n_tiles = ks_a.shape[0] // bk
    ncp = kc_a.shape[0]

    @pl.when(pl.program_id(2) == 0)
    def _():
        rr, cc = _iota((w, w), 0), _iota((w, w), 1)
        mine = ((rr >> 6) == g) & ((rr & 63) == (cc & 63))
        to_slot0 = jnp.where(mine & (cc < HEAD_DIM), 1.0, 0.0).astype(BF16)
        kc_a[...] = _dot(kc_ref[...], to_slot0).astype(BF16)
        kw_a[...] = _dot(kw_ref[...], to_slot0).astype(BF16)
        seq = ks_a.shape[0]
        tag = jnp.where(_iota((seq, w), 1) - HEAD_DIM == (_iota((seq, w), 0) >> 6), 1.0, 0.0)
        ks_a[...] = (_dot(ks_ref[...], to_slot0) + tag).astype(BF16)
        pick = jnp.where(_iota((V_ROWS, w), 1) == g * HEAD_DIM + _iota((V_ROWS, w), 0), 1.0, 0.0)
        pick = jnp.where(_iota((V_ROWS, w), 0) < HEAD_DIM, pick, 0.0).astype(BF16)
        vct[...] = _dot_nt(pick[0:HEAD_DIM], vc_ref[...]).astype(BF16)
        ones_row = jnp.where(_iota((V_ROWS, bk), 0) == HEAD_DIM, 1.0, 0.0)
        for j in range(n_tiles):
            vst[j] = (_dot_nt(pick, vs_ref[j * bk:(j + 1) * bk, :]) + ones_row).astype(BF16)
            vwt[j] = (_dot_nt(pick, vw_ref[j * bk:(j + 1) * bk, :]) + ones_row).astype(BF16)

    def lanes4(x):
        return jnp.concatenate([x] * R, axis=1)

    def part_max(x):
        return jnp.max(x.reshape(x.shape[0] // 8, 8, rows), axis=0)

    def tile(i):
        qn = _rope(_head_rmsnorm(q_ref[...], bd_ref[...], qg_ref[...]),
                   c_ref[...], sa_ref[...], sb_ref[...]) * scale
        slot0 = lane < HEAD_DIM
        parts = []
        for r in range(R):
            qr = qn if r == 0 else pltpu.roll(qn, w - r * HEAD_DIM, 1)
            parts.append(jnp.where(slot0, qr, 0.0))
        qs_f32 = jnp.concatenate(parts, axis=0)
        qs = qs_f32.astype(BF16)
        t_col = i * tq + (_iota((1, rows), 1) & (tq - 1))
        key_le_query = _iota((bk, tq), 0) <= _iota((bk, tq), 1)
        diag_bias = lanes4(jnp.where(key_le_query, 0.0, NEG))

        s = _dot_nt(kc_a[...], qs)

        win_blk = [j for j in (i - 2, i - 1, i) if j >= 0]
        win_sc = []
        for j in win_blk:
            sc = _dot_nt(kw_a[j * bk:(j + 1) * bk, :], qs)
            if j == i - 2:
                sc = sc + lanes4(jnp.where(key_le_query, NEG, 0.0))
            elif j == i:
                sc = sc + diag_bias
            win_sc.append(sc)

        c_end = _iota((ncp, 1), 0) * CMP_STRIDE + (CMP_BLOCK - 1)
        s = jnp.where(c_end <= t_col, s, NEG)
        e = jnp.exp(s - jnp.max(s, axis=0, keepdims=True))
        p = e / jnp.sum(e, axis=0, keepdims=True)
        p = jnp.where(t_col >= CMP_BLOCK - 1, p, 0.0)
        o_cmp = _dot(vct[...], p.astype(BF16))
        p_sum = p[:, 0:tq]
        for r in range(1, R):
            p_sum = p_sum + p[:, r * tq:(r + 1) * tq]

        hi, lo = _split_bf16(p_sum)
        ov = ov_ref[...]
        imp = (_dot(ov, hi) + _dot(ov, lo))[0:nsb]
        jb = _iota((nsb, 1), 0)
        cur = (i * tq + _iota((1, tq), 1)) >> 6
        forced = (jb == 0) | ((cur - jb >= 0) & (cur - jb < N_LOCAL_SEL))
        score = jnp.where(forced, FORCE, jnp.where(jb <= cur, imp, -FORCE))
        cnt = jnp.zeros((nsb, tq), F32)
        for i2 in range(nsb):
            row = score[i2:i2 + 1, :]
            beats = (row > score) | ((row == score) & (jb > i2))
            cnt = cnt + jnp.where(beats, 1.0, 0.0)
        sel_t = jnp.where((cnt < float(min(N_SEL, nsb))) & (jb <= cur), 0.0, -MASK_BIG)
        pad_t = jnp.concatenate([jnp.zeros((HEAD_DIM, tq), F32), sel_t,
                                 jnp.zeros((LANES - HEAD_DIM - nsb, tq), F32)], axis=0)
        aug = jnp.concatenate([pad_t.T, jnp.zeros((tq, w - LANES), F32)], axis=1)
        qsel = (qs_f32 + jnp.concatenate([aug] * R, axis=0)).astype(BF16)

        m_w = win_sc[0]
        for sc in win_sc[1:]:
            m_w = jnp.maximum(m_w, sc)
        m_w = jnp.max(m_w, axis=0, keepdims=True)
        p_cat = jnp.concatenate([jnp.exp(sc - m_w).astype(BF16) for sc in win_sc], axis=0)
        v_cat = jnp.concatenate([vwt[j] for j in win_blk], axis=1)
        acc_w = _dot(v_cat, p_cat)
        o_win = acc_w[0:HEAD_DIM] / acc_w[HEAD_DIM:HEAD_DIM + 1]

        mx8 = None
        for x in range(i + 1):
            sc = _dot_nt(ks_a[x * bk:(x + 1) * bk, :], qsel)
            if x == i:
                sc = sc + diag_bias
            s_scr[x] = sc
            mx8 = part_max(sc) if mx8 is None else jnp.maximum(mx8, part_max(sc))
        m_s = jnp.max(mx8, axis=0, keepdims=True)

        acc_s = None
        for x0 in range(0, i + 1, 2):
            xs = [x for x in (x0, x0 + 1) if x <= i]
            p2 = jnp.concatenate([jnp.exp(s_scr[x] - m_s).astype(BF16) for x in xs], axis=0)
            v2 = jnp.concatenate([vst[x] for x in xs], axis=1)
            d = _dot(v2, p2)
            acc_s = d if acc_s is None else acc_s + d
        o_sel = acc_s[0:HEAD_DIM] / acc_s[HEAD_DIM:HEAD_DIM + 1]

        def to_rows(o_t):
            return jnp.concatenate([o_t[:, r * tq:(r + 1) * tq] for r in range(R)], axis=0)

        ghi, glo = _split_bf16(_sigmoid(gl_ref[...]))
        row_head = _iota((w, LANES), 0) >> 6
        col = _iota((w, LANES), 1)
        out_t = jnp.zeros((w, tq), F32)
        for br, o_br in enumerate((o_cmp, o_sel, o_win)):
            spread = jnp.where(col == br * (NSA_KV_HEADS * R) + g * R + row_head, 1.0, 0.0).astype(BF16)
            out_t = out_t + (_dot_nt(spread, ghi) + _dot_nt(spread, glo)) * to_rows(o_br)
        o_ref[...] = out_t.T.astype(o_ref.dtype)

    for n in range(q_tiles):
        pl.when(pl.program_id(2) == n)(functools.partial(tile, n))


def _nsa_attention(proj, kc, vc, ksw, q_gain, coeffs, B, S):
    T = B * S
    w = GROUP_LANES
    G = NSA_KV_HEADS
    tq, bk = ATT_TQ, ATT_BK
    nq = S // tq
    nc = S // CMP_STRIDE
    nsb = S // SEL_BLOCK
    n_cmp = (S - CMP_BLOCK) // CMP_STRIDE + 1
    cs = np.arange(nc) * CMP_STRIDE
    js = np.arange(LANES) * SEL_BLOCK
    ov = ((cs[None, :] < js[:, None] + SEL_BLOCK) & (cs[None, :] + CMP_BLOCK > js[:, None])
          & (np.arange(nc)[None, :] < n_cmp) & (np.arange(LANES)[:, None] < nsb))
    ov = jnp.asarray(ov.astype(np.float32), dtype=BF16)
    gate_blk = (G * NSA_GROUP * HEAD_DIM) // LANES
    assert tq == bk and WINDOW == 2 * bk and HEAD_DIM + nsb <= LANES
    kern = functools.partial(_nsa_kernel_s, tq=tq, bk=bk, nsb=nsb, scale=HEAD_DIM ** -0.5, q_tiles=nq)
    seq = lambda col: pl.BlockSpec((S, w), lambda b, g, i: (b, col))
    const = lambda shape: pl.BlockSpec(shape, lambda b, g, i: (0, 0))
    tab = pl.BlockSpec((tq, w), lambda b, g, i: (i, 0))
    rows = NSA_GROUP * tq
    n_tiles = S // bk
    return pl.pallas_call(
        kern,
        grid=(B, G, nq),
        in_specs=[
            pl.BlockSpec((tq, w), lambda b, g, i: (b * nq + i, g)),
            pl.BlockSpec((tq, LANES), lambda b, g, i: (b * nq + i, gate_blk)),
            pl.BlockSpec((nc, w), lambda b, g, i: (b, 0)),
            pl.BlockSpec((nc, w), lambda b, g, i: (b, 0)),
            seq(0), seq(1), seq(2), seq(3),
            const((w, w)), const((1, w)), tab, tab, tab, const((LANES, nc)),
        ],
        out_specs=pl.BlockSpec((tq, w), lambda b, g, i: (b * nq + i, g)),
        out_shape=jax.ShapeDtypeStruct((T, G * w), BF16),
        scratch_shapes=[
            pltpu.VMEM((nc, w), BF16), pltpu.VMEM((S, w), BF16), pltpu.VMEM((S, w), BF16),
            pltpu.VMEM((HEAD_DIM, nc), BF16),
            pltpu.VMEM((n_tiles, V_ROWS, bk), BF16), pltpu.VMEM((n_tiles, V_ROWS, bk), BF16),
            pltpu.VMEM((n_tiles, bk, rows), F32),
        ],
        compiler_params=_params("parallel", "arbitrary", "arbitrary"),
        name="nsa_attn",
    )(proj, proj, kc, vc, ksw, ksw, ksw, ksw,
      _block_mean_matrix(w), jnp.tile(q_gain, NSA_GROUP).reshape(1, w), *coeffs, ov)


def kernel(x, ffn1_norm, ffn1_w_in, ffn1_w_out, mix_norm, ffn2_norm, ffn2_w_in, ffn2_w_out,
           sb_w_qkv, sb_w_out, kv_norm, nsa_w_kv, nsa_k_norm, cmp_pos_k, cmp_pos_v,
           cmp_k_w1, cmp_k_w2, cmp_v_w1, cmp_v_w2, nsa_w_in, nsa_q_norm, nsa_w_out):
    B, S, D = x.shape
    depth = ffn1_norm.shape[0]
    n_a = sb_w_qkv.shape[0]
    n_sb_heads = sb_w_out.shape[1] // HEAD_DIM
    assert S % 512 == 0 and S // CMP_STRIDE == LANES, "sequence tiling assumes 16*128 tokens"
    h = x.reshape(B * S, D)
    bf = lambda a: a.astype(BF16)
    coeffs = _rope_coeffs(jnp.arange(S), NSA_KV_HEADS)
    n_q = nsa_w_in.shape[-1]
    n_q_pad = -(-n_q // LANES) * LANES
    kc = vc = ksw = None
    w1_in, w1_out, w2_in, w2_out = bf(ffn1_w_in), bf(ffn1_w_out), bf(ffn2_w_in), bf(ffn2_w_out)
    w_qkv, w_sb_out, w_nsa_out = bf(sb_w_qkv), bf(sb_w_out), bf(nsa_w_out)
    w_nsa_in = jnp.pad(bf(nsa_w_in), ((0, 0), (0, 0), (0, n_q_pad - n_q)))
    w_kv = bf(nsa_w_kv)[None]
    for layer in range(depth):
        h = _ffn(h, ffn1_norm[layer], w1_in, w1_out, layer)
        if layer < n_a:
            qkv = _norm_proj(h, mix_norm[layer], w_qkv, layer, 1024, BF16)
            o = _sb_attention(qkv, B, S, n_sb_heads)
            h = _out_proj(h, o, w_sb_out, layer)
        else:
            li = layer - n_a
            proj = _norm_proj(h, mix_norm[layer], w_nsa_in, li, n_q_pad, F32)
            o = _nsa_attention(proj, kc, vc, ksw, nsa_q_norm[li], coeffs, B, S)
            h = _out_proj(h, o, w_nsa_out, li)
        h = _ffn(h, ffn2_norm[layer], w2_in, w2_out, layer)
        if layer == n_a - 1:
            kv = _norm_proj(h, kv_norm, w_kv, 0, 768, F32)
            kc, vc = _compress(kv, B, S, cmp_pos_k, cmp_pos_v, cmp_k_w1, cmp_k_w2,
                               cmp_v_w1, cmp_v_w2, nsa_k_norm[0])
            ksw = _kv_prep(kv, nsa_k_norm, coeffs, S)
    return h.reshape(B, S, D)
```

```python
import functools

import jax
import jax.numpy as jnp
import numpy as np
from jax import lax
from jax.experimental import pallas as pl
from jax.experimental.pallas import tpu as pltpu

F32 = jnp.float32
BF16 = jnp.bfloat16

HEAD_DIM = 64
NSA_KV_HEADS = 4
NSA_GROUP = 4
ROPE_DIMS = 16
ROPE_THETA = 500000.0
CMP_BLOCK = 32
CMP_STRIDE = 16
SEL_BLOCK = 64
N_SEL = 8
N_LOCAL_SEL = 2
WINDOW = 512
EPS = 1e-6
LOG2E = 1.4426950408889634
NEG = -1e30
FORCE = 1e4

LANES = 128
GROUP_LANES = NSA_GROUP * HEAD_DIM
VMEM_LIMIT_BYTES = 56 * 1024 * 1024

FFN_TM = 512
PROJ_TM = 1024
ATT_TQ = 256
ATT_BK = 256


def _dot(a, b):
    return jnp.dot(a, b, preferred_element_type=F32)


def _dot_nt(a, b):
    return lax.dot_general(a, b, (((1,), (1,)), ((), ())), preferred_element_type=F32)


def _split_bf16(x):
    hi = x.astype(BF16)
    lo = (x - hi.astype(F32)).astype(BF16)
    return hi, lo


def _dot_split(x, w):
    hi, lo = _split_bf16(x)
    n = x.shape[0]
    r = _dot(jnp.concatenate([hi, lo], axis=0), w)
    return r[0:n] + r[n:2 * n]


def _sigmoid(x):
    return 1.0 / (1.0 + jnp.exp(-x))


def _iota(shape, dim):
    return lax.broadcasted_iota(jnp.int32, shape, dim)


def _params(*sem):
    return pltpu.CompilerParams(dimension_semantics=sem, vmem_limit_bytes=VMEM_LIMIT_BYTES)


def _ffn_kernel(h_ref, g_ref, wi_ref, wo_ref, o_ref, *, tf, nf):
    x = h_ref[...]
    ms = jnp.mean(x * x, axis=-1, keepdims=True)
    xn = (x * lax.rsqrt(ms + EPS) * g_ref[...]).astype(BF16)
    ff = tf * nf
    ab = [(_dot(xn, wi_ref[:, c * tf:(c + 1) * tf]), _dot(xn, wi_ref[:, ff + c * tf:ff + (c + 1) * tf]))
          for c in range(nf)]
    acc = None
    for c, (a, b) in enumerate(ab):
        act = (a * _sigmoid(a) * b).astype(BF16)
        part = _dot(act, wo_ref[c * tf:(c + 1) * tf, :])
        acc = part if acc is None else acc + part
    o_ref[...] = x + 0.5 * acc


def _ffn(h, gain, w_in, w_out, layer):
    T, D = h.shape
    F = w_out.shape[1]
    tm = min(FFN_TM, T)
    tf = F // 2 if (F // 2) % LANES == 0 else F
    nf = F // tf
    resident = dict(pipeline_mode=pl.Buffered(1))
    return pl.pallas_call(
        functools.partial(_ffn_kernel, tf=tf, nf=nf),
        grid=(T // tm,),
        in_specs=[
            pl.BlockSpec((tm, D), lambda i: (i, 0)),
            pl.BlockSpec((1, D), lambda i: (0, 0)),
            pl.BlockSpec((None, D, 2 * F), lambda i: (layer, 0, 0), **resident),
            pl.BlockSpec((None, F, D), lambda i: (layer, 0, 0), **resident),
        ],
        out_specs=pl.BlockSpec((tm, D), lambda i: (i, 0)),
        out_shape=jax.ShapeDtypeStruct((T, D), F32),
        compiler_params=_params("parallel"),
        name="ffn",
    )(h, gain.reshape(1, D), w_in, w_out)


def _norm_proj_kernel(h_ref, g_ref, w_ref, o_ref, xn_ref):
    @pl.when(pl.program_id(1) == 0)
    def _():
        x = h_ref[...]
        ms = jnp.mean(x * x, axis=-1, keepdims=True)
        xn_ref[...] = (x * lax.rsqrt(ms + EPS) * g_ref[...]).astype(BF16)

    o_ref[...] = _dot(xn_ref[...], w_ref[...]).astype(o_ref.dtype)


def _norm_proj(h, gain, w, layer, tn, out_dtype):
    T, D = h.shape
    N = w.shape[2]
    tm = min(PROJ_TM, T)
    return pl.pallas_call(
        _norm_proj_kernel,
        grid=(T // tm, N // tn),
        in_specs=[
            pl.BlockSpec((tm, D), lambda i, j: (i, 0)),
            pl.BlockSpec((1, D), lambda i, j: (0, 0)),
            pl.BlockSpec((None, D, tn), lambda i, j: (layer, 0, j)),
        ],
        out_specs=pl.BlockSpec((tm, tn), lambda i, j: (i, j)),
        out_shape=jax.ShapeDtypeStruct((T, N), out_dtype),
        scratch_shapes=[pltpu.VMEM((tm, D), BF16)],
        compiler_params=_params("parallel", "arbitrary"),
        name="norm_proj",
    )(h, gain.reshape(1, D), w)


def _out_proj_kernel(h_ref, x_ref, w_ref, o_ref):
    o_ref[...] = h_ref[...] + _dot(x_ref[...], w_ref[...])


def _out_proj(h, x, w, layer):
    T, D = h.shape
    K = x.shape[1]
    tm = min(PROJ_TM, T)
    return pl.pallas_call(
        _out_proj_kernel,
        grid=(T // tm,),
        in_specs=[
            pl.BlockSpec((tm, D), lambda i: (i, 0)),
            pl.BlockSpec((tm, K), lambda i: (i, 0)),
            pl.BlockSpec((None, K, D), lambda i: (layer, 0, 0)),
        ],
        out_specs=pl.BlockSpec((tm, D), lambda i: (i, 0)),
        out_shape=jax.ShapeDtypeStruct((T, D), F32),
        compiler_params=_params("parallel"),
        name="out_proj",
    )(h, x, w)


def _sb_kernel(q_ref, k_ref, v_ref, o_ref, *, tq, bk, scale, q_tiles):
    i = pl.program_id(2)
    lane = _iota((1, LANES), 1)
    is_a = lane < HEAD_DIM
    q = q_ref[...] * scale
    zero = jnp.zeros_like(q)
    qh = (jnp.where(is_a, q, zero), jnp.where(is_a, zero, q))
    tri = jnp.where(_iota((bk, bk), 0) > _iota((bk, bk), 1), 1.0, 0.0).astype(BF16)

    def sweep_static(n_blocks):
        past = _iota((tq, bk), 1) < _iota((tq, bk), 0)
        chains = [(j, hd) for j in range(n_blocks - 1, -1, -1) for hd in range(2)]
        n = len(chains)
        z, log_b, log_1m, within = {}, {}, {}, {}
        carry = [jnp.zeros((tq, 1), F32), jnp.zeros((tq, 1), F32)]
        acc = [jnp.zeros((tq, LANES), F32), jnp.zeros((tq, LANES), F32)]
        for k in range(n + 2):
            if k < n:
                j, hd = chains[k]
                z[k] = _dot_nt(qh[hd], k_ref[j * bk:(j + 1) * bk, :])
            if 0 <= k - 1 < n:
                c = k - 1
                j, hd = chains[c]
                e = jnp.exp2(jnp.abs(z[c]) * (-LOG2E))
                log_b[c] = jnp.minimum(z[c], 0.0) - jnp.log(1.0 + e)
                m = log_b[c] - z.pop(c)
                log_1m[c] = jnp.where(past, m, 0.0) if j == n_blocks - 1 else m
                within[c] = _dot(log_1m[c].astype(BF16), tri)
            if 0 <= k - 2 < n:
                c = k - 2
                j, hd = chains[c]
                wc = jnp.exp(log_b.pop(c) + (carry[hd] + within.pop(c)))
                if j == n_blocks - 1:
                    wc = jnp.where(past, wc, 0.0)
                carry[hd] = carry[hd] + jnp.sum(log_1m.pop(c), axis=1, keepdims=True)
                acc[hd] = acc[hd] + _dot(wc.astype(BF16), v_ref[j * bk:(j + 1) * bk, :])
        return acc

    for n_blocks in range(1, q_tiles + 1):
        @pl.when(i == n_blocks - 1)
        def _(n_blocks=n_blocks):
            acc_a, acc_b = sweep_static(n_blocks)
            o_ref[...] = jnp.where(is_a, acc_a, acc_b).astype(o_ref.dtype)


def _sb_attention(qkv, B, S, n_heads):
    T = B * S
    tq, bk = ATT_TQ, ATT_BK
    nq = S // tq
    n_pairs = n_heads * HEAD_DIM // LANES
    kern = functools.partial(_sb_kernel, tq=tq, bk=bk, scale=HEAD_DIM ** -0.5, q_tiles=nq)
    return pl.pallas_call(
        kern,
        grid=(B, n_pairs, nq),
        in_specs=[
            pl.BlockSpec((tq, LANES), lambda b, p, i: (b * nq + i, p)),
            pl.BlockSpec((S, LANES), lambda b, p, i: (b, n_pairs + p)),
            pl.BlockSpec((S, LANES), lambda b, p, i: (b, 2 * n_pairs + p)),
        ],
        out_specs=pl.BlockSpec((tq, LANES), lambda b, p, i: (b * nq + i, p)),
        out_shape=jax.ShapeDtypeStruct((T, n_heads * HEAD_DIM), BF16),
        compiler_params=_params("parallel", "parallel", "arbitrary"),
        name="sb_attn",
    )(qkv, qkv, qkv)


def _head_rmsnorm(x, bd, gain):
    ms = _dot_split(x * x, bd)
    return x * lax.rsqrt(ms + EPS) * gain


def _rope(y, c, sa, sb):
    w = y.shape[-1]
    half = ROPE_DIMS // 2
    return y * c + pltpu.roll(y, w - half, 1) * sa + pltpu.roll(y, half, 1) * sb


def _rope_coeffs(positions, n_heads):
    inv_freq = jnp.power(ROPE_THETA, -jnp.arange(0, ROPE_DIMS, 2, dtype=F32) / ROPE_DIMS)
    ang = positions.astype(F32)[:, None] * inv_freq[None, :]
    cos, sin = jnp.cos(ang), jnp.sin(ang)
    p = positions.shape[0]
    rest = HEAD_DIM - ROPE_DIMS
    zeros_h = jnp.zeros((p, ROPE_DIMS // 2), F32)
    c = jnp.concatenate([cos, cos, jnp.ones((p, rest), F32)], axis=-1)
    sa = jnp.concatenate([-sin, zeros_h, jnp.zeros((p, rest), F32)], axis=-1)
    sb = jnp.concatenate([zeros_h, sin, jnp.zeros((p, rest), F32)], axis=-1)
    return tuple(jnp.tile(t, (1, n_heads)) for t in (c, sa, sb))


def _block_mean_matrix(width):
    idx = np.arange(width) // HEAD_DIM
    return jnp.asarray((idx[:, None] == idx[None, :]).astype(np.float32) / HEAD_DIM, dtype=BF16)


def _kv_prep_kernel(sel_ref, win_ref, bd_ref, gs_ref, gw_ref, c_ref, sa_ref, sb_ref, o_ref):
    w = GROUP_LANES
    bd = bd_ref[...]
    c, sa, sb = c_ref[...], sa_ref[...], sb_ref[...]
    ks = _rope(_head_rmsnorm(sel_ref[:, :w], bd, gs_ref[...]), c, sa, sb)
    kw = _rope(_head_rmsnorm(win_ref[:, :w], bd, gw_ref[...]), c, sa, sb)
    o_ref[:, 0 * w:1 * w] = ks.astype(BF16)
    o_ref[:, 1 * w:2 * w] = sel_ref[:, w:].astype(BF16)
    o_ref[:, 2 * w:3 * w] = kw.astype(BF16)
    o_ref[:, 3 * w:4 * w] = win_ref[:, w:].astype(BF16)


def _kv_prep(kv, k_norm, coeffs, S):
    T = kv.shape[0]
    w = GROUP_LANES
    tm = min(PROJ_TM, S)
    ns = S // tm
    gs = jnp.tile(k_norm[1], NSA_KV_HEADS).reshape(1, w)
    gw = jnp.tile(k_norm[2], NSA_KV_HEADS).reshape(1, w)
    tab = pl.BlockSpec((tm, w), lambda i: (i % ns, 0))
    one = pl.BlockSpec((1, w), lambda i: (0, 0))
    return pl.pallas_call(
        _kv_prep_kernel,
        grid=(T // tm,),
        in_specs=[
            pl.BlockSpec((tm, 2 * w), lambda i: (i, 1)),
            pl.BlockSpec((tm, 2 * w), lambda i: (i, 2)),
            pl.BlockSpec((w, w), lambda i: (0, 0)),
            one, one, tab, tab, tab,
        ],
        out_specs=pl.BlockSpec((tm, 4 * w), lambda i: (i, 0)),
        out_shape=jax.ShapeDtypeStruct((T, 4 * w), BF16),
        compiler_params=_params("parallel"),
        name="kv_prep",
    )(kv, kv, _block_mean_matrix(w), gs, gw, *coeffs)


def _compress_kernel(x_ref, pos_ref, w1_ref, w2_ref, bd_ref, g_ref, c_ref, sa_ref, sb_ref,
                     o_ref, top_ref, bot_ref, wbd_ref):
    t = pl.program_id(0)
    l = pl.program_id(1)
    half = CMP_BLOCK // 2
    hid = w1_ref.shape[-1]

    @pl.when(l == 0)
    def _():
        top_ref[...] = jnp.zeros_like(top_ref)
        bot_ref[...] = jnp.zeros_like(bot_ref)
        wbd_ref[...] = jnp.zeros_like(wbd_ref)

    for u in range(2):
        for g in range(NSA_KV_HEADS):
            wbd_ref[u, g * HEAD_DIM:(g + 1) * HEAD_DIM, g * hid:(g + 1) * hid] = w1_ref[0, u]

    x = x_ref[0]
    top_ref[...] += _dot((x + pos_ref[0, 0]).astype(BF16), wbd_ref[0])
    bot_ref[...] += _dot((x + pos_ref[0, 1]).astype(BF16), wbd_ref[1])

    @pl.when(l == half - 1)
    def _():
        rows = bot_ref.shape[0]
        hidden = top_ref[...] + pltpu.roll(bot_ref[...], rows - 1, 0)
        act = (hidden * _sigmoid(hidden)).astype(BF16)
        y = _dot(act, w2_ref[0])
        yk = _rope(_head_rmsnorm(y, bd_ref[...], g_ref[...]), c_ref[...], sa_ref[...], sb_ref[...])
        o_ref[0] = jnp.where(t == 0, yk, y).astype(BF16)


def _compress(kv, B, S, pos_k, pos_v, k_w1, k_w2, v_w1, v_w2, k_gain):
    w = GROUP_LANES
    G = NSA_KV_HEADS
    half = CMP_BLOCK // 2
    nc = S // CMP_STRIDE
    rows = B * nc
    hid = k_w1.shape[1]
    raw = kv[:, :2 * w].reshape(B, nc, half, 2, w)
    xc = jnp.transpose(raw, (3, 2, 0, 1, 4)).reshape(2, half, rows, w)
    eye = jnp.eye(G, dtype=F32)

    def blockdiag(wm):
        out = jnp.einsum('gh,...ab->...gahb', eye, wm)
        return out.reshape(*wm.shape[:-2], G * wm.shape[-2], G * wm.shape[-1])

    def prep_w1(w1):
        return jnp.transpose(w1.reshape(2, half, HEAD_DIM, hid), (1, 0, 2, 3))

    w1 = jnp.stack([prep_w1(k_w1), prep_w1(v_w1)]).astype(BF16)
    w2 = jnp.stack([blockdiag(k_w2), blockdiag(v_w2)]).astype(BF16)

    def prep_pos(p):
        p = jnp.tile(p, (1, G)).reshape(2, half, 1, w)
        return jnp.transpose(p, (1, 0, 2, 3))

    pos = jnp.stack([prep_pos(pos_k), prep_pos(pos_v)])
    c_end = jnp.arange(nc) * CMP_STRIDE + (CMP_BLOCK - 1)
    coeffs = tuple(jnp.tile(tb, (B, 1)) for tb in _rope_coeffs(c_end, G))
    gain = jnp.tile(k_gain, G).reshape(1, w)
    const2 = lambda t, l: (0, 0)
    out = pl.pallas_call(
        _compress_kernel,
        grid=(2, half),
        in_specs=[
            pl.BlockSpec((None, 1, rows, w), lambda t, l: (t, l, 0, 0)),
            pl.BlockSpec((None, 1, 2, 1, w), lambda t, l: (t, l, 0, 0, 0)),
            pl.BlockSpec((None, 1, 2, HEAD_DIM, hid), lambda t, l: (t, l, 0, 0, 0)),
            pl.BlockSpec((1, G * hid, w), lambda t, l: (t, 0, 0)),
            pl.BlockSpec((w, w), const2),
            pl.BlockSpec((1, w), const2),
            pl.BlockSpec((rows, w), const2),
            pl.BlockSpec((rows, w), const2),
            pl.BlockSpec((rows, w), const2),
        ],
        out_specs=pl.BlockSpec((1, rows, w), lambda t, l: (t, 0, 0)),
        out_shape=jax.ShapeDtypeStruct((2, rows, w), BF16),
        scratch_shapes=[pltpu.VMEM((rows, G * hid), F32), pltpu.VMEM((rows, G * hid), F32),
                        pltpu.VMEM((2, w, G * hid), BF16)],
        compiler_params=_params("arbitrary", "arbitrary"),
        name="nsa_compress",
    )(xc, pos, w1, w2, _block_mean_matrix(w), gain, *coeffs)
    return out[0], out[1]


MASK_BIG = 2.0 ** 30
V_ROWS = HEAD_DIM + 16


def _nsa_kernel(q_ref, gl_ref, kc_ref, vc_ref, ks_ref, vs_ref, kw_ref, vw_ref,
                bd_ref, qg_ref, c_ref, sa_ref, sb_ref, ov_ref,
                o_ref,
                kc_a, ks_a, kw_a, vct, vst, vwt, s_scr,
                *, tq, bk, nsb, scale, q_tiles):
    g = pl.program_id(1)
    w = GROUP_LANES
    R = NSA_GROUP
    rows = R * tq
    lane = _iota((1, w), 1)
    n_tiles = ks_a.shape[0] // bk
    ncp = kc_a.shape[0]

    @pl.when(pl.program_id(2) == 0)
    def _():
        rr, cc = _iota((w, w), 0), _iota((w, w), 1)
        mine = ((rr >> 6) == g) & ((rr & 63) == (cc & 63))
        to_slot0 = jnp.where(mine & (cc < HEAD_DIM), 1.0, 0.0).astype(BF16)
        kc_a[...] = _dot(kc_ref[...], to_slot0).astype(BF16)
        kw_a[...] = _dot(kw_ref[...], to_slot0).astype(BF16)
        seq = ks_a.shape[0]
        tag = jnp.where(_iota((seq, w), 1) - HEAD_DIM == (_iota((seq, w), 0) >> 6), 1.0, 0.0)
        ks_a[...] = (_dot(ks_ref[...], to_slot0) + tag).astype(BF16)
        pick = jnp.where(_iota((V_ROWS, w), 1) == g * HEAD_DIM + _iota((V_ROWS, w), 0), 1.0, 0.0)
        pick = jnp.where(_iota((V_ROWS, w), 0) < HEAD_DIM, pick, 0.0).astype(BF16)
        vct[...] = _dot_nt(pick[0:HEAD_DIM], vc_ref[...]).astype(BF16)
        ones_row = jnp.where(_iota((V_ROWS, bk), 0) == HEAD_DIM, 1.0, 0.0)
        for j in range(n_tiles):
            vst[j] = (_dot_nt(pick, vs_ref[j * bk:(j + 1) * bk, :]) + ones_row).astype(BF16)
            vwt[j] = (_dot_nt(pick, vw_ref[j * bk:(j + 1) * bk, :]) + ones_row).astype(BF16)

    def lanes4(x):
        return jnp.concatenate([x] * R, axis=1)

    def part_max(x):
        return jnp.max(x.reshape(x.shape[0] // 8, 8, rows), axis=0)

    def tile(i):
        qn = _rope(_head_rmsnorm(q_ref[...], bd_ref[...], qg_ref[...]),
                   c_ref[...], sa_ref[...], sb_ref[...]) * scale
        slot0 = lane < HEAD_DIM
        parts = []
        for r in range(R):
            qr = qn if r == 0 else pltpu.roll(qn, w - r * HEAD_DIM, 1)
            parts.append(jnp.where(slot0, qr, 0.0))
        qs_f32 = jnp.concatenate(parts, axis=0)
        qs = qs_f32.astype(BF16)
        t_col = i * tq + (_iota((1, rows), 1) & (tq - 1))
        key_le_query = _iota((bk, tq), 0) <= _iota((bk, tq), 1)
        diag_bias = lanes4(jnp.where(key_le_query, 0.0, NEG))

        s = _dot_nt(kc_a[...], qs)

        win_blk = [j for j in (i - 2, i - 1, i) if j >= 0]
        win_sc = []
        for j in win_blk:
            sc = _dot_nt(kw_a[j * bk:(j + 1) * bk, :], qs)
            if j == i - 2:
                sc = sc + lanes4(jnp.where(key_le_query, NEG, 0.0))
            elif j == i:
                sc = sc + diag_bias
            win_sc.append(sc)

        c_end = _iota((ncp, 1), 0) * CMP_STRIDE + (CMP_BLOCK - 1)
        s = jnp.where(c_end <= t_col, s, NEG)
        e = jnp.exp(s - jnp.max(s, axis=0, keepdims=True))
        p = e / jnp.sum(e, axis=0, keepdims=True)
        p = jnp.where(t_col >= CMP_BLOCK - 1, p, 0.0)
        o_cmp = _dot(vct[...], p.astype(BF16))
        p_sum = p[:, 0:tq]
        for r in range(1, R):
            p_sum = p_sum + p[:, r * tq:(r + 1) * tq]

        hi, lo = _split_bf16(p_sum)
        ov = ov_ref[...]
        imp = (_dot(ov, hi) + _dot(ov, lo))[0:nsb]
        jb = _iota((nsb, 1), 0)
        cur = (i * tq + _iota((1, tq), 1)) >> 6
        forced = (jb == 0) | ((cur - jb >= 0) & (cur - jb < N_LOCAL_SEL))
        score = jnp.where(forced, FORCE, jnp.where(jb <= cur, imp, -FORCE))
        cnt = jnp.zeros((nsb, tq), F32)
        for i2 in range(nsb):
            row = score[i2:i2 + 1, :]
            beats = (row > score) | ((row == score) & (jb > i2))
            cnt = cnt + jnp.where(beats, 1.0, 0.0)
        sel_t = jnp.where((cnt < float(min(N_SEL, nsb))) & (jb <= cur), 0.0, -MASK_BIG)
        pad_t = jnp.concatenate([jnp.zeros((HEAD_DIM, tq), F32), sel_t,
                                 jnp.zeros((LANES - HEAD_DIM - nsb, tq), F32)], axis=0)
        aug = jnp.concatenate([pad_t.T, jnp.zeros((tq, w - LANES), F32)], axis=1)
        qsel = (qs_f32 + jnp.concatenate([aug] * R, axis=0)).astype(BF16)

        m_w = win_sc[0]
        for sc in win_sc[1:]:
            m_w = jnp.maximum(m_w, sc)
        m_w = jnp.max(m_w, axis=0, keepdims=True)
        p_cat = jnp.concatenate([jnp.exp(sc - m_w).astype(BF16) for sc in win_sc], axis=0)
        v_cat = jnp.concatenate([vwt[j] for j in win_blk], axis=1)
        acc_w = _dot(v_cat, p_cat)
        o_win = acc_w[0:HEAD_DIM] / acc_w[HEAD_DIM:HEAD_DIM + 1]

        mx8 = None
        for x in range(i + 1):
            sc = _dot_nt(ks_a[x * bk:(x + 1) * bk, :], qsel)
            if x == i:
                sc = sc + diag_bias
            s_scr[x] = sc
            mx8 = part_max(sc) if mx8 is None else jnp.maximum(mx8, part_max(sc))
        m_s = jnp.max(mx8, axis=0, keepdims=True)

        acc_s = None
        for x0 in range(0, i + 1, 2):
            xs = [x for x in (x0, x0 + 1) if x <= i]
            p2 = jnp.concatenate([jnp.exp(s_scr[x] - m_s).astype(BF16) for x in xs], axis=0)
            v2 = jnp.concatenate([vst[x] for x in xs], axis=1)
            d = _dot(v2, p2)
            acc_s = d if acc_s is None else acc_s + d
        o_sel = acc_s[0:HEAD_DIM] / acc_s[HEAD_DIM:HEAD_DIM + 1]

        def to_rows(o_t):
            return jnp.concatenate([o_t[:, r * tq:(r + 1) * tq] for r in range(R)], axis=0)

        ghi, glo = _split_bf16(_sigmoid(gl_ref[...]))
        row_head = _iota((w, LANES), 0) >> 6
        col = _iota((w, LANES), 1)
        out_t = jnp.zeros((w, tq), F32)
        for br, o_br in enumerate((o_cmp, o_sel, o_win)):
            spread = jnp.where(col == br * (NSA_KV_HEADS * R) + g * R + row_head, 1.0, 0.0).astype(BF16)
            out_t = out_t + (_dot_nt(spread, ghi) + _dot_nt(spread, glo)) * to_rows(o_br)
        o_ref[...] = out_t.T.astype(o_ref.dtype)

    for n in range(q_tiles):
        pl.when(pl.program_id(2) == n)(functools.partial(tile, n))


def _nsa_attention(proj, kc, vc, ksw, q_gain, coeffs, B, S):
    T = B * S
    w = GROUP_LANES
    G = NSA_KV_HEADS
    tq, bk = ATT_TQ, ATT_BK
    nq = S // tq
    nc = S // CMP_STRIDE
    nsb = S // SEL_BLOCK
    n_cmp = (S - CMP_BLOCK) // CMP_STRIDE + 1
    cs = np.arange(nc) * CMP_STRIDE
    js = np.arange(LANES) * SEL_BLOCK
    ov = ((cs[None, :] < js[:, None] + SEL_BLOCK) & (cs[None, :] + CMP_BLOCK > js[:, None])
          & (np.arange(nc)[None, :] < n_cmp) & (np.arange(LANES)[:, None] < nsb))
    ov = jnp.asarray(ov.astype(np.float32), dtype=BF16)
    gate_blk = (G * NSA_GROUP * HEAD_DIM) // LANES
    assert tq == bk and WINDOW == 2 * bk and HEAD_DIM + nsb <= LANES
    kern = functools.partial(_nsa_kernel, tq=tq, bk=bk, nsb=nsb, scale=HEAD_DIM ** -0.5, q_tiles=nq)
    seq = lambda col: pl.BlockSpec((S, w), lambda b, g, i: (b, col))
    const = lambda shape: pl.BlockSpec(shape, lambda b, g, i: (0, 0))
    tab = pl.BlockSpec((tq, w), lambda b, g, i: (i, 0))
    rows = NSA_GROUP * tq
    n_tiles = S // bk
    return pl.pallas_call(
        kern,
        grid=(B, G, nq),
        in_specs=[
            pl.BlockSpec((tq, w), lambda b, g, i: (b * nq + i, g)),
            pl.BlockSpec((tq, LANES), lambda b, g, i: (b * nq + i, gate_blk)),
            pl.BlockSpec((nc, w), lambda b, g, i: (b, 0)),
            pl.BlockSpec((nc, w), lambda b, g, i: (b, 0)),
            seq(0), seq(1), seq(2), seq(3),
            const((w, w)), const((1, w)), tab, tab, tab, const((LANES, nc)),
        ],
        out_specs=pl.BlockSpec((tq, w), lambda b, g, i: (b * nq + i, g)),
        out_shape=jax.ShapeDtypeStruct((T, G * w), BF16),
        scratch_shapes=[
            pltpu.VMEM((nc, w), BF16), pltpu.VMEM((S, w), BF16), pltpu.VMEM((S, w), BF16),
            pltpu.VMEM((HEAD_DIM, nc), BF16),
            pltpu.VMEM((n_tiles, V_ROWS, bk), BF16), pltpu.VMEM((n_tiles, V_ROWS, bk), BF16),
            pltpu.VMEM((n_tiles, bk, rows), F32),
        ],
        compiler_params=_params("parallel", "arbitrary", "arbitrary"),
        name="nsa_attn",
    )(proj, proj, kc, vc, ksw, ksw, ksw, ksw,
      _block_mean_matrix(w), jnp.tile(q_gain, NSA_GROUP).reshape(1, w), *coeffs, ov)


def kernel(x, ffn1_norm, ffn1_w_in, ffn1_w_out, mix_norm, ffn2_norm, ffn2_w_in, ffn2_w_out,
           sb_w_qkv, sb_w_out, kv_norm, nsa_w_kv, nsa_k_norm, cmp_pos_k, cmp_pos_v,
           cmp_k_w1, cmp_k_w2, cmp_v_w1, cmp_v_w2, nsa_w_in, nsa_q_norm, nsa_w_out):
    B, S, D = x.shape
    depth = ffn1_norm.shape[0]
    n_a = sb_w_qkv.shape[0]
    n_sb_heads = sb_w_out.shape[1] // HEAD_DIM
    assert S % 512 == 0 and S // CMP_STRIDE == LANES, "sequence tiling assumes 16*128 tokens"
    h = x.reshape(B * S, D)
    bf = lambda a: a.astype(BF16)
    coeffs = _rope_coeffs(jnp.arange(S), NSA_KV_HEADS)
    n_q = nsa_w_in.shape[-1]
    n_q_pad = -(-n_q // LANES) * LANES
    kc = vc = ksw = None
    w1_in, w1_out, w2_in, w2_out = bf(ffn1_w_in), bf(ffn1_w_out), bf(ffn2_w_in), bf(ffn2_w_out)
    w_qkv, w_sb_out, w_nsa_out = bf(sb_w_qkv), bf(sb_w_out), bf(nsa_w_out)
    w_nsa_in = jnp.pad(bf(nsa_w_in), ((0, 0), (0, 0), (0, n_q_pad - n_q)))
    w_kv = bf(nsa_w_kv)[None]
    for layer in range(depth):
        h = _ffn(h, ffn1_norm[layer], w1_in, w1_out, layer)
        if layer < n_a:
            qkv = _norm_proj(h, mix_norm[layer], w_qkv, layer, 1024, BF16)
            o = _sb_attention(qkv, B, S, n_sb_heads)
            h = _out_proj(h, o, w_sb_out, layer)
        else:
            li = layer - n_a
            proj = _norm_proj(h, mix_norm[layer], w_nsa_in, li, n_q_pad, F32)
            o = _nsa_attention(proj, kc, vc, ksw, nsa_q_norm[li], coeffs, B, S)
            h = _out_proj(h, o, w_nsa_out, li)
        h = _ffn(h, ffn2_norm[layer], w2_in, w2_out, layer)
        if layer == n_a - 1:
            kv = _norm_proj(h, kv_norm, w_kv, 0, 768, F32)
            kc, vc = _compress(kv, B, S, cmp_pos_k, cmp_pos_v, cmp_k_w1, cmp_k_w2,
                               cmp_v_w1, cmp_v_w2, nsa_k_norm[0])
            ksw = _kv_prep(kv, nsa_k_norm, coeffs, S)
    return h.reshape(B, S, D)
```

```python
import functools

import jax
import jax.numpy as jnp
import numpy as np
from jax import lax
from jax.experimental import pallas as pl
from jax.experimental.pallas import tpu as pltpu

F32 = jnp.float32
BF16 = jnp.bfloat16

HEAD_DIM = 64
NSA_KV_HEADS = 4
NSA_GROUP = 4
ROPE_DIMS = 16
ROPE_THETA = 500000.0
CMP_BLOCK = 32
CMP_STRIDE = 16
SEL_BLOCK = 64
N_SEL = 8
N_LOCAL_SEL = 2
WINDOW = 512
EPS = 1e-6
LOG2E = 1.4426950408889634
NEG = -1e30
FORCE = 1e4

LANES = 128
GROUP_LANES = NSA_GROUP * HEAD_DIM
VMEM_LIMIT_BYTES = 56 * 1024 * 1024

FFN_TM = 512
PROJ_TM = 1024
ATT_TQ = 256
ATT_BK = 256


def _dot(a, b):
    return jnp.dot(a, b, preferred_element_type=F32)


def _dot_nt(a, b):
    return lax.dot_general(a, b, (((1,), (1,)), ((), ())), preferred_element_type=F32)


def _split_bf16(x):
    hi = x.astype(BF16)
    lo = (x - hi.astype(F32)).astype(BF16)
    return hi, lo


def _dot_split(x, w):
    hi, lo = _split_bf16(x)
    n = x.shape[0]
    r = _dot(jnp.concatenate([hi, lo], axis=0), w)
    return r[0:n] + r[n:2 * n]


def _sigmoid(x):
    return 1.0 / (1.0 + jnp.exp(-x))


def _iota(shape, dim):
    return lax.broadcasted_iota(jnp.int32, shape, dim)


def _params(*sem):
    return pltpu.CompilerParams(dimension_semantics=sem, vmem_limit_bytes=VMEM_LIMIT_BYTES)


def _ffn_kernel(h_ref, g_ref, wi_ref, wo_ref, o_ref, *, tf, nf):
    x = h_ref[...]
    ms = jnp.mean(x * x, axis=-1, keepdims=True)
    xn = (x * lax.rsqrt(ms + EPS) * g_ref[...]).astype(BF16)
    ff = tf * nf
    ab = [(_dot(xn, wi_ref[:, c * tf:(c + 1) * tf]), _dot(xn, wi_ref[:, ff + c * tf:ff + (c + 1) * tf]))
          for c in range(nf)]
    acc = None
    for c, (a, b) in enumerate(ab):
        act = (a * _sigmoid(a) * b).astype(BF16)
        part = _dot(act, wo_ref[c * tf:(c + 1) * tf, :])
        acc = part if acc is None else acc + part
    o_ref[...] = x + 0.5 * acc


def _ffn(h, gain, w_in, w_out, layer):
    T, D = h.shape
    F = w_out.shape[1]
    tm = min(FFN_TM, T)
    tf = F // 2 if (F // 2) % LANES == 0 else F
    nf = F // tf
    resident = dict(pipeline_mode=pl.Buffered(1))
    return pl.pallas_call(
        functools.partial(_ffn_kernel, tf=tf, nf=nf),
        grid=(T // tm,),
        in_specs=[
            pl.BlockSpec((tm, D), lambda i: (i, 0)),
            pl.BlockSpec((1, D), lambda i: (0, 0)),
            pl.BlockSpec((None, D, 2 * F), lambda i: (layer, 0, 0), **resident),
            pl.BlockSpec((None, F, D), lambda i: (layer, 0, 0), **resident),
        ],
        out_specs=pl.BlockSpec((tm, D), lambda i: (i, 0)),
        out_shape=jax.ShapeDtypeStruct((T, D), F32),
        compiler_params=_params("parallel"),
        name="ffn",
    )(h, gain.reshape(1, D), w_in, w_out)


def _norm_proj_kernel(h_ref, g_ref, w_ref, o_ref, xn_ref):
    @pl.when(pl.program_id(1) == 0)
    def _():
        x = h_ref[...]
        ms = jnp.mean(x * x, axis=-1, keepdims=True)
        xn_ref[...] = (x * lax.rsqrt(ms + EPS) * g_ref[...]).astype(BF16)

    o_ref[...] = _dot(xn_ref[...], w_ref[...]).astype(o_ref.dtype)


def _norm_proj(h, gain, w, layer, tn, out_dtype):
    T, D = h.shape
    N = w.shape[2]
    tm = min(PROJ_TM, T)
    return pl.pallas_call(
        _norm_proj_kernel,
        grid=(T // tm, N // tn),
        in_specs=[
            pl.BlockSpec((tm, D), lambda i, j: (i, 0)),
            pl.BlockSpec((1, D), lambda i, j: (0, 0)),
            pl.BlockSpec((None, D, tn), lambda i, j: (layer, 0, j)),
        ],
        out_specs=pl.BlockSpec((tm, tn), lambda i, j: (i, j)),
        out_shape=jax.ShapeDtypeStruct((T, N), out_dtype),
        scratch_shapes=[pltpu.VMEM((tm, D), BF16)],
        compiler_params=_params("parallel", "arbitrary"),
        name="norm_proj",
    )(h, gain.reshape(1, D), w)


def _out_proj_kernel(h_ref, x_ref, w_ref, o_ref):
    o_ref[...] = h_ref[...] + _dot(x_ref[...], w_ref[...])


def _out_proj(h, x, w, layer):
    T, D = h.shape
    K = x.shape[1]
    tm = min(PROJ_TM, T)
    return pl.pallas_call(
        _out_proj_kernel,
        grid=(T // tm,),
        in_specs=[
            pl.BlockSpec((tm, D), lambda i: (i, 0)),
            pl.BlockSpec((tm, K), lambda i: (i, 0)),
            pl.BlockSpec((None, K, D), lambda i: (layer, 0, 0)),
        ],
        out_specs=pl.BlockSpec((tm, D), lambda i: (i, 0)),
        out_shape=jax.ShapeDtypeStruct((T, D), F32),
        compiler_params=_params("parallel"),
        name="out_proj",
    )(h, x, w)


def _sb_kernel(q_ref, k_ref, v_ref, o_ref, *, tq, bk, scale, q_tiles):
    lane = _iota((1, LANES), 1)
    is_a = lane < HEAD_DIM
    tri = jnp.where(_iota((bk, bk), 0) > _iota((bk, bk), 1), 1.0, 0.0).astype(BF16)
    past = _iota((tq, bk), 1) < _iota((tq, bk), 0)

    def sweep_static(n_blocks):
        rows = slice((n_blocks - 1) * tq, n_blocks * tq)
        q = q_ref[rows, :] * scale
        zero = jnp.zeros_like(q)
        qh = (jnp.where(is_a, q, zero), jnp.where(is_a, zero, q))
        chains = [(j, hd) for j in range(n_blocks - 1, -1, -1) for hd in range(2)]
        n = len(chains)
        z, log_b, log_1m, within = {}, {}, {}, {}
        carry = [jnp.zeros((tq, 1), F32), jnp.zeros((tq, 1), F32)]
        acc = [jnp.zeros((tq, LANES), F32), jnp.zeros((tq, LANES), F32)]
        for k in range(n + 2):
            if k < n:
                j, hd = chains[k]
                z[k] = _dot_nt(qh[hd], k_ref[j * bk:(j + 1) * bk, :])
            if 0 <= k - 1 < n:
                c = k - 1
                j, hd = chains[c]
                e = jnp.exp2(jnp.abs(z[c]) * (-LOG2E))
                log_b[c] = jnp.minimum(z[c], 0.0) - jnp.log(1.0 + e)
                m = log_b[c] - z.pop(c)
                log_1m[c] = jnp.where(past, m, 0.0) if j == n_blocks - 1 else m
                within[c] = _dot(log_1m[c].astype(BF16), tri)
            if 0 <= k - 2 < n:
                c = k - 2
                j, hd = chains[c]
                wc = jnp.exp(log_b.pop(c) + (carry[hd] + within.pop(c)))
                if j == n_blocks - 1:
                    wc = jnp.where(past, wc, 0.0)
                carry[hd] = carry[hd] + jnp.sum(log_1m.pop(c), axis=1, keepdims=True)
                acc[hd] = acc[hd] + _dot(wc.astype(BF16), v_ref[j * bk:(j + 1) * bk, :])
        o_ref[rows, :] = jnp.where(is_a, acc[0], acc[1]).astype(o_ref.dtype)

    for n_blocks in range(1, q_tiles + 1):
        sweep_static(n_blocks)


def _sb_attention(qkv, B, S, n_heads):
    T = B * S
    tq, bk = ATT_TQ, ATT_BK
    nq = S // tq
    n_pairs = n_heads * HEAD_DIM // LANES
    kern = functools.partial(_sb_kernel, tq=tq, bk=bk, scale=HEAD_DIM ** -0.5, q_tiles=nq)
    return pl.pallas_call(
        kern,
        grid=(B, n_pairs),
        in_specs=[
            pl.BlockSpec((S, LANES), lambda b, p: (b, p)),
            pl.BlockSpec((S, LANES), lambda b, p: (b, n_pairs + p)),
            pl.BlockSpec((S, LANES), lambda b, p: (b, 2 * n_pairs + p)),
        ],
        out_specs=pl.BlockSpec((S, LANES), lambda b, p: (b, p)),
        out_shape=jax.ShapeDtypeStruct((T, n_heads * HEAD_DIM), BF16),
        compiler_params=_params("parallel", "parallel"),
        name="sb_attn",
    )(qkv, qkv, qkv)


def _head_rmsnorm(x, bd, gain):
    ms = _dot_split(x * x, bd)
    return x * lax.rsqrt(ms + EPS) * gain


def _rope(y, c, sa, sb):
    w = y.shape[-1]
    half = ROPE_DIMS // 2
    return y * c + pltpu.roll(y, w - half, 1) * sa + pltpu.roll(y, half, 1) * sb


def _rope_coeffs(positions, n_heads):
    inv_freq = jnp.power(ROPE_THETA, -jnp.arange(0, ROPE_DIMS, 2, dtype=F32) / ROPE_DIMS)
    ang = positions.astype(F32)[:, None] * inv_freq[None, :]
    cos, sin = jnp.cos(ang), jnp.sin(ang)
    p = positions.shape[0]
    rest = HEAD_DIM - ROPE_DIMS
    zeros_h = jnp.zeros((p, ROPE_DIMS // 2), F32)
    c = jnp.concatenate([cos, cos, jnp.ones((p, rest), F32)], axis=-1)
    sa = jnp.concatenate([-sin, zeros_h, jnp.zeros((p, rest), F32)], axis=-1)
    sb = jnp.concatenate([zeros_h, sin, jnp.zeros((p, rest), F32)], axis=-1)
    return tuple(jnp.tile(t, (1, n_heads)) for t in (c, sa, sb))


def _block_mean_matrix(width):
    idx = np.arange(width) // HEAD_DIM
    return jnp.asarray((idx[:, None] == idx[None, :]).astype(np.float32) / HEAD_DIM, dtype=BF16)


def _kv_prep_kernel(sel_ref, win_ref, bd_ref, gs_ref, gw_ref, c_ref, sa_ref, sb_ref, o_ref):
    w = GROUP_LANES
    bd = bd_ref[...]
    c, sa, sb = c_ref[...], sa_ref[...], sb_ref[...]
    ks = _rope(_head_rmsnorm(sel_ref[:, :w], bd, gs_ref[...]), c, sa, sb)
    kw = _rope(_head_rmsnorm(win_ref[:, :w], bd, gw_ref[...]), c, sa, sb)
    o_ref[:, 0 * w:1 * w] = ks.astype(BF16)
    o_ref[:, 1 * w:2 * w] = sel_ref[:, w:].astype(BF16)
    o_ref[:, 2 * w:3 * w] = kw.astype(BF16)
    o_ref[:, 3 * w:4 * w] = win_ref[:, w:].astype(BF16)


def _kv_prep(kv, k_norm, coeffs, S):
    T = kv.shape[0]
    w = GROUP_LANES
    tm = min(PROJ_TM, S)
    ns = S // tm
    gs = jnp.tile(k_norm[1], NSA_KV_HEADS).reshape(1, w)
    gw = jnp.tile(k_norm[2], NSA_KV_HEADS).reshape(1, w)
    tab = pl.BlockSpec((tm, w), lambda i: (i % ns, 0))
    one = pl.BlockSpec((1, w), lambda i: (0, 0))
    return pl.pallas_call(
        _kv_prep_kernel,
        grid=(T // tm,),
        in_specs=[
            pl.BlockSpec((tm, 2 * w), lambda i: (i, 1)),
            pl.BlockSpec((tm, 2 * w), lambda i: (i, 2)),
            pl.BlockSpec((w, w), lambda i: (0, 0)),
            one, one, tab, tab, tab,
        ],
        out_specs=pl.BlockSpec((tm, 4 * w), lambda i: (i, 0)),
        out_shape=jax.ShapeDtypeStruct((T, 4 * w), BF16),
        compiler_params=_params("parallel"),
        name="kv_prep",
    )(kv, kv, _block_mean_matrix(w), gs, gw, *coeffs)


def _compress_kernel(x_ref, pos_ref, w1_ref, w2_ref, bd_ref, g_ref, c_ref, sa_ref, sb_ref,
                     o_ref, top_ref, bot_ref, wbd_ref):
    t = pl.program_id(0)
    l = pl.program_id(1)
    half = CMP_BLOCK // 2
    hid = w1_ref.shape[-1]

    @pl.when(l == 0)
    def _():
        top_ref[...] = jnp.zeros_like(top_ref)
        bot_ref[...] = jnp.zeros_like(bot_ref)
        wbd_ref[...] = jnp.zeros_like(wbd_ref)

    for u in range(2):
        for g in range(NSA_KV_HEADS):
            wbd_ref[u, g * HEAD_DIM:(g + 1) * HEAD_DIM, g * hid:(g + 1) * hid] = w1_ref[0, u]

    x = x_ref[0]
    top_ref[...] += _dot((x + pos_ref[0, 0]).astype(BF16), wbd_ref[0])
    bot_ref[...] += _dot((x + pos_ref[0, 1]).astype(BF16), wbd_ref[1])

    @pl.when(l == half - 1)
    def _():
        rows = bot_ref.shape[0]
        hidden = top_ref[...] + pltpu.roll(bot_ref[...], rows - 1, 0)
        act = (hidden * _sigmoid(hidden)).astype(BF16)
        y = _dot(act, w2_ref[0])
        yk = _rope(_head_rmsnorm(y, bd_ref[...], g_ref[...]), c_ref[...], sa_ref[...], sb_ref[...])
        o_ref[0] = jnp.where(t == 0, yk, y).astype(BF16)


def _compress(kv, B, S, pos_k, pos_v, k_w1, k_w2, v_w1, v_w2, k_gain):
    w = GROUP_LANES
    G = NSA_KV_HEADS
    half = CMP_BLOCK // 2
    nc = S // CMP_STRIDE
    rows = B * nc
    hid = k_w1.shape[1]
    raw = kv[:, :2 * w].reshape(B, nc, half, 2, w)
    xc = jnp.transpose(raw, (3, 2, 0, 1, 4)).reshape(2, half, rows, w)
    eye = jnp.eye(G, dtype=F32)

    def blockdiag(wm):
        out = jnp.einsum('gh,...ab->...gahb', eye, wm)
        return out.reshape(*wm.shape[:-2], G * wm.shape[-2], G * wm.shape[-1])

    def prep_w1(w1):
        return jnp.transpose(w1.reshape(2, half, HEAD_DIM, hid), (1, 0, 2, 3))

    w1 = jnp.stack([prep_w1(k_w1), prep_w1(v_w1)]).astype(BF16)
    w2 = jnp.stack([blockdiag(k_w2), blockdiag(v_w2)]).astype(BF16)

    def prep_pos(p):
        p = jnp.tile(p, (1, G)).reshape(2, half, 1, w)
        return jnp.transpose(p, (1, 0, 2, 3))

    pos = jnp.stack([prep_pos(pos_k), prep_pos(pos_v)])
    c_end = jnp.arange(nc) * CMP_STRIDE + (CMP_BLOCK - 1)
    coeffs = tuple(jnp.tile(tb, (B, 1)) for tb in _rope_coeffs(c_end, G))
    gain = jnp.tile(k_gain, G).reshape(1, w)
    const2 = lambda t, l: (0, 0)
    out = pl.pallas_call(
        _compress_kernel,
        grid=(2, half),
        in_specs=[
            pl.BlockSpec((None, 1, rows, w), lambda t, l: (t, l, 0, 0)),
            pl.BlockSpec((None, 1, 2, 1, w), lambda t, l: (t, l, 0, 0, 0)),
            pl.BlockSpec((None, 1, 2, HEAD_DIM, hid), lambda t, l: (t, l, 0, 0, 0)),
            pl.BlockSpec((1, G * hid, w), lambda t, l: (t, 0, 0)),
            pl.BlockSpec((w, w), const2),
            pl.BlockSpec((1, w), const2),
            pl.BlockSpec((rows, w), const2),
            pl.BlockSpec((rows, w), const2),
            pl.BlockSpec((rows, w), const2),
        ],
        out_specs=pl.BlockSpec((1, rows, w), lambda t, l: (t, 0, 0)),
        out_shape=jax.ShapeDtypeStruct((2, rows, w), BF16),
        scratch_shapes=[pltpu.VMEM((rows, G * hid), F32), pltpu.VMEM((rows, G * hid), F32),
                        pltpu.VMEM((2, w, G * hid), BF16)],
        compiler_params=_params("arbitrary", "arbitrary"),
        name="nsa_compress",
    )(xc, pos, w1, w2, _block_mean_matrix(w), gain, *coeffs)
    return out[0], out[1]


MASK_BIG = 2.0 ** 30
V_ROWS = HEAD_DIM + 16


def _nsa_kernel(q_ref, gl_ref, kc_ref, vc_ref, ks_ref, vs_ref, kw_ref, vw_ref,
                bd_ref, qg_ref, c_ref, sa_ref, sb_ref, ov_ref,
                o_ref,
                kc_a, ks_a, kw_a, vct, vst, vwt, s_scr,
                *, tq, bk, nsb, scale, q_tiles):
    g = pl.program_id(1)
    w = GROUP_LANES
    R = NSA_GROUP
    rows = R * tq
    lane = _iota((1, w), 1)
    n_tiles = ks_a.shape[0] // bk
    ncp = kc_a.shape[0]

    def stage_kv():
        rr, cc = _iota((w, w), 0), _iota((w, w), 1)
        mine = ((rr >> 6) == g) & ((rr & 63) == (cc & 63))
        to_slot0 = jnp.where(mine & (cc < HEAD_DIM), 1.0, 0.0).astype(BF16)
        kc_a[...] = _dot(kc_ref[...], to_slot0).astype(BF16)
        kw_a[...] = _dot(kw_ref[...], to_slot0).astype(BF16)
        seq = ks_a.shape[0]
        tag = jnp.where(_iota((seq, w), 1) - HEAD_DIM == (_iota((seq, w), 0) >> 6), 1.0, 0.0)
        ks_a[...] = (_dot(ks_ref[...], to_slot0) + tag).astype(BF16)
        pick = jnp.where(_iota((V_ROWS, w), 1) == g * HEAD_DIM + _iota((V_ROWS, w), 0), 1.0, 0.0)
        pick = jnp.where(_iota((V_ROWS, w), 0) < HEAD_DIM, pick, 0.0).astype(BF16)
        vct[...] = _dot_nt(pick[0:HEAD_DIM], vc_ref[...]).astype(BF16)
        ones_row = jnp.where(_iota((V_ROWS, bk), 0) == HEAD_DIM, 1.0, 0.0)
        for j in range(n_tiles):
            vst[j] = (_dot_nt(pick, vs_ref[j * bk:(j + 1) * bk, :]) + ones_row).astype(BF16)
            vwt[j] = (_dot_nt(pick, vw_ref[j * bk:(j + 1) * bk, :]) + ones_row).astype(BF16)

    def lanes4(x):
        return jnp.concatenate([x] * R, axis=1)

    def part_max(x):
        return jnp.max(x.reshape(x.shape[0] // 8, 8, rows), axis=0)

    def tile(i):
        tok = slice(i * tq, (i + 1) * tq)
        qn = _rope(_head_rmsnorm(q_ref[tok, :], bd_ref[...], qg_ref[...]),
                   c_ref[tok, :], sa_ref[tok, :], sb_ref[tok, :]) * scale
        slot0 = lane < HEAD_DIM
        parts = []
        for r in range(R):
            qr = qn if r == 0 else pltpu.roll(qn, w - r * HEAD_DIM, 1)
            parts.append(jnp.where(slot0, qr, 0.0))
        qs_f32 = jnp.concatenate(parts, axis=0)
        qs = qs_f32.astype(BF16)
        t_col = i * tq + (_iota((1, rows), 1) & (tq - 1))
        key_le_query = _iota((bk, tq), 0) <= _iota((bk, tq), 1)
        diag_bias = lanes4(jnp.where(key_le_query, 0.0, NEG))

        s = _dot_nt(kc_a[...], qs)

        win_blk = [j for j in (i - 2, i - 1, i) if j >= 0]
        win_sc = []
        for j in win_blk:
            sc = _dot_nt(kw_a[j * bk:(j + 1) * bk, :], qs)
            if j == i - 2:
                sc = sc + lanes4(jnp.where(key_le_query, NEG, 0.0))
            elif j == i:
                sc = sc + diag_bias
            win_sc.append(sc)

        c_end = _iota((ncp, 1), 0) * CMP_STRIDE + (CMP_BLOCK - 1)
        s = jnp.where(c_end <= t_col, s, NEG)
        e = jnp.exp(s - jnp.max(s, axis=0, keepdims=True))
        p = e / jnp.sum(e, axis=0, keepdims=True)
        p = jnp.where(t_col >= CMP_BLOCK - 1, p, 0.0)
        o_cmp = _dot(vct[...], p.astype(BF16))
        p_sum = p[:, 0:tq]
        for r in range(1, R):
            p_sum = p_sum + p[:, r * tq:(r + 1) * tq]

        hi, lo = _split_bf16(p_sum)
        ov = ov_ref[...]
        imp = (_dot(ov, hi) + _dot(ov, lo))[0:nsb]
        jb = _iota((nsb, 1), 0)
        cur = (i * tq + _iota((1, tq), 1)) >> 6
        forced = (jb == 0) | ((cur - jb >= 0) & (cur - jb < N_LOCAL_SEL))
        score = jnp.where(forced, FORCE, jnp.where(jb <= cur, imp, -FORCE))
        cnt = jnp.zeros((nsb, tq), F32)
        for i2 in range(nsb):
            row = score[i2:i2 + 1, :]
            beats = (row > score) | ((row == score) & (jb > i2))
            cnt = cnt + jnp.where(beats, 1.0, 0.0)
        sel_t = jnp.where((cnt < float(min(N_SEL, nsb))) & (jb <= cur), 0.0, -MASK_BIG)
        pad_t = jnp.concatenate([jnp.zeros((HEAD_DIM, tq), F32), sel_t,
                                 jnp.zeros((LANES - HEAD_DIM - nsb, tq), F32)], axis=0)
        aug = jnp.concatenate([pad_t.T, jnp.zeros((tq, w - LANES), F32)], axis=1)
        qsel = (qs_f32 + jnp.concatenate([aug] * R, axis=0)).astype(BF16)

        m_w = win_sc[0]
        for sc in win_sc[1:]:
            m_w = jnp.maximum(m_w, sc)
        m_w = jnp.max(m_w, axis=0, keepdims=True)
        p_cat = jnp.concatenate([jnp.exp(sc - m_w).astype(BF16) for sc in win_sc], axis=0)
        v_cat = jnp.concatenate([vwt[j] for j in win_blk], axis=1)
        acc_w = _dot(v_cat, p_cat)
        o_win = acc_w[0:HEAD_DIM] / acc_w[HEAD_DIM:HEAD_DIM + 1]

        mx8 = None
        for x in range(i + 1):
            sc = _dot_nt(ks_a[x * bk:(x + 1) * bk, :], qsel)
            if x == i:
                sc = sc + diag_bias
            s_scr[x] = sc
            mx8 = part_max(sc) if mx8 is None else jnp.maximum(mx8, part_max(sc))
        m_s = jnp.max(mx8, axis=0, keepdims=True)

        acc_s = None
        for x0 in range(0, i + 1, 2):
            xs = [x for x in (x0, x0 + 1) if x <= i]
            p2 = jnp.concatenate([jnp.exp(s_scr[x] - m_s).astype(BF16) for x in xs], axis=0)
            v2 = jnp.concatenate([vst[x] for x in xs], axis=1)
            d = _dot(v2, p2)
            acc_s = d if acc_s is None else acc_s + d
        o_sel = acc_s[0:HEAD_DIM] / acc_s[HEAD_DIM:HEAD_DIM + 1]

        def to_rows(o_t):
            return jnp.concatenate([o_t[:, r * tq:(r + 1) * tq] for r in range(R)], axis=0)

        ghi, glo = _split_bf16(_sigmoid(gl_ref[tok, :]))
        row_head = _iota((w, LANES), 0) >> 6
        col = _iota((w, LANES), 1)
        out_t = jnp.zeros((w, tq), F32)
        for br, o_br in enumerate((o_cmp, o_sel, o_win)):
            spread = jnp.where(col == br * (NSA_KV_HEADS * R) + g * R + row_head, 1.0, 0.0).astype(BF16)
            out_t = out_t + (_dot_nt(spread, ghi) + _dot_nt(spread, glo)) * to_rows(o_br)
        o_ref[tok, :] = out_t.T.astype(o_ref.dtype)

    stage_kv()
    for n in range(q_tiles):
        tile(n)


def _nsa_attention(proj, kc, vc, ksw, q_gain, coeffs, B, S):
    T = B * S
    w = GROUP_LANES
    G = NSA_KV_HEADS
    tq, bk = ATT_TQ, ATT_BK
    nq = S // tq
    nc = S // CMP_STRIDE
    nsb = S // SEL_BLOCK
    n_cmp = (S - CMP_BLOCK) // CMP_STRIDE + 1
    cs = np.arange(nc) * CMP_STRIDE
    js = np.arange(LANES) * SEL_BLOCK
    ov = ((cs[None, :] < js[:, None] + SEL_BLOCK) & (cs[None, :] + CMP_BLOCK > js[:, None])
          & (np.arange(nc)[None, :] < n_cmp) & (np.arange(LANES)[:, None] < nsb))
    ov = jnp.asarray(ov.astype(np.float32), dtype=BF16)
    gate_blk = (G * NSA_GROUP * HEAD_DIM) // LANES
    assert tq == bk and WINDOW == 2 * bk and HEAD_DIM + nsb <= LANES
    kern = functools.partial(_nsa_kernel, tq=tq, bk=bk, nsb=nsb, scale=HEAD_DIM ** -0.5, q_tiles=nq)
    seq = lambda col: pl.BlockSpec((S, w), lambda b, g: (b, col))
    const = lambda shape: pl.BlockSpec(shape, lambda b, g: (0, 0), pipeline_mode=pl.Buffered(1))
    rows = NSA_GROUP * tq
    n_tiles = S // bk
    return pl.pallas_call(
        kern,
        grid=(B, G),
        in_specs=[
            pl.BlockSpec((S, w), lambda b, g: (b, g)),
            pl.BlockSpec((S, LANES), lambda b, g: (b, gate_blk)),
            pl.BlockSpec((nc, w), lambda b, g: (b, 0)),
            pl.BlockSpec((nc, w), lambda b, g: (b, 0)),
            seq(0), seq(1), seq(2), seq(3),
            const((w, w)), const((1, w)), const((S, w)), const((S, w)), const((S, w)), const((LANES, nc)),
        ],
        out_specs=pl.BlockSpec((S, w), lambda b, g: (b, g)),
        out_shape=jax.ShapeDtypeStruct((T, G * w), BF16),
        scratch_shapes=[
            pltpu.VMEM((nc, w), BF16), pltpu.VMEM((S, w), BF16), pltpu.VMEM((S, w), BF16),
            pltpu.VMEM((HEAD_DIM, nc), BF16),
            pltpu.VMEM((n_tiles, V_ROWS, bk), BF16), pltpu.VMEM((n_tiles, V_ROWS, bk), BF16),
            pltpu.VMEM((n_tiles, bk, rows), F32),
        ],
        compiler_params=_params("parallel", "parallel"),
        name="nsa_attn",
    )(proj, proj, kc, vc, ksw, ksw, ksw, ksw,
      _block_mean_matrix(w), jnp.tile(q_gain, NSA_GROUP).reshape(1, w), *coeffs, ov)


def kernel(x, ffn1_norm, ffn1_w_in, ffn1_w_out, mix_norm, ffn2_norm, ffn2_w_in, ffn2_w_out,
           sb_w_qkv, sb_w_out, kv_norm, nsa_w_kv, nsa_k_norm, cmp_pos_k, cmp_pos_v,
           cmp_k_w1, cmp_k_w2, cmp_v_w1, cmp_v_w2, nsa_w_in, nsa_q_norm, nsa_w_out):
    B, S, D = x.shape
    depth = ffn1_norm.shape[0]
    n_a = sb_w_qkv.shape[0]
    n_sb_heads = sb_w_out.shape[1] // HEAD_DIM
    assert S % 512 == 0 and S // CMP_STRIDE == LANES, "sequence tiling assumes 16*128 tokens"
    h = x.reshape(B * S, D)
    bf = lambda a: a.astype(BF16)
    coeffs = _rope_coeffs(jnp.arange(S), NSA_KV_HEADS)
    n_q = nsa_w_in.shape[-1]
    n_q_pad = -(-n_q // LANES) * LANES
    kc = vc = ksw = None
    w1_in, w1_out, w2_in, w2_out = bf(ffn1_w_in), bf(ffn1_w_out), bf(ffn2_w_in), bf(ffn2_w_out)
    w_qkv, w_sb_out, w_nsa_out = bf(sb_w_qkv), bf(sb_w_out), bf(nsa_w_out)
    w_nsa_in = jnp.pad(bf(nsa_w_in), ((0, 0), (0, 0), (0, n_q_pad - n_q)))
    w_kv = bf(nsa_w_kv)[None]
    for layer in range(depth):
        h = _ffn(h, ffn1_norm[layer], w1_in, w1_out, layer)
        if layer < n_a:
            qkv = _norm_proj(h, mix_norm[layer], w_qkv, layer, 1024, BF16)
            o = _sb_attention(qkv, B, S, n_sb_heads)
            h = _out_proj(h, o, w_sb_out, layer)
        else:
            li = layer - n_a
            proj = _norm_proj(h, mix_norm[layer], w_nsa_in, li, n_q_pad, F32)
            o = _nsa_attention(proj, kc, vc, ksw, nsa_q_norm[li], coeffs, B, S)
            h = _out_proj(h, o, w_nsa_out, li)
        h = _ffn(h, ffn2_norm[layer], w2_in, w2_out, layer)
        if layer == n_a - 1:
            kv = _norm_proj(h, kv_norm, w_kv, 0, 768, F32)
            kc, vc = _compress(kv, B, S, cmp_pos_k, cmp_pos_v, cmp_k_w1, cmp_k_w2,
                               cmp_v_w1, cmp_v_w2, nsa_k_norm[0])
            ksw = _kv_prep(kv, nsa_k_norm, coeffs, S)
    return h.reshape(B, S, D)
```

```python
import functools

import jax
import jax.numpy as jnp
import numpy as np
from jax import lax
from jax.experimental import pallas as pl
from jax.experimental.pallas import tpu as pltpu

F32 = jnp.float32
BF16 = jnp.bfloat16

HEAD_DIM = 64
NSA_KV_HEADS = 4
NSA_GROUP = 4
ROPE_DIMS = 16
ROPE_THETA = 500000.0
CMP_BLOCK = 32
CMP_STRIDE = 16
SEL_BLOCK = 64
N_SEL = 8
N_LOCAL_SEL = 2
WINDOW = 512
EPS = 1e-6
LOG2E = 1.4426950408889634
NEG = -1e30
FORCE = 1e4

LANES = 128
MXU_TILE = 256
GROUP_LANES = NSA_GROUP * HEAD_DIM
VMEM_LIMIT_BYTES = 56 * 1024 * 1024

FFN_TM = 512
PROJ_TM = 1024
ATT_TQ = 256
ATT_BK = 256


def _dot(a, b):
    return jnp.dot(a, b, preferred_element_type=F32)


def _dot_nt(a, b):
    return lax.dot_general(a, b, (((1,), (1,)), ((), ())), preferred_element_type=F32)


def _split_bf16(x):
    hi = x.astype(BF16)
    lo = (x - hi.astype(F32)).astype(BF16)
    return hi, lo


def _dot_split(x, w):
    hi, lo = _split_bf16(x)
    n = x.shape[0]
    r = _dot(jnp.concatenate([hi, lo], axis=0), w)
    return r[0:n] + r[n:2 * n]


def _sigmoid(x):
    return 1.0 / (1.0 + jnp.exp(-x))


def _iota(shape, dim):
    return lax.broadcasted_iota(jnp.int32, shape, dim)


def _params(*sem):
    return pltpu.CompilerParams(dimension_semantics=sem, vmem_limit_bytes=VMEM_LIMIT_BYTES)


def _ffn_kernel(h_ref, g_ref, wi_ref, wo_ref, o_ref, *, chunks):
    x = h_ref[...]
    ms = jnp.mean(x * x, axis=-1, keepdims=True)
    xn = (x * lax.rsqrt(ms + EPS) * g_ref[...]).astype(BF16)
    ff = chunks[-1][1]
    ab = [(_dot(xn, wi_ref[:, lo:hi]), _dot(xn, wi_ref[:, ff + lo:ff + hi])) for lo, hi in chunks]
    acc = None
    for (lo, hi), (a, b) in zip(chunks, ab):
        act = (a * _sigmoid(a) * b).astype(BF16)
        part = _dot(act, wo_ref[lo:hi, :])
        acc = part if acc is None else acc + part
    o_ref[...] = x + 0.5 * acc


def _ffn(h, gain, w_in, w_out, layer):
    T, D = h.shape
    F = w_out.shape[1]
    tm = min(FFN_TM, T)
    n_mxu = F // MXU_TILE
    split = -(-n_mxu // 2) * MXU_TILE
    chunks = ((0, split), (split, F)) if F % MXU_TILE == 0 and n_mxu > 1 else ((0, F),)
    resident = dict(pipeline_mode=pl.Buffered(1))
    return pl.pallas_call(
        functools.partial(_ffn_kernel, chunks=chunks),
        grid=(T // tm,),
        in_specs=[
            pl.BlockSpec((tm, D), lambda i: (i, 0)),
            pl.BlockSpec((1, D), lambda i: (0, 0)),
            pl.BlockSpec((None, D, 2 * F), lambda i: (layer, 0, 0), **resident),
            pl.BlockSpec((None, F, D), lambda i: (layer, 0, 0), **resident),
        ],
        out_specs=pl.BlockSpec((tm, D), lambda i: (i, 0)),
        out_shape=jax.ShapeDtypeStruct((T, D), F32),
        compiler_params=_params("parallel"),
        name="ffn",
    )(h, gain.reshape(1, D), w_in, w_out)


def _norm_proj_kernel(h_ref, g_ref, w_ref, o_ref, xn_ref):
    @pl.when(pl.program_id(1) == 0)
    def _():
        x = h_ref[...]
        ms = jnp.mean(x * x, axis=-1, keepdims=True)
        xn_ref[...] = (x * lax.rsqrt(ms + EPS) * g_ref[...]).astype(BF16)

    o_ref[...] = _dot(xn_ref[...], w_ref[...]).astype(o_ref.dtype)


def _norm_proj(h, gain, w, layer, tn, out_dtype):
    T, D = h.shape
    N = w.shape[2]
    tm = min(PROJ_TM, T)
    return pl.pallas_call(
        _norm_proj_kernel,
        grid=(T // tm, N // tn),
        in_specs=[
            pl.BlockSpec((tm, D), lambda i, j: (i, 0)),
            pl.BlockSpec((1, D), lambda i, j: (0, 0)),
            pl.BlockSpec((None, D, tn), lambda i, j: (layer, 0, j)),
        ],
        out_specs=pl.BlockSpec((tm, tn), lambda i, j: (i, j)),
        out_shape=jax.ShapeDtypeStruct((T, N), out_dtype),
        scratch_shapes=[pltpu.VMEM((tm, D), BF16)],
        compiler_params=_params("parallel", "arbitrary"),
        name="norm_proj",
    )(h, gain.reshape(1, D), w)


def _out_proj_kernel(h_ref, x_ref, w_ref, o_ref):
    o_ref[...] = h_ref[...] + _dot(x_ref[...], w_ref[...])


def _out_proj(h, x, w, layer):
    T, D = h.shape
    K = x.shape[1]
    tm = min(PROJ_TM, T)
    return pl.pallas_call(
        _out_proj_kernel,
        grid=(T // tm,),
        in_specs=[
            pl.BlockSpec((tm, D), lambda i: (i, 0)),
            pl.BlockSpec((tm, K), lambda i: (i, 0)),
            pl.BlockSpec((None, K, D), lambda i: (layer, 0, 0)),
        ],
        out_specs=pl.BlockSpec((tm, D), lambda i: (i, 0)),
        out_shape=jax.ShapeDtypeStruct((T, D), F32),
        compiler_params=_params("parallel"),
        name="out_proj",
    )(h, x, w)


def _sb_kernel(q_ref, k_ref, v_ref, o_ref, *, tq, bk, scale, q_tiles):
    lane = _iota((1, LANES), 1)
    is_a = lane < HEAD_DIM
    tri = jnp.where(_iota((bk, bk), 0) > _iota((bk, bk), 1), 1.0, 0.0).astype(BF16)
    past = _iota((tq, bk), 1) < _iota((tq, bk), 0)

    def sweep_static(n_blocks):
        rows = slice((n_blocks - 1) * tq, n_blocks * tq)
        q = q_ref[rows, :] * scale
        zero = jnp.zeros_like(q)
        qh = (jnp.where(is_a, q, zero), jnp.where(is_a, zero, q))
        chains = [(j, hd) for j in range(n_blocks - 1, -1, -1) for hd in range(2)]
        n = len(chains)
        z, log_b, log_1m, within = {}, {}, {}, {}
        carry = [jnp.zeros((tq, 1), F32), jnp.zeros((tq, 1), F32)]
        acc = [jnp.zeros((tq, LANES), F32), jnp.zeros((tq, LANES), F32)]
        for k in range(n + 2):
            if k < n:
                j, hd = chains[k]
                z[k] = _dot_nt(qh[hd], k_ref[j * bk:(j + 1) * bk, :])
            if 0 <= k - 1 < n:
                c = k - 1
                j, hd = chains[c]
                e = jnp.exp2(jnp.abs(z[c]) * (-LOG2E))
                log_b[c] = jnp.minimum(z[c], 0.0) - jnp.log(1.0 + e)
                m = log_b[c] - z.pop(c)
                log_1m[c] = jnp.where(past, m, 0.0) if j == n_blocks - 1 else m
                within[c] = _dot(log_1m[c].astype(BF16), tri)
            if 0 <= k - 2 < n:
                c = k - 2
                j, hd = chains[c]
                wc = jnp.exp(log_b.pop(c) + (carry[hd] + within.pop(c)))
                if j == n_blocks - 1:
                    wc = jnp.where(past, wc, 0.0)
                carry[hd] = carry[hd] + jnp.sum(log_1m.pop(c), axis=1, keepdims=True)
                acc[hd] = acc[hd] + _dot(wc.astype(BF16), v_ref[j * bk:(j + 1) * bk, :])
        o_ref[rows, :] = jnp.where(is_a, acc[0], acc[1]).astype(o_ref.dtype)

    for n_blocks in range(1, q_tiles + 1):
        sweep_static(n_blocks)


def _sb_attention(qkv, B, S, n_heads):
    T = B * S
    tq, bk = ATT_TQ, ATT_BK
    nq = S // tq
    n_pairs = n_heads * HEAD_DIM // LANES
    kern = functools.partial(_sb_kernel, tq=tq, bk=bk, scale=HEAD_DIM ** -0.5, q_tiles=nq)
    return pl.pallas_call(
        kern,
        grid=(B, n_pairs),
        in_specs=[
            pl.BlockSpec((S, LANES), lambda b, p: (b, p)),
            pl.BlockSpec((S, LANES), lambda b, p: (b, n_pairs + p)),
            pl.BlockSpec((S, LANES), lambda b, p: (b, 2 * n_pairs + p)),
        ],
        out_specs=pl.BlockSpec((S, LANES), lambda b, p: (b, p)),
        out_shape=jax.ShapeDtypeStruct((T, n_heads * HEAD_DIM), BF16),
        compiler_params=_params("parallel", "parallel"),
        name="sb_attn",
    )(qkv, qkv, qkv)


def _head_rmsnorm(x, bd, gain):
    ms = _dot_split(x * x, bd)
    return x * lax.rsqrt(ms + EPS) * gain


def _rope(y, c, sa, sb):
    w = y.shape[-1]
    half = ROPE_DIMS // 2
    return y * c + pltpu.roll(y, w - half, 1) * sa + pltpu.roll(y, half, 1) * sb


def _rope_coeffs(positions, n_heads):
    inv_freq = jnp.power(ROPE_THETA, -jnp.arange(0, ROPE_DIMS, 2, dtype=F32) / ROPE_DIMS)
    ang = positions.astype(F32)[:, None] * inv_freq[None, :]
    cos, sin = jnp.cos(ang), jnp.sin(ang)
    p = positions.shape[0]
    rest = HEAD_DIM - ROPE_DIMS
    zeros_h = jnp.zeros((p, ROPE_DIMS // 2), F32)
    c = jnp.concatenate([cos, cos, jnp.ones((p, rest), F32)], axis=-1)
    sa = jnp.concatenate([-sin, zeros_h, jnp.zeros((p, rest), F32)], axis=-1)
    sb = jnp.concatenate([zeros_h, sin, jnp.zeros((p, rest), F32)], axis=-1)
    return tuple(jnp.tile(t, (1, n_heads)) for t in (c, sa, sb))


def _block_mean_matrix(width):
    idx = np.arange(width) // HEAD_DIM
    return jnp.asarray((idx[:, None] == idx[None, :]).astype(np.float32) / HEAD_DIM, dtype=BF16)


def _kv_prep_kernel(sel_ref, win_ref, bd_ref, gs_ref, gw_ref, c_ref, sa_ref, sb_ref, o_ref):
    w = GROUP_LANES
    bd = bd_ref[...]
    c, sa, sb = c_ref[...], sa_ref[...], sb_ref[...]
    ks = _rope(_head_rmsnorm(sel_ref[:, :w], bd, gs_ref[...]), c, sa, sb)
    kw = _rope(_head_rmsnorm(win_ref[:, :w], bd, gw_ref[...]), c, sa, sb)
    o_ref[:, 0 * w:1 * w] = ks.astype(BF16)
    o_ref[:, 1 * w:2 * w] = sel_ref[:, w:].astype(BF16)
    o_ref[:, 2 * w:3 * w] = kw.astype(BF16)
    o_ref[:, 3 * w:4 * w] = win_ref[:, w:].astype(BF16)


def _kv_prep(kv, k_norm, coeffs, S):
    T = kv.shape[0]
    w = GROUP_LANES
    tm = min(PROJ_TM, S)
    ns = S // tm
    gs = jnp.tile(k_norm[1], NSA_KV_HEADS).reshape(1, w)
    gw = jnp.tile(k_norm[2], NSA_KV_HEADS).reshape(1, w)
    tab = pl.BlockSpec((tm, w), lambda i: (i % ns, 0))
    one = pl.BlockSpec((1, w), lambda i: (0, 0))
    return pl.pallas_call(
        _kv_prep_kernel,
        grid=(T // tm,),
        in_specs=[
            pl.BlockSpec((tm, 2 * w), lambda i: (i, 1)),
            pl.BlockSpec((tm, 2 * w), lambda i: (i, 2)),
            pl.BlockSpec((w, w), lambda i: (0, 0)),
            one, one, tab, tab, tab,
        ],
        out_specs=pl.BlockSpec((tm, 4 * w), lambda i: (i, 0)),
        out_shape=jax.ShapeDtypeStruct((T, 4 * w), BF16),
        compiler_params=_params("parallel"),
        name="kv_prep",
    )(kv, kv, _block_mean_matrix(w), gs, gw, *coeffs)


def _compress_kernel(x_ref, pos_ref, w1_ref, w2_ref, bd_ref, g_ref, c_ref, sa_ref, sb_ref,
                     o_ref, top_ref, bot_ref, wbd_ref):
    t = pl.program_id(0)
    l = pl.program_id(1)
    half = CMP_BLOCK // 2
    hid = w1_ref.shape[-1]

    @pl.when(l == 0)
    def _():
        top_ref[...] = jnp.zeros_like(top_ref)
        bot_ref[...] = jnp.zeros_like(bot_ref)
        wbd_ref[...] = jnp.zeros_like(wbd_ref)

    for u in range(2):
        for g in range(NSA_KV_HEADS):
            wbd_ref[u, g * HEAD_DIM:(g + 1) * HEAD_DIM, g * hid:(g + 1) * hid] = w1_ref[0, u]

    x = x_ref[0]
    top_ref[...] += _dot((x + pos_ref[0, 0]).astype(BF16), wbd_ref[0])
    bot_ref[...] += _dot((x + pos_ref[0, 1]).astype(BF16), wbd_ref[1])

    @pl.when(l == half - 1)
    def _():
        rows = bot_ref.shape[0]
        hidden = top_ref[...] + pltpu.roll(bot_ref[...], rows - 1, 0)
        act = (hidden * _sigmoid(hidden)).astype(BF16)
        y = _dot(act, w2_ref[0])
        yk = _rope(_head_rmsnorm(y, bd_ref[...], g_ref[...]), c_ref[...], sa_ref[...], sb_ref[...])
        o_ref[0] = jnp.where(t == 0, yk, y).astype(BF16)


def _compress(kv, B, S, pos_k, pos_v, k_w1, k_w2, v_w1, v_w2, k_gain):
    w = GROUP_LANES
    G = NSA_KV_HEADS
    half = CMP_BLOCK // 2
    nc = S // CMP_STRIDE
    rows = B * nc
    hid = k_w1.shape[1]
    raw = kv[:, :2 * w].reshape(B, nc, half, 2, w)
    xc = jnp.transpose(raw, (3, 2, 0, 1, 4)).reshape(2, half, rows, w)
    eye = jnp.eye(G, dtype=F32)

    def blockdiag(wm):
        out = jnp.einsum('gh,...ab->...gahb', eye, wm)
        return out.reshape(*wm.shape[:-2], G * wm.shape[-2], G * wm.shape[-1])

    def prep_w1(w1):
        return jnp.transpose(w1.reshape(2, half, HEAD_DIM, hid), (1, 0, 2, 3))

    w1 = jnp.stack([prep_w1(k_w1), prep_w1(v_w1)]).astype(BF16)
    w2 = jnp.stack([blockdiag(k_w2), blockdiag(v_w2)]).astype(BF16)

    def prep_pos(p):
        p = jnp.tile(p, (1, G)).reshape(2, half, 1, w)
        return jnp.transpose(p, (1, 0, 2, 3))

    pos = jnp.stack([prep_pos(pos_k), prep_pos(pos_v)])
    c_end = jnp.arange(nc) * CMP_STRIDE + (CMP_BLOCK - 1)
    coeffs = tuple(jnp.tile(tb, (B, 1)) for tb in _rope_coeffs(c_end, G))
    gain = jnp.tile(k_gain, G).reshape(1, w)
    const2 = lambda t, l: (0, 0)
    out = pl.pallas_call(
        _compress_kernel,
        grid=(2, half),
        in_specs=[
            pl.BlockSpec((None, 1, rows, w), lambda t, l: (t, l, 0, 0)),
            pl.BlockSpec((None, 1, 2, 1, w), lambda t, l: (t, l, 0, 0, 0)),
            pl.BlockSpec((None, 1, 2, HEAD_DIM, hid), lambda t, l: (t, l, 0, 0, 0)),
            pl.BlockSpec((1, G * hid, w), lambda t, l: (t, 0, 0)),
            pl.BlockSpec((w, w), const2),
            pl.BlockSpec((1, w), const2),
            pl.BlockSpec((rows, w), const2),
            pl.BlockSpec((rows, w), const2),
            pl.BlockSpec((rows, w), const2),
        ],
        out_specs=pl.BlockSpec((1, rows, w), lambda t, l: (t, 0, 0)),
        out_shape=jax.ShapeDtypeStruct((2, rows, w), BF16),
        scratch_shapes=[pltpu.VMEM((rows, G * hid), F32), pltpu.VMEM((rows, G * hid), F32),
                        pltpu.VMEM((2, w, G * hid), BF16)],
        compiler_params=_params("arbitrary", "arbitrary"),
        name="nsa_compress",
    )(xc, pos, w1, w2, _block_mean_matrix(w), gain, *coeffs)
    return out[0], out[1]


MASK_BIG = 2.0 ** 30
V_ROWS = HEAD_DIM + 16


def _nsa_kernel(q_ref, gl_ref, kc_ref, vc_ref, ks_ref, vs_ref, kw_ref, vw_ref,
                bd_ref, qg_ref, c_ref, sa_ref, sb_ref, ov_ref,
                o_ref,
                kc_a, ks_a, kw_a, vct, vst, vwt, s_scr,
                *, tq, bk, nsb, scale, q_tiles):
    g = pl.program_id(1)
    w = GROUP_LANES
    R = NSA_GROUP
    rows = R * tq
    lane = _iota((1, w), 1)
    n_tiles = ks_a.shape[0] // bk
    ncp = kc_a.shape[0]

    def stage_kv():
        rr, cc = _iota((w, w), 0), _iota((w, w), 1)
        mine = ((rr >> 6) == g) & ((rr & 63) == (cc & 63))
        to_slot0 = jnp.where(mine & (cc < HEAD_DIM), 1.0, 0.0).astype(BF16)
        kc_a[...] = _dot(kc_ref[...], to_slot0).astype(BF16)
        kw_a[...] = _dot(kw_ref[...], to_slot0).astype(BF16)
        seq = ks_a.shape[0]
        tag = jnp.where(_iota((seq, w), 1) - HEAD_DIM == (_iota((seq, w), 0) >> 6), 1.0, 0.0)
        ks_a[...] = (_dot(ks_ref[...], to_slot0) + tag).astype(BF16)
        pick = jnp.where(_iota((V_ROWS, w), 1) == g * HEAD_DIM + _iota((V_ROWS, w), 0), 1.0, 0.0)
        pick = jnp.where(_iota((V_ROWS, w), 0) < HEAD_DIM, pick, 0.0).astype(BF16)
        vct[...] = _dot_nt(pick[0:HEAD_DIM], vc_ref[...]).astype(BF16)
        ones_row = jnp.where(_iota((V_ROWS, bk), 0) == HEAD_DIM, 1.0, 0.0)
        for j in range(n_tiles):
            vst[j] = (_dot_nt(pick, vs_ref[j * bk:(j + 1) * bk, :]) + ones_row).astype(BF16)
            vwt[j] = (_dot_nt(pick, vw_ref[j * bk:(j + 1) * bk, :]) + ones_row).astype(BF16)

    def lanes4(x):
        return jnp.concatenate([x] * R, axis=1)

    def part_max(x):
        return jnp.max(x.reshape(x.shape[0] // 8, 8, rows), axis=0)

    def tile(i):
        tok = slice(i * tq, (i + 1) * tq)
        qn = _rope(_head_rmsnorm(q_ref[tok, :], bd_ref[...], qg_ref[...]),
                   c_ref[tok, :], sa_ref[tok, :], sb_ref[tok, :]) * scale
        slot0 = lane < HEAD_DIM
        parts = []
        for r in range(R):
            qr = qn if r == 0 else pltpu.roll(qn, w - r * HEAD_DIM, 1)
            parts.append(jnp.where(slot0, qr, 0.0))
        qs_f32 = jnp.concatenate(parts, axis=0)
        qs = qs_f32.astype(BF16)
        t_col = i * tq + (_iota((1, rows), 1) & (tq - 1))
        key_le_query = _iota((bk, tq), 0) <= _iota((bk, tq), 1)
        diag_bias = lanes4(jnp.where(key_le_query, 0.0, NEG))

        s = _dot_nt(kc_a[...], qs)

        win_blk = [j for j in (i - 2, i - 1, i) if j >= 0]
        win_sc = []
        for j in win_blk:
            sc = _dot_nt(kw_a[j * bk:(j + 1) * bk, :], qs)
            if j == i - 2:
                sc = sc + lanes4(jnp.where(key_le_query, NEG, 0.0))
            elif j == i:
                sc = sc + diag_bias
            win_sc.append(sc)

        c_end = _iota((ncp, 1), 0) * CMP_STRIDE + (CMP_BLOCK - 1)
        s = jnp.where(c_end <= t_col, s, NEG)
        e = jnp.exp(s - jnp.max(s, axis=0, keepdims=True))
        p = e / jnp.sum(e, axis=0, keepdims=True)
        p = jnp.where(t_col >= CMP_BLOCK - 1, p, 0.0)
        o_cmp = _dot(vct[...], p.astype(BF16))
        p_sum = p[:, 0:tq]
        for r in range(1, R):
            p_sum = p_sum + p[:, r * tq:(r + 1) * tq]

        hi, lo = _split_bf16(p_sum)
        ov = ov_ref[...]
        imp = (_dot(ov, hi) + _dot(ov, lo))[0:nsb]
        jb = _iota((nsb, 1), 0)
        cur = (i * tq + _iota((1, tq), 1)) >> 6
        forced = (jb == 0) | ((cur - jb >= 0) & (cur - jb < N_LOCAL_SEL))
        score = jnp.where(forced, FORCE, jnp.where(jb <= cur, imp, -FORCE))
        cnt = jnp.zeros((nsb, tq), F32)
        for i2 in range(nsb):
            row = score[i2:i2 + 1, :]
            beats = (row > score) | ((row == score) & (jb > i2))
            cnt = cnt + jnp.where(beats, 1.0, 0.0)
        sel_t = jnp.where((cnt < float(min(N_SEL, nsb))) & (jb <= cur), 0.0, -MASK_BIG)
        pad_t = jnp.concatenate([jnp.zeros((HEAD_DIM, tq), F32), sel_t,
                                 jnp.zeros((LANES - HEAD_DIM - nsb, tq), F32)], axis=0)
        aug = jnp.concatenate([pad_t.T, jnp.zeros((tq, w - LANES), F32)], axis=1)
        qsel = (qs_f32 + jnp.concatenate([aug] * R, axis=0)).astype(BF16)

        m_w = win_sc[0]
        for sc in win_sc[1:]:
            m_w = jnp.maximum(m_w, sc)
        m_w = jnp.max(m_w, axis=0, keepdims=True)
        p_cat = jnp.concatenate([jnp.exp(sc - m_w).astype(BF16) for sc in win_sc], axis=0)
        v_cat = jnp.concatenate([vwt[j] for j in win_blk], axis=1)
        acc_w = _dot(v_cat, p_cat)
        o_win = acc_w[0:HEAD_DIM] / acc_w[HEAD_DIM:HEAD_DIM + 1]

        mx8 = None
        for x in range(i + 1):
            sc = _dot_nt(ks_a[x * bk:(x + 1) * bk, :], qsel)
            if x == i:
                sc = sc + diag_bias
            s_scr[x] = sc
            mx8 = part_max(sc) if mx8 is None else jnp.maximum(mx8, part_max(sc))
        m_s = jnp.max(mx8, axis=0, keepdims=True)

        acc_s = None
        for x0 in range(0, i + 1, 2):
            xs = [x for x in (x0, x0 + 1) if x <= i]
            p2 = jnp.concatenate([jnp.exp(s_scr[x] - m_s).astype(BF16) for x in xs], axis=0)
            v2 = jnp.concatenate([vst[x] for x in xs], axis=1)
            d = _dot(v2, p2)
            acc_s = d if acc_s is None else acc_s + d
        o_sel = acc_s[0:HEAD_DIM] / acc_s[HEAD_DIM:HEAD_DIM + 1]

        def to_rows(o_t):
            return jnp.concatenate([o_t[:, r * tq:(r + 1) * tq] for r in range(R)], axis=0)

        ghi, glo = _split_bf16(_sigmoid(gl_ref[tok, :]))
        row_head = _iota((w, LANES), 0) >> 6
        col = _iota((w, LANES), 1)
        out_t = jnp.zeros((w, tq), F32)
        for br, o_br in enumerate((o_cmp, o_sel, o_win)):
            spread = jnp.where(col == br * (NSA_KV_HEADS * R) + g * R + row_head, 1.0, 0.0).astype(BF16)
            out_t = out_t + (_dot_nt(spread, ghi) + _dot_nt(spread, glo)) * to_rows(o_br)
        o_ref[tok, :] = out_t.T.astype(o_ref.dtype)

    stage_kv()
    for n in range(q_tiles):
        tile(n)


def _nsa_attention(proj, kc, vc, ksw, q_gain, coeffs, B, S):
    T = B * S
    w = GROUP_LANES
    G = NSA_KV_HEADS
    tq, bk = ATT_TQ, ATT_BK
    nq = S // tq
    nc = S // CMP_STRIDE
    nsb = S // SEL_BLOCK
    n_cmp = (S - CMP_BLOCK) // CMP_STRIDE + 1
    cs = np.arange(nc) * CMP_STRIDE
    js = np.arange(LANES) * SEL_BLOCK
    ov = ((cs[None, :] < js[:, None] + SEL_BLOCK) & (cs[None, :] + CMP_BLOCK > js[:, None])
          & (np.arange(nc)[None, :] < n_cmp) & (np.arange(LANES)[:, None] < nsb))
    ov = jnp.asarray(ov.astype(np.float32), dtype=BF16)
    gate_blk = (G * NSA_GROUP * HEAD_DIM) // LANES
    assert tq == bk and WINDOW == 2 * bk and HEAD_DIM + nsb <= LANES
    kern = functools.partial(_nsa_kernel, tq=tq, bk=bk, nsb=nsb, scale=HEAD_DIM ** -0.5, q_tiles=nq)
    seq = lambda col: pl.BlockSpec((S, w), lambda b, g: (b, col))
    const = lambda shape: pl.BlockSpec(shape, lambda b, g: (0, 0), pipeline_mode=pl.Buffered(1))
    rows = NSA_GROUP * tq
    n_tiles = S // bk
    return pl.pallas_call(
        kern,
        grid=(B, G),
        in_specs=[
            pl.BlockSpec((S, w), lambda b, g: (b, g)),
            pl.BlockSpec((S, LANES), lambda b, g: (b, gate_blk)),
            pl.BlockSpec((nc, w), lambda b, g: (b, 0)),
            pl.BlockSpec((nc, w), lambda b, g: (b, 0)),
            seq(0), seq(1), seq(2), seq(3),
            const((w, w)), const((1, w)), const((S, w)), const((S, w)), const((S, w)), const((LANES, nc)),
        ],
        out_specs=pl.BlockSpec((S, w), lambda b, g: (b, g)),
        out_shape=jax.ShapeDtypeStruct((T, G * w), BF16),
        scratch_shapes=[
            pltpu.VMEM((nc, w), BF16), pltpu.VMEM((S, w), BF16), pltpu.VMEM((S, w), BF16),
            pltpu.VMEM((HEAD_DIM, nc), BF16),
            pltpu.VMEM((n_tiles, V_ROWS, bk), BF16), pltpu.VMEM((n_tiles, V_ROWS, bk), BF16),
            pltpu.VMEM((n_tiles, bk, rows), F32),
        ],
        compiler_params=_params("parallel", "parallel"),
        name="nsa_attn",
    )(proj, proj, kc, vc, ksw, ksw, ksw, ksw,
      _block_mean_matrix(w), jnp.tile(q_gain, NSA_GROUP).reshape(1, w), *coeffs, ov)


def kernel(x, ffn1_norm, ffn1_w_in, ffn1_w_out, mix_norm, ffn2_norm, ffn2_w_in, ffn2_w_out,
           sb_w_qkv, sb_w_out, kv_norm, nsa_w_kv, nsa_k_norm, cmp_pos_k, cmp_pos_v,
           cmp_k_w1, cmp_k_w2, cmp_v_w1, cmp_v_w2, nsa_w_in, nsa_q_norm, nsa_w_out):
    B, S, D = x.shape
    depth = ffn1_norm.shape[0]
    n_a = sb_w_qkv.shape[0]
    n_sb_heads = sb_w_out.shape[1] // HEAD_DIM
    assert S % 512 == 0 and S // CMP_STRIDE == LANES, "sequence tiling assumes 16*128 tokens"
    h = x.reshape(B * S, D)
    bf = lambda a: a.astype(BF16)
    coeffs = _rope_coeffs(jnp.arange(S), NSA_KV_HEADS)
    n_q = nsa_w_in.shape[-1]
    n_q_pad = -(-n_q // LANES) * LANES
    kc = vc = ksw = None
    w1_in, w1_out, w2_in, w2_out = bf(ffn1_w_in), bf(ffn1_w_out), bf(ffn2_w_in), bf(ffn2_w_out)
    w_qkv, w_sb_out, w_nsa_out = bf(sb_w_qkv), bf(sb_w_out), bf(nsa_w_out)
    w_nsa_in = jnp.pad(bf(nsa_w_in), ((0, 0), (0, 0), (0, n_q_pad - n_q)))
    w_kv = bf(nsa_w_kv)[None]
    for layer in range(depth):
        h = _ffn(h, ffn1_norm[layer], w1_in, w1_out, layer)
        if layer < n_a:
            qkv = _norm_proj(h, mix_norm[layer], w_qkv, layer, 1024, BF16)
            o = _sb_attention(qkv, B, S, n_sb_heads)
            h = _out_proj(h, o, w_sb_out, layer)
        else:
            li = layer - n_a
            proj = _norm_proj(h, mix_norm[layer], w_nsa_in, li, n_q_pad, F32)
            o = _nsa_attention(proj, kc, vc, ksw, nsa_q_norm[li], coeffs, B, S)
            h = _out_proj(h, o, w_nsa_out, li)
        h = _ffn(h, ffn2_norm[layer], w2_in, w2_out, layer)
        if layer == n_a - 1:
            kv = _norm_proj(h, kv_norm, w_kv, 0, 768, F32)
            kc, vc = _compress(kv, B, S, cmp_pos_k, cmp_pos_v, cmp_k_w1, cmp_k_w2,
                               cmp_v_w1, cmp_v_w2, nsa_k_norm[0])
            ksw = _kv_prep(kv, nsa_k_norm, coeffs, S)
    return h.reshape(B, S, D)
```

```python
import functools

import jax
import jax.numpy as jnp
import numpy as np
from jax import lax
from jax.experimental import pallas as pl
from jax.experimental.pallas import tpu as pltpu

F32 = jnp.float32
BF16 = jnp.bfloat16

HEAD_DIM = 64
NSA_KV_HEADS = 4
NSA_GROUP = 4
ROPE_DIMS = 16
ROPE_THETA = 500000.0
CMP_BLOCK = 32
CMP_STRIDE = 16
SEL_BLOCK = 64
N_SEL = 8
N_LOCAL_SEL = 2
WINDOW = 512
EPS = 1e-6
LOG2E = 1.4426950408889634
NEG = -1e30
FORCE = 1e4

LANES = 128
MXU_TILE = 256
GROUP_LANES = NSA_GROUP * HEAD_DIM
VMEM_LIMIT_BYTES = 56 * 1024 * 1024

FFN_TM = 512
PROJ_TM = 1024
ATT_TQ = 256
ATT_BK = 256


def _dot(a, b):
    return jnp.dot(a, b, preferred_element_type=F32)


def _dot_nt(a, b):
    return lax.dot_general(a, b, (((1,), (1,)), ((), ())), preferred_element_type=F32)


def _split_bf16(x):
    hi = x.astype(BF16)
    lo = (x - hi.astype(F32)).astype(BF16)
    return hi, lo


def _dot_split(x, w):
    hi, lo = _split_bf16(x)
    n = x.shape[0]
    r = _dot(jnp.concatenate([hi, lo], axis=0), w)
    return r[0:n] + r[n:2 * n]


def _sigmoid(x):
    return 1.0 / (1.0 + jnp.exp(-x))


def _iota(shape, dim):
    return lax.broadcasted_iota(jnp.int32, shape, dim)


def _params(*sem):
    return pltpu.CompilerParams(dimension_semantics=sem, vmem_limit_bytes=VMEM_LIMIT_BYTES)


def _ffn_kernel(h_ref, g_ref, wi_ref, wo_ref, o_ref, *, chunks):
    x = h_ref[...]
    ms = jnp.mean(x * x, axis=-1, keepdims=True)
    xn = (x * lax.rsqrt(ms + EPS) * g_ref[...]).astype(BF16)
    ff = chunks[-1][1]
    ab = [(_dot(xn, wi_ref[:, lo:hi]), _dot(xn, wi_ref[:, ff + lo:ff + hi])) for lo, hi in chunks]
    acc = None
    for (lo, hi), (a, b) in zip(chunks, ab):
        act = (a * _sigmoid(a) * b).astype(BF16)
        part = _dot(act, wo_ref[lo:hi, :])
        acc = part if acc is None else acc + part
    o_ref[...] = x + 0.5 * acc


def _ffn(h, gain, w_in, w_out, layer):
    T, D = h.shape
    F = w_out.shape[1]
    tm = min(FFN_TM, T)
    n_mxu = F // MXU_TILE
    split = -(-n_mxu // 2) * MXU_TILE
    chunks = ((0, split), (split, F)) if F % MXU_TILE == 0 and n_mxu > 1 else ((0, F),)
    resident = dict(pipeline_mode=pl.Buffered(1))
    return pl.pallas_call(
        functools.partial(_ffn_kernel, chunks=chunks),
        grid=(T // tm,),
        in_specs=[
            pl.BlockSpec((tm, D), lambda i: (i, 0)),
            pl.BlockSpec((1, D), lambda i: (0, 0)),
            pl.BlockSpec((None, D, 2 * F), lambda i: (layer, 0, 0), **resident),
            pl.BlockSpec((None, F, D), lambda i: (layer, 0, 0), **resident),
        ],
        out_specs=pl.BlockSpec((tm, D), lambda i: (i, 0)),
        out_shape=jax.ShapeDtypeStruct((T, D), F32),
        compiler_params=_params("parallel"),
        name="ffn",
    )(h, gain.reshape(1, D), w_in, w_out)


def _norm_proj_kernel(h_ref, g_ref, w_ref, o_ref, xn_ref):
    @pl.when(pl.program_id(1) == 0)
    def _():
        x = h_ref[...]
        ms = jnp.mean(x * x, axis=-1, keepdims=True)
        xn_ref[...] = (x * lax.rsqrt(ms + EPS) * g_ref[...]).astype(BF16)

    o_ref[...] = _dot(xn_ref[...], w_ref[...]).astype(o_ref.dtype)


def _norm_proj(h, gain, w, layer, tn, out_dtype):
    T, D = h.shape
    N = w.shape[2]
    tm = min(PROJ_TM, T)
    return pl.pallas_call(
        _norm_proj_kernel,
        grid=(T // tm, N // tn),
        in_specs=[
            pl.BlockSpec((tm, D), lambda i, j: (i, 0)),
            pl.BlockSpec((1, D), lambda i, j: (0, 0)),
            pl.BlockSpec((None, D, tn), lambda i, j: (layer, 0, j)),
        ],
        out_specs=pl.BlockSpec((tm, tn), lambda i, j: (i, j)),
        out_shape=jax.ShapeDtypeStruct((T, N), out_dtype),
        scratch_shapes=[pltpu.VMEM((tm, D), BF16)],
        compiler_params=_params("parallel", "arbitrary"),
        name="norm_proj",
    )(h, gain.reshape(1, D), w)


def _out_proj_kernel(h_ref, x_ref, w_ref, o_ref):
    o_ref[...] = h_ref[...] + _dot(x_ref[...], w_ref[...])


def _out_proj(h, x, w, layer):
    T, D = h.shape
    K = x.shape[1]
    tm = min(PROJ_TM, T)
    return pl.pallas_call(
        _out_proj_kernel,
        grid=(T // tm,),
        in_specs=[
            pl.BlockSpec((tm, D), lambda i: (i, 0)),
            pl.BlockSpec((tm, K), lambda i: (i, 0)),
            pl.BlockSpec((None, K, D), lambda i: (layer, 0, 0)),
        ],
        out_specs=pl.BlockSpec((tm, D), lambda i: (i, 0)),
        out_shape=jax.ShapeDtypeStruct((T, D), F32),
        compiler_params=_params("parallel"),
        name="out_proj",
    )(h, x, w)


def _sb_kernel(q_ref, k_ref, v_ref, o_ref, *, tq, bk, scale, q_tiles):
    lane = _iota((1, LANES), 1)
    is_a = lane < HEAD_DIM
    tri = jnp.where(_iota((bk, bk), 0) > _iota((bk, bk), 1), 1.0, 0.0).astype(BF16)
    past = _iota((tq, bk), 1) < _iota((tq, bk), 0)

    def sweep_static(n_blocks):
        rows = slice((n_blocks - 1) * tq, n_blocks * tq)
        q = q_ref[rows, :] * scale
        zero = jnp.zeros_like(q)
        qh = (jnp.where(is_a, q, zero), jnp.where(is_a, zero, q))
        chains = [(j, hd) for j in range(n_blocks - 1, -1, -1) for hd in range(2)]
        n = len(chains)
        z, log_b, log_1m, within = {}, {}, {}, {}
        carry = [jnp.zeros((tq, 1), F32), jnp.zeros((tq, 1), F32)]
        acc = [jnp.zeros((tq, LANES), F32), jnp.zeros((tq, LANES), F32)]
        for k in range(n + 2):
            if k < n:
                j, hd = chains[k]
                z[k] = _dot_nt(qh[hd], k_ref[j * bk:(j + 1) * bk, :])
            if 0 <= k - 1 < n:
                c = k - 1
                j, hd = chains[c]
                e = jnp.exp2(jnp.abs(z[c]) * (-LOG2E))
                log_b[c] = jnp.minimum(z[c], 0.0) - jnp.log(1.0 + e)
                m = log_b[c] - z.pop(c)
                log_1m[c] = jnp.where(past, m, 0.0) if j == n_blocks - 1 else m
                within[c] = _dot(log_1m[c].astype(BF16), tri)
            if 0 <= k - 2 < n:
                c = k - 2
                j, hd = chains[c]
                wc = jnp.exp(log_b.pop(c) + (carry[hd] + within.pop(c)))
                if j == n_blocks - 1:
                    wc = jnp.where(past, wc, 0.0)
                carry[hd] = carry[hd] + jnp.sum(log_1m.pop(c), axis=1, keepdims=True)
                acc[hd] = acc[hd] + _dot(wc.astype(BF16), v_ref[j * bk:(j + 1) * bk, :])
        o_ref[rows, :] = jnp.where(is_a, acc[0], acc[1]).astype(o_ref.dtype)

    for n_blocks in range(1, q_tiles + 1):
        sweep_static(n_blocks)


def _sb_attention(qkv, B, S, n_heads):
    T = B * S
    tq, bk = ATT_TQ, ATT_BK
    nq = S // tq
    n_pairs = n_heads * HEAD_DIM // LANES
    kern = functools.partial(_sb_kernel, tq=tq, bk=bk, scale=HEAD_DIM ** -0.5, q_tiles=nq)
    return pl.pallas_call(
        kern,
        grid=(B, n_pairs),
        in_specs=[
            pl.BlockSpec((S, LANES), lambda b, p: (b, p)),
            pl.BlockSpec((S, LANES), lambda b, p: (b, n_pairs + p)),
            pl.BlockSpec((S, LANES), lambda b, p: (b, 2 * n_pairs + p)),
        ],
        out_specs=pl.BlockSpec((S, LANES), lambda b, p: (b, p)),
        out_shape=jax.ShapeDtypeStruct((T, n_heads * HEAD_DIM), BF16),
        compiler_params=_params("parallel", "parallel"),
        name="sb_attn",
    )(qkv, qkv, qkv)


def _head_rmsnorm(x, bd, gain):
    ms = _dot_split(x * x, bd)
    return x * lax.rsqrt(ms + EPS) * gain


def _rope(y, c, sa, sb):
    w = y.shape[-1]
    half = ROPE_DIMS // 2
    return y * c + pltpu.roll(y, w - half, 1) * sa + pltpu.roll(y, half, 1) * sb


def _rope_coeffs(positions, n_heads):
    inv_freq = jnp.power(ROPE_THETA, -jnp.arange(0, ROPE_DIMS, 2, dtype=F32) / ROPE_DIMS)
    ang = positions.astype(F32)[:, None] * inv_freq[None, :]
    cos, sin = jnp.cos(ang), jnp.sin(ang)
    p = positions.shape[0]
    rest = HEAD_DIM - ROPE_DIMS
    zeros_h = jnp.zeros((p, ROPE_DIMS // 2), F32)
    c = jnp.concatenate([cos, cos, jnp.ones((p, rest), F32)], axis=-1)
    sa = jnp.concatenate([-sin, zeros_h, jnp.zeros((p, rest), F32)], axis=-1)
    sb = jnp.concatenate([zeros_h, sin, jnp.zeros((p, rest), F32)], axis=-1)
    return tuple(jnp.tile(t, (1, n_heads)) for t in (c, sa, sb))


def _block_mean_matrix(width):
    idx = np.arange(width) // HEAD_DIM
    return jnp.asarray((idx[:, None] == idx[None, :]).astype(np.float32) / HEAD_DIM, dtype=BF16)


def _kv_prep_kernel(sel_ref, win_ref, bd_ref, gs_ref, gw_ref, c_ref, sa_ref, sb_ref, o_ref):
    w = GROUP_LANES
    bd = bd_ref[...]
    c, sa, sb = c_ref[...], sa_ref[...], sb_ref[...]
    ks = _rope(_head_rmsnorm(sel_ref[:, :w], bd, gs_ref[...]), c, sa, sb)
    kw = _rope(_head_rmsnorm(win_ref[:, :w], bd, gw_ref[...]), c, sa, sb)
    o_ref[:, 0 * w:1 * w] = ks.astype(BF16)
    o_ref[:, 1 * w:2 * w] = sel_ref[:, w:].astype(BF16)
    o_ref[:, 2 * w:3 * w] = kw.astype(BF16)
    o_ref[:, 3 * w:4 * w] = win_ref[:, w:].astype(BF16)


def _kv_prep(kv, k_norm, coeffs, S):
    T = kv.shape[0]
    w = GROUP_LANES
    tm = min(PROJ_TM, S)
    ns = S // tm
    gs = jnp.tile(k_norm[1], NSA_KV_HEADS).reshape(1, w)
    gw = jnp.tile(k_norm[2], NSA_KV_HEADS).reshape(1, w)
    tab = pl.BlockSpec((tm, w), lambda i: (i % ns, 0))
    one = pl.BlockSpec((1, w), lambda i: (0, 0))
    return pl.pallas_call(
        _kv_prep_kernel,
        grid=(T // tm,),
        in_specs=[
            pl.BlockSpec((tm, 2 * w), lambda i: (i, 1)),
            pl.BlockSpec((tm, 2 * w), lambda i: (i, 2)),
            pl.BlockSpec((w, w), lambda i: (0, 0)),
            one, one, tab, tab, tab,
        ],
        out_specs=pl.BlockSpec((tm, 4 * w), lambda i: (i, 0)),
        out_shape=jax.ShapeDtypeStruct((T, 4 * w), BF16),
        compiler_params=_params("parallel"),
        name="kv_prep",
    )(kv, kv, _block_mean_matrix(w), gs, gw, *coeffs)


def _compress_kernel(x_ref, pos_ref, w1_ref, w2_ref, bd_ref, g_ref, c_ref, sa_ref, sb_ref,
                     o_ref, top_ref, bot_ref, wbd_ref):
    t = pl.program_id(0)
    l = pl.program_id(1)
    half = CMP_BLOCK // 2
    hid = w1_ref.shape[-1]

    @pl.when(l == 0)
    def _():
        top_ref[...] = jnp.zeros_like(top_ref)
        bot_ref[...] = jnp.zeros_like(bot_ref)
        wbd_ref[...] = jnp.zeros_like(wbd_ref)

    for u in range(2):
        for g in range(NSA_KV_HEADS):
            wbd_ref[u, g * HEAD_DIM:(g + 1) * HEAD_DIM, g * hid:(g + 1) * hid] = w1_ref[0, u]

    x = x_ref[0]
    top_ref[...] += _dot((x + pos_ref[0, 0]).astype(BF16), wbd_ref[0])
    bot_ref[...] += _dot((x + pos_ref[0, 1]).astype(BF16), wbd_ref[1])

    @pl.when(l == half - 1)
    def _():
        rows = bot_ref.shape[0]
        hidden = top_ref[...] + pltpu.roll(bot_ref[...], rows - 1, 0)
        act = (hidden * _sigmoid(hidden)).astype(BF16)
        y = _dot(act, w2_ref[0])
        yk = _rope(_head_rmsnorm(y, bd_ref[...], g_ref[...]), c_ref[...], sa_ref[...], sb_ref[...])
        o_ref[0] = jnp.where(t == 0, yk, y).astype(BF16)


def _compress(kv, B, S, pos_k, pos_v, k_w1, k_w2, v_w1, v_w2, k_gain):
    w = GROUP_LANES
    G = NSA_KV_HEADS
    half = CMP_BLOCK // 2
    nc = S // CMP_STRIDE
    rows = B * nc
    hid = k_w1.shape[1]
    raw = kv[:, :2 * w].reshape(B, nc, half, 2, w)
    xc = jnp.transpose(raw, (3, 2, 0, 1, 4)).reshape(2, half, rows, w)
    eye = jnp.eye(G, dtype=F32)

    def blockdiag(wm):
        out = jnp.einsum('gh,...ab->...gahb', eye, wm)
        return out.reshape(*wm.shape[:-2], G * wm.shape[-2], G * wm.shape[-1])

    def prep_w1(w1):
        return jnp.transpose(w1.reshape(2, half, HEAD_DIM, hid), (1, 0, 2, 3))

    w1 = jnp.stack([prep_w1(k_w1), prep_w1(v_w1)]).astype(BF16)
    w2 = jnp.stack([blockdiag(k_w2), blockdiag(v_w2)]).astype(BF16)

    def prep_pos(p):
        p = jnp.tile(p, (1, G)).reshape(2, half, 1, w)
        return jnp.transpose(p, (1, 0, 2, 3))

    pos = jnp.stack([prep_pos(pos_k), prep_pos(pos_v)])
    c_end = jnp.arange(nc) * CMP_STRIDE + (CMP_BLOCK - 1)
    coeffs = tuple(jnp.tile(tb, (B, 1)) for tb in _rope_coeffs(c_end, G))
    gain = jnp.tile(k_gain, G).reshape(1, w)
    const2 = lambda t, l: (0, 0)
    out = pl.pallas_call(
        _compress_kernel,
        grid=(2, half),
        in_specs=[
            pl.BlockSpec((None, 1, rows, w), lambda t, l: (t, l, 0, 0)),
            pl.BlockSpec((None, 1, 2, 1, w), lambda t, l: (t, l, 0, 0, 0)),
            pl.BlockSpec((None, 1, 2, HEAD_DIM, hid), lambda t, l: (t, l, 0, 0, 0)),
            pl.BlockSpec((1, G * hid, w), lambda t, l: (t, 0, 0)),
            pl.BlockSpec((w, w), const2),
            pl.BlockSpec((1, w), const2),
            pl.BlockSpec((rows, w), const2),
            pl.BlockSpec((rows, w), const2),
            pl.BlockSpec((rows, w), const2),
        ],
        out_specs=pl.BlockSpec((1, rows, w), lambda t, l: (t, 0, 0)),
        out_shape=jax.ShapeDtypeStruct((2, rows, w), BF16),
        scratch_shapes=[pltpu.VMEM((rows, G * hid), F32), pltpu.VMEM((rows, G * hid), F32),
                        pltpu.VMEM((2, w, G * hid), BF16)],
        compiler_params=_params("arbitrary", "arbitrary"),
        name="nsa_compress",
    )(xc, pos, w1, w2, _block_mean_matrix(w), gain, *coeffs)
    return out[0], out[1]


MASK_BIG = 2.0 ** 30
V_ROWS = HEAD_DIM + 16


def _nsa_kernel(q_ref, gl_ref, kc_ref, vc_ref, ks_ref, vs_ref, kw_ref, vw_ref,
                bd_ref, qg_ref, c_ref, sa_ref, sb_ref, ov_ref,
                o_ref,
                kc_a, ks_a, kw_a, vct, vst, vwt, s_scr,
                *, tq, bk, nsb, scale, q_tiles):
    g = pl.program_id(1)
    w = GROUP_LANES
    R = NSA_GROUP
    rows = R * tq
    lane = _iota((1, w), 1)
    n_tiles = ks_a.shape[0] // bk
    ncp = kc_a.shape[0]

    def stage_kv():
        rr, cc = _iota((w, w), 0), _iota((w, w), 1)
        mine = ((rr >> 6) == g) & ((rr & 63) == (cc & 63))
        to_slot0 = jnp.where(mine & (cc < HEAD_DIM), 1.0, 0.0).astype(BF16)
        kc_a[...] = _dot(kc_ref[...], to_slot0).astype(BF16)
        kw_a[...] = _dot(kw_ref[...], to_slot0).astype(BF16)
        seq = ks_a.shape[0]
        tag = jnp.where(_iota((seq, w), 1) - HEAD_DIM == (_iota((seq, w), 0) >> 6), 1.0, 0.0)
        ks_a[...] = (_dot(ks_ref[...], to_slot0) + tag).astype(BF16)
        pick = jnp.where(_iota((V_ROWS, w), 1) == g * HEAD_DIM + _iota((V_ROWS, w), 0), 1.0, 0.0)
        pick = jnp.where(_iota((V_ROWS, w), 0) < HEAD_DIM, pick, 0.0).astype(BF16)
        vct[...] = _dot_nt(pick[0:HEAD_DIM], vc_ref[...]).astype(BF16)
        ones_row = jnp.where(_iota((V_ROWS, bk), 0) == HEAD_DIM, 1.0, 0.0)
        for j in range(n_tiles):
            vst[j] = (_dot_nt(pick, vs_ref[j * bk:(j + 1) * bk, :]) + ones_row).astype(BF16)
            vwt[j] = (_dot_nt(pick, vw_ref[j * bk:(j + 1) * bk, :]) + ones_row).astype(BF16)

    def lanes4(x):
        return jnp.concatenate([x] * R, axis=1)

    def part_max(x):
        return jnp.max(x.reshape(x.shape[0] // 8, 8, rows), axis=0)

    def tile_front(i):
        tok = slice(i * tq, (i + 1) * tq)
        qn = _rope(_head_rmsnorm(q_ref[tok, :], bd_ref[...], qg_ref[...]),
                   c_ref[tok, :], sa_ref[tok, :], sb_ref[tok, :]) * scale
        slot0 = lane < HEAD_DIM
        parts = []
        for r in range(R):
            qr = qn if r == 0 else pltpu.roll(qn, w - r * HEAD_DIM, 1)
            parts.append(jnp.where(slot0, qr, 0.0))
        qs_f32 = jnp.concatenate(parts, axis=0)
        qs = qs_f32.astype(BF16)
        t_col = i * tq + (_iota((1, rows), 1) & (tq - 1))
        key_le_query = _iota((bk, tq), 0) <= _iota((bk, tq), 1)
        diag_bias = lanes4(jnp.where(key_le_query, 0.0, NEG))

        s = _dot_nt(kc_a[...], qs)

        win_blk = [j for j in (i - 2, i - 1, i) if j >= 0]
        win_sc = []
        for j in win_blk:
            sc = _dot_nt(kw_a[j * bk:(j + 1) * bk, :], qs)
            if j == i - 2:
                sc = sc + lanes4(jnp.where(key_le_query, NEG, 0.0))
            elif j == i:
                sc = sc + diag_bias
            win_sc.append(sc)

        c_end = _iota((ncp, 1), 0) * CMP_STRIDE + (CMP_BLOCK - 1)
        s = jnp.where(c_end <= t_col, s, NEG)
        e = jnp.exp(s - jnp.max(s, axis=0, keepdims=True))
        p = e / jnp.sum(e, axis=0, keepdims=True)
        p = jnp.where(t_col >= CMP_BLOCK - 1, p, 0.0)
        o_cmp = _dot(vct[...], p.astype(BF16))
        p_sum = p[:, 0:tq]
        for r in range(1, R):
            p_sum = p_sum + p[:, r * tq:(r + 1) * tq]

        hi, lo = _split_bf16(p_sum)
        ov = ov_ref[...]
        imp = (_dot(ov, hi) + _dot(ov, lo))[0:nsb]
        jb = _iota((nsb, 1), 0)
        cur = (i * tq + _iota((1, tq), 1)) >> 6
        forced = (jb == 0) | ((cur - jb >= 0) & (cur - jb < N_LOCAL_SEL))
        score = jnp.where(forced, FORCE, jnp.where(jb <= cur, imp, -FORCE))
        cnt = jnp.zeros((nsb, tq), F32)
        sub8 = _iota((8, 1), 0)
        for i2 in range(nsb):
            row = score[i2:i2 + 1, :]
            grp = []
            for r in range(nsb // 8):
                blk = score[8 * r:8 * r + 8]
                gt = jnp.where(row > blk, 1.0, 0.0)
                if 8 * r + 7 <= i2:
                    grp.append(gt)
                elif 8 * r > i2:
                    grp.append(jnp.where(row >= blk, 1.0, 0.0))
                else:
                    grp.append(jnp.where(sub8 > i2 - 8 * r, jnp.where(row >= blk, 1.0, 0.0), gt))
            cnt = cnt + jnp.concatenate(grp, axis=0)
        sel_t = jnp.where((cnt < float(min(N_SEL, nsb))) & (jb <= cur), 0.0, -MASK_BIG)
        pad_t = jnp.concatenate([jnp.zeros((HEAD_DIM, tq), F32), sel_t,
                                 jnp.zeros((LANES - HEAD_DIM - nsb, tq), F32)], axis=0)
        aug = jnp.concatenate([pad_t.T, jnp.zeros((tq, w - LANES), F32)], axis=1)
        qsel = (qs_f32 + jnp.concatenate([aug] * R, axis=0)).astype(BF16)
        return dict(tok=tok, qsel=qsel, win_sc=win_sc, win_blk=win_blk, o_cmp=o_cmp, diag_bias=diag_bias)

    def selected_scores(i, c):
        mx8 = None
        for x in range(i + 1):
            sc = _dot_nt(ks_a[x * bk:(x + 1) * bk, :], c["qsel"])
            if x == i:
                sc = sc + c["diag_bias"]
            s_scr[x] = sc
            mx8 = part_max(sc) if mx8 is None else jnp.maximum(mx8, part_max(sc))
        return jnp.max(mx8, axis=0, keepdims=True)

    def finish(i, c, m_s):
        tok, win_sc, win_blk, o_cmp = c["tok"], c["win_sc"], c["win_blk"], c["o_cmp"]
        m_w = win_sc[0]
        for sc in win_sc[1:]:
            m_w = jnp.maximum(m_w, sc)
        m_w = jnp.max(m_w, axis=0, keepdims=True)
        p_cat = jnp.concatenate([jnp.exp(sc - m_w).astype(BF16) for sc in win_sc], axis=0)
        v_cat = jnp.concatenate([vwt[j] for j in win_blk], axis=1)
        acc_w = _dot(v_cat, p_cat)
        o_win = acc_w[0:HEAD_DIM] / acc_w[HEAD_DIM:HEAD_DIM + 1]

        acc_s = None
        for x0 in range(0, i + 1, 2):
            xs = [x for x in (x0, x0 + 1) if x <= i]
            p2 = jnp.concatenate([jnp.exp(s_scr[x] - m_s).astype(BF16) for x in xs], axis=0)
            v2 = jnp.concatenate([vst[x] for x in xs], axis=1)
            d = _dot(v2, p2)
            acc_s = d if acc_s is None else acc_s + d
        o_sel = acc_s[0:HEAD_DIM] / acc_s[HEAD_DIM:HEAD_DIM + 1]

        def to_rows(o_t):
            return jnp.concatenate([o_t[:, r * tq:(r + 1) * tq] for r in range(R)], axis=0)

        ghi, glo = _split_bf16(_sigmoid(gl_ref[tok, :]))
        row_head = _iota((w, LANES), 0) >> 6
        col = _iota((w, LANES), 1)
        out_t = jnp.zeros((w, tq), F32)
        for br, o_br in enumerate((o_cmp, o_sel, o_win)):
            spread = jnp.where(col == br * (NSA_KV_HEADS * R) + g * R + row_head, 1.0, 0.0).astype(BF16)
            out_t = out_t + (_dot_nt(spread, ghi) + _dot_nt(spread, glo)) * to_rows(o_br)
        o_ref[tok, :] = out_t.T.astype(o_ref.dtype)

    stage_kv()
    ctx = tile_front(0)
    for n in range(q_tiles):
        m_s = selected_scores(n, ctx)
        nxt = tile_front(n + 1) if n + 1 < q_tiles else None
        finish(n, ctx, m_s)
        ctx = nxt


def _nsa_attention(proj, kc, vc, ksw, q_gain, coeffs, B, S):
    T = B * S
    w = GROUP_LANES
    G = NSA_KV_HEADS
    tq, bk = ATT_TQ, ATT_BK
    nq = S // tq
    nc = S // CMP_STRIDE
    nsb = S // SEL_BLOCK
    n_cmp = (S - CMP_BLOCK) // CMP_STRIDE + 1
    cs = np.arange(nc) * CMP_STRIDE
    js = np.arange(LANES) * SEL_BLOCK
    ov = ((cs[None, :] < js[:, None] + SEL_BLOCK) & (cs[None, :] + CMP_BLOCK > js[:, None])
          & (np.arange(nc)[None, :] < n_cmp) & (np.arange(LANES)[:, None] < nsb))
    ov = jnp.asarray(ov.astype(np.float32), dtype=BF16)
    gate_blk = (G * NSA_GROUP * HEAD_DIM) // LANES
    assert tq == bk and WINDOW == 2 * bk and HEAD_DIM + nsb <= LANES
    kern = functools.partial(_nsa_kernel, tq=tq, bk=bk, nsb=nsb, scale=HEAD_DIM ** -0.5, q_tiles=nq)
    seq = lambda col: pl.BlockSpec((S, w), lambda b, g: (b, col))
    const = lambda shape: pl.BlockSpec(shape, lambda b, g: (0, 0), pipeline_mode=pl.Buffered(1))
    rows = NSA_GROUP * tq
    n_tiles = S // bk
    return pl.pallas_call(
        kern,
        grid=(B, G),
        in_specs=[
            pl.BlockSpec((S, w), lambda b, g: (b, g)),
            pl.BlockSpec((S, LANES), lambda b, g: (b, gate_blk)),
            pl.BlockSpec((nc, w), lambda b, g: (b, 0)),
            pl.BlockSpec((nc, w), lambda b, g: (b, 0)),
            seq(0), seq(1), seq(2), seq(3),
            const((w, w)), const((1, w)), const((S, w)), const((S, w)), const((S, w)), const((LANES, nc)),
        ],
        out_specs=pl.BlockSpec((S, w), lambda b, g: (b, g)),
        out_shape=jax.ShapeDtypeStruct((T, G * w), BF16),
        scratch_shapes=[
            pltpu.VMEM((nc, w), BF16), pltpu.VMEM((S, w), BF16), pltpu.VMEM((S, w), BF16),
            pltpu.VMEM((HEAD_DIM, nc), BF16),
            pltpu.VMEM((n_tiles, V_ROWS, bk), BF16), pltpu.VMEM((n_tiles, V_ROWS, bk), BF16),
            pltpu.VMEM((n_tiles, bk, rows), F32),
        ],
        compiler_params=_params("parallel", "parallel"),
        name="nsa_attn",
    )(proj, proj, kc, vc, ksw, ksw, ksw, ksw,
      _block_mean_matrix(w), jnp.tile(q_gain, NSA_GROUP).reshape(1, w), *coeffs, ov)


def kernel(x, ffn1_norm, ffn1_w_in, ffn1_w_out, mix_norm, ffn2_norm, ffn2_w_in, ffn2_w_out,
           sb_w_qkv, sb_w_out, kv_norm, nsa_w_kv, nsa_k_norm, cmp_pos_k, cmp_pos_v,
           cmp_k_w1, cmp_k_w2, cmp_v_w1, cmp_v_w2, nsa_w_in, nsa_q_norm, nsa_w_out):
    B, S, D = x.shape
    depth = ffn1_norm.shape[0]
    n_a = sb_w_qkv.shape[0]
    n_sb_heads = sb_w_out.shape[1] // HEAD_DIM
    assert S % 512 == 0 and S // CMP_STRIDE == LANES, "sequence tiling assumes 16*128 tokens"
    h = x.reshape(B * S, D)
    bf = lambda a: a.astype(BF16)
    coeffs = _rope_coeffs(jnp.arange(S), NSA_KV_HEADS)
    n_q = nsa_w_in.shape[-1]
    n_q_pad = -(-n_q // LANES) * LANES
    kc = vc = ksw = None
    w1_in, w1_out, w2_in, w2_out = bf(ffn1_w_in), bf(ffn1_w_out), bf(ffn2_w_in), bf(ffn2_w_out)
    w_qkv, w_sb_out, w_nsa_out = bf(sb_w_qkv), bf(sb_w_out), bf(nsa_w_out)
    w_nsa_in = jnp.pad(bf(nsa_w_in), ((0, 0), (0, 0), (0, n_q_pad - n_q)))
    w_kv = bf(nsa_w_kv)[None]
    for layer in range(depth):
        h = _ffn(h, ffn1_norm[layer], w1_in, w1_out, layer)
        if layer < n_a:
            qkv = _norm_proj(h, mix_norm[layer], w_qkv, layer, 1024, BF16)
            o = _sb_attention(qkv, B, S, n_sb_heads)
            h = _out_proj(h, o, w_sb_out, layer)
        else:
            li = layer - n_a
            proj = _norm_proj(h, mix_norm[layer], w_nsa_in, li, n_q_pad, F32)
            o = _nsa_attention(proj, kc, vc, ksw, nsa_q_norm[li], coeffs, B, S)
            h = _out_proj(h, o, w_nsa_out, li)
        h = _ffn(h, ffn2_norm[layer], w2_in, w2_out, layer)
        if layer == n_a - 1:
            kv = _norm_proj(h, kv_norm, w_kv, 0, 768, F32)
            kc, vc = _compress(kv, B, S, cmp_pos_k, cmp_pos_v, cmp_k_w1, cmp_k_w2,
                               cmp_v_w1, cmp_v_w2, nsa_k_norm[0])
            ksw = _kv_prep(kv, nsa_k_norm, coeffs, S)
    return h.reshape(B, S, D)
```

```python
import functools

import jax
import jax.numpy as jnp
import numpy as np
from jax import lax
from jax.experimental import pallas as pl
from jax.experimental.pallas import tpu as pltpu

F32 = jnp.float32
BF16 = jnp.bfloat16

HEAD_DIM = 64
NSA_KV_HEADS = 4
NSA_GROUP = 4
ROPE_DIMS = 16
ROPE_THETA = 500000.0
CMP_BLOCK = 32
CMP_STRIDE = 16
SEL_BLOCK = 64
N_SEL = 8
N_LOCAL_SEL = 2
WINDOW = 512
EPS = 1e-6
LOG2E = 1.4426950408889634
NEG = -1e30
FORCE = 1e4

LANES = 128
MXU_TILE = 256
GROUP_LANES = NSA_GROUP * HEAD_DIM
VMEM_LIMIT_BYTES = 56 * 1024 * 1024

FFN_TM = 512
PROJ_TM = 1024
ATT_TQ = 256
ATT_BK = 256


def _dot(a, b):
    return jnp.dot(a, b, preferred_element_type=F32)


def _dot_nt(a, b):
    return lax.dot_general(a, b, (((1,), (1,)), ((), ())), preferred_element_type=F32)


def _split_bf16(x):
    hi = x.astype(BF16)
    lo = (x - hi.astype(F32)).astype(BF16)
    return hi, lo


def _dot_split(x, w):
    hi, lo = _split_bf16(x)
    n = x.shape[0]
    r = _dot(jnp.concatenate([hi, lo], axis=0), w)
    return r[0:n] + r[n:2 * n]


def _sigmoid(x):
    return 1.0 / (1.0 + jnp.exp(-x))


def _iota(shape, dim):
    return lax.broadcasted_iota(jnp.int32, shape, dim)


def _params(*sem):
    return pltpu.CompilerParams(dimension_semantics=sem, vmem_limit_bytes=VMEM_LIMIT_BYTES)


def _ffn_kernel(h_ref, g_ref, wi_ref, wo_ref, o_ref, *, chunks):
    x = h_ref[...]
    ms = jnp.mean(x * x, axis=-1, keepdims=True)
    xn = (x * lax.rsqrt(ms + EPS) * g_ref[...]).astype(BF16)
    ff = chunks[-1][1]
    ab = [(_dot(xn, wi_ref[:, lo:hi]), _dot(xn, wi_ref[:, ff + lo:ff + hi])) for lo, hi in chunks]
    acc = None
    for (lo, hi), (a, b) in zip(chunks, ab):
        act = (a * _sigmoid(a) * b).astype(BF16)
        part = _dot(act, wo_ref[lo:hi, :])
        acc = part if acc is None else acc + part
    o_ref[...] = x + 0.5 * acc


def _ffn(h, gain, w_in, w_out, layer):
    T, D = h.shape
    F = w_out.shape[1]
    tm = min(FFN_TM, T)
    n_mxu = F // MXU_TILE
    split = -(-n_mxu // 2) * MXU_TILE
    chunks = ((0, split), (split, F)) if F % MXU_TILE == 0 and n_mxu > 1 else ((0, F),)
    resident = dict(pipeline_mode=pl.Buffered(1))
    return pl.pallas_call(
        functools.partial(_ffn_kernel, chunks=chunks),
        grid=(T // tm,),
        in_specs=[
            pl.BlockSpec((tm, D), lambda i: (i, 0)),
            pl.BlockSpec((1, D), lambda i: (0, 0)),
            pl.BlockSpec((None, D, 2 * F), lambda i: (layer, 0, 0), **resident),
            pl.BlockSpec((None, F, D), lambda i: (layer, 0, 0), **resident),
        ],
        out_specs=pl.BlockSpec((tm, D), lambda i: (i, 0)),
        out_shape=jax.ShapeDtypeStruct((T, D), F32),
        compiler_params=_params("parallel"),
        name="ffn",
    )(h, gain.reshape(1, D), w_in, w_out)


def _norm_proj_kernel(h_ref, g_ref, w_ref, o_ref, xn_ref):
    @pl.when(pl.program_id(1) == 0)
    def _():
        x = h_ref[...]
        ms = jnp.mean(x * x, axis=-1, keepdims=True)
        xn_ref[...] = (x * lax.rsqrt(ms + EPS) * g_ref[...]).astype(BF16)

    o_ref[...] = _dot(xn_ref[...], w_ref[...]).astype(o_ref.dtype)


def _norm_proj(h, gain, w, layer, tn, out_dtype):
    T, D = h.shape
    N = w.shape[2]
    tm = min(PROJ_TM, T)
    return pl.pallas_call(
        _norm_proj_kernel,
        grid=(T // tm, N // tn),
        in_specs=[
            pl.BlockSpec((tm, D), lambda i, j: (i, 0)),
            pl.BlockSpec((1, D), lambda i, j: (0, 0)),
            pl.BlockSpec((None, D, tn), lambda i, j: (layer, 0, j)),
        ],
        out_specs=pl.BlockSpec((tm, tn), lambda i, j: (i, j)),
        out_shape=jax.ShapeDtypeStruct((T, N), out_dtype),
        scratch_shapes=[pltpu.VMEM((tm, D), BF16)],
        compiler_params=_params("parallel", "arbitrary"),
        name="norm_proj",
    )(h, gain.reshape(1, D), w)


def _out_proj_kernel(h_ref, x_ref, w_ref, o_ref):
    o_ref[...] = h_ref[...] + _dot(x_ref[...], w_ref[...])


def _out_proj(h, x, w, layer):
    T, D = h.shape
    K = x.shape[1]
    tm = min(PROJ_TM, T)
    return pl.pallas_call(
        _out_proj_kernel,
        grid=(T // tm,),
        in_specs=[
            pl.BlockSpec((tm, D), lambda i: (i, 0)),
            pl.BlockSpec((tm, K), lambda i: (i, 0)),
            pl.BlockSpec((None, K, D), lambda i: (layer, 0, 0)),
        ],
        out_specs=pl.BlockSpec((tm, D), lambda i: (i, 0)),
        out_shape=jax.ShapeDtypeStruct((T, D), F32),
        compiler_params=_params("parallel"),
        name="out_proj",
    )(h, x, w)


def _sb_kernel(q_ref, k_ref, v_ref, o_ref, *, tq, bk, scale, q_tiles):
    lane = _iota((1, LANES), 1)
    is_a = lane < HEAD_DIM
    tri = jnp.where(_iota((bk, bk), 0) > _iota((bk, bk), 1), 1.0, 0.0).astype(BF16)
    past = _iota((tq, bk), 1) < _iota((tq, bk), 0)

    chains = [(t, j, hd) for t in range(q_tiles) for j in range(t, -1, -1) for hd in range(2)]
    n = len(chains)
    qh, carry, acc = {}, {}, {}
    z, log_b, log_1m, within = {}, {}, {}, {}
    for k in range(n + 2):
        if k < n:
            t, j, hd = chains[k]
            if t not in qh:
                q = q_ref[t * tq:(t + 1) * tq, :] * scale
                zero = jnp.zeros_like(q)
                qh[t] = (jnp.where(is_a, q, zero), jnp.where(is_a, zero, q))
                carry[t] = [jnp.zeros((tq, 1), F32), jnp.zeros((tq, 1), F32)]
                acc[t] = [jnp.zeros((tq, LANES), F32), jnp.zeros((tq, LANES), F32)]
            z[k] = _dot_nt(qh[t][hd], k_ref[j * bk:(j + 1) * bk, :])
        if 0 <= k - 1 < n:
            c = k - 1
            t, j, hd = chains[c]
            e = jnp.exp2(jnp.abs(z[c]) * (-LOG2E))
            log_b[c] = jnp.minimum(z[c], 0.0) - jnp.log(1.0 + e)
            m = log_b[c] - z.pop(c)
            log_1m[c] = jnp.where(past, m, 0.0) if j == t else m
            within[c] = _dot(log_1m[c].astype(BF16), tri)
        if 0 <= k - 2 < n:
            c = k - 2
            t, j, hd = chains[c]
            wc = jnp.exp(log_b.pop(c) + (carry[t][hd] + within.pop(c)))
            if j == t:
                wc = jnp.where(past, wc, 0.0)
            carry[t][hd] = carry[t][hd] + jnp.sum(log_1m.pop(c), axis=1, keepdims=True)
            acc[t][hd] = acc[t][hd] + _dot(wc.astype(BF16), v_ref[j * bk:(j + 1) * bk, :])
            if j == 0 and hd == 1:
                o_ref[t * tq:(t + 1) * tq, :] = jnp.where(is_a, acc[t][0], acc[t][1]).astype(o_ref.dtype)
                del qh[t], carry[t], acc[t]


def _sb_attention(qkv, B, S, n_heads):
    T = B * S
    tq, bk = ATT_TQ, ATT_BK
    nq = S // tq
    n_pairs = n_heads * HEAD_DIM // LANES
    kern = functools.partial(_sb_kernel, tq=tq, bk=bk, scale=HEAD_DIM ** -0.5, q_tiles=nq)
    return pl.pallas_call(
        kern,
        grid=(B, n_pairs),
        in_specs=[
            pl.BlockSpec((S, LANES), lambda b, p: (b, p)),
            pl.BlockSpec((S, LANES), lambda b, p: (b, n_pairs + p)),
            pl.BlockSpec((S, LANES), lambda b, p: (b, 2 * n_pairs + p)),
        ],
        out_specs=pl.BlockSpec((S, LANES), lambda b, p: (b, p)),
        out_shape=jax.ShapeDtypeStruct((T, n_heads * HEAD_DIM), BF16),
        compiler_params=_params("parallel", "parallel"),
        name="sb_attn",
    )(qkv, qkv, qkv)


def _head_rmsnorm(x, bd, gain):
    ms = _dot_split(x * x, bd)
    return x * lax.rsqrt(ms + EPS) * gain


def _rope(y, c, sa, sb):
    w = y.shape[-1]
    half = ROPE_DIMS // 2
    return y * c + pltpu.roll(y, w - half, 1) * sa + pltpu.roll(y, half, 1) * sb


def _rope_coeffs(positions, n_heads):
    inv_freq = jnp.power(ROPE_THETA, -jnp.arange(0, ROPE_DIMS, 2, dtype=F32) / ROPE_DIMS)
    ang = positions.astype(F32)[:, None] * inv_freq[None, :]
    cos, sin = jnp.cos(ang), jnp.sin(ang)
    p = positions.shape[0]
    rest = HEAD_DIM - ROPE_DIMS
    zeros_h = jnp.zeros((p, ROPE_DIMS // 2), F32)
    c = jnp.concatenate([cos, cos, jnp.ones((p, rest), F32)], axis=-1)
    sa = jnp.concatenate([-sin, zeros_h, jnp.zeros((p, rest), F32)], axis=-1)
    sb = jnp.concatenate([zeros_h, sin, jnp.zeros((p, rest), F32)], axis=-1)
    return tuple(jnp.tile(t, (1, n_heads)) for t in (c, sa, sb))


def _block_mean_matrix(width):
    idx = np.arange(width) // HEAD_DIM
    return jnp.asarray((idx[:, None] == idx[None, :]).astype(np.float32) / HEAD_DIM, dtype=BF16)


def _kv_prep_kernel(sel_ref, win_ref, bd_ref, gs_ref, gw_ref, c_ref, sa_ref, sb_ref, o_ref):
    w = GROUP_LANES
    bd = bd_ref[...]
    c, sa, sb = c_ref[...], sa_ref[...], sb_ref[...]
    ks = _rope(_head_rmsnorm(sel_ref[:, :w], bd, gs_ref[...]), c, sa, sb)
    kw = _rope(_head_rmsnorm(win_ref[:, :w], bd, gw_ref[...]), c, sa, sb)
    o_ref[:, 0 * w:1 * w] = ks.astype(BF16)
    o_ref[:, 1 * w:2 * w] = sel_ref[:, w:].astype(BF16)
    o_ref[:, 2 * w:3 * w] = kw.astype(BF16)
    o_ref[:, 3 * w:4 * w] = win_ref[:, w:].astype(BF16)


def _kv_prep(kv, k_norm, coeffs, S):
    T = kv.shape[0]
    w = GROUP_LANES
    tm = min(PROJ_TM, S)
    ns = S // tm
    gs = jnp.tile(k_norm[1], NSA_KV_HEADS).reshape(1, w)
    gw = jnp.tile(k_norm[2], NSA_KV_HEADS).reshape(1, w)
    tab = pl.BlockSpec((tm, w), lambda i: (i % ns, 0))
    one = pl.BlockSpec((1, w), lambda i: (0, 0))
    return pl.pallas_call(
        _kv_prep_kernel,
        grid=(T // tm,),
        in_specs=[
            pl.BlockSpec((tm, 2 * w), lambda i: (i, 1)),
            pl.BlockSpec((tm, 2 * w), lambda i: (i, 2)),
            pl.BlockSpec((w, w), lambda i: (0, 0)),
            one, one, tab, tab, tab,
        ],
        out_specs=pl.BlockSpec((tm, 4 * w), lambda i: (i, 0)),
        out_shape=jax.ShapeDtypeStruct((T, 4 * w), BF16),
        compiler_params=_params("parallel"),
        name="kv_prep",
    )(kv, kv, _block_mean_matrix(w), gs, gw, *coeffs)


def _compress_kernel(x_ref, pos_ref, w1_ref, w2_ref, bd_ref, g_ref, c_ref, sa_ref, sb_ref,
                     o_ref, top_ref, bot_ref, wbd_ref):
    t = pl.program_id(0)
    l = pl.program_id(1)
    half = CMP_BLOCK // 2
    hid = w1_ref.shape[-1]

    @pl.when(l == 0)
    def _():
        top_ref[...] = jnp.zeros_like(top_ref)
        bot_ref[...] = jnp.zeros_like(bot_ref)
        wbd_ref[...] = jnp.zeros_like(wbd_ref)

    for u in range(2):
        for g in range(NSA_KV_HEADS):
            wbd_ref[u, g * HEAD_DIM:(g + 1) * HEAD_DIM, g * hid:(g + 1) * hid] = w1_ref[0, u]

    x = x_ref[0]
    top_ref[...] += _dot((x + pos_ref[0, 0]).astype(BF16), wbd_ref[0])
    bot_ref[...] += _dot((x + pos_ref[0, 1]).astype(BF16), wbd_ref[1])

    @pl.when(l == half - 1)
    def _():
        rows = bot_ref.shape[0]
        hidden = top_ref[...] + pltpu.roll(bot_ref[...], rows - 1, 0)
        act = (hidden * _sigmoid(hidden)).astype(BF16)
        y = _dot(act, w2_ref[0])
        yk = _rope(_head_rmsnorm(y, bd_ref[...], g_ref[...]), c_ref[...], sa_ref[...], sb_ref[...])
        o_ref[0] = jnp.where(t == 0, yk, y).astype(BF16)


def _compress(kv, B, S, pos_k, pos_v, k_w1, k_w2, v_w1, v_w2, k_gain):
    w = GROUP_LANES
    G = NSA_KV_HEADS
    half = CMP_BLOCK // 2
    nc = S // CMP_STRIDE
    rows = B * nc
    hid = k_w1.shape[1]
    raw = kv[:, :2 * w].reshape(B, nc, half, 2, w)
    xc = jnp.transpose(raw, (3, 2, 0, 1, 4)).reshape(2, half, rows, w)
    eye = jnp.eye(G, dtype=F32)

    def blockdiag(wm):
        out = jnp.einsum('gh,...ab->...gahb', eye, wm)
        return out.reshape(*wm.shape[:-2], G * wm.shape[-2], G * wm.shape[-1])

    def prep_w1(w1):
        return jnp.transpose(w1.reshape(2, half, HEAD_DIM, hid), (1, 0, 2, 3))

    w1 = jnp.stack([prep_w1(k_w1), prep_w1(v_w1)]).astype(BF16)
    w2 = jnp.stack([blockdiag(k_w2), blockdiag(v_w2)]).astype(BF16)

    def prep_pos(p):
        p = jnp.tile(p, (1, G)).reshape(2, half, 1, w)
        return jnp.transpose(p, (1, 0, 2, 3))

    pos = jnp.stack([prep_pos(pos_k), prep_pos(pos_v)])
    c_end = jnp.arange(nc) * CMP_STRIDE + (CMP_BLOCK - 1)
    coeffs = tuple(jnp.tile(tb, (B, 1)) for tb in _rope_coeffs(c_end, G))
    gain = jnp.tile(k_gain, G).reshape(1, w)
    const2 = lambda t, l: (0, 0)
    out = pl.pallas_call(
        _compress_kernel,
        grid=(2, half),
        in_specs=[
            pl.BlockSpec((None, 1, rows, w), lambda t, l: (t, l, 0, 0)),
            pl.BlockSpec((None, 1, 2, 1, w), lambda t, l: (t, l, 0, 0, 0)),
            pl.BlockSpec((None, 1, 2, HEAD_DIM, hid), lambda t, l: (t, l, 0, 0, 0)),
            pl.BlockSpec((1, G * hid, w), lambda t, l: (t, 0, 0)),
            pl.BlockSpec((w, w), const2),
            pl.BlockSpec((1, w), const2),
            pl.BlockSpec((rows, w), const2),
            pl.BlockSpec((rows, w), const2),
            pl.BlockSpec((rows, w), const2),
        ],
        out_specs=pl.BlockSpec((1, rows, w), lambda t, l: (t, 0, 0)),
        out_shape=jax.ShapeDtypeStruct((2, rows, w), BF16),
        scratch_shapes=[pltpu.VMEM((rows, G * hid), F32), pltpu.VMEM((rows, G * hid), F32),
                        pltpu.VMEM((2, w, G * hid), BF16)],
        compiler_params=_params("arbitrary", "arbitrary"),
        name="nsa_compress",
    )(xc, pos, w1, w2, _block_mean_matrix(w), gain, *coeffs)
    return out[0], out[1]


MASK_BIG = 2.0 ** 30
V_ROWS = HEAD_DIM + 16


def _nsa_kernel(q_ref, gl_ref, kc_ref, vc_ref, ks_ref, vs_ref, kw_ref, vw_ref,
                bd_ref, qg_ref, c_ref, sa_ref, sb_ref, ov_ref,
                o_ref,
                kc_a, ks_a, kw_a, vct, vst, vwt, s_scr,
                *, tq, bk, nsb, scale, q_tiles):
    g = pl.program_id(1)
    w = GROUP_LANES
    R = NSA_GROUP
    rows = R * tq
    lane = _iota((1, w), 1)
    n_tiles = ks_a.shape[0] // bk
    ncp = kc_a.shape[0]

    def stage_kv():
        rr, cc = _iota((w, w), 0), _iota((w, w), 1)
        mine = ((rr >> 6) == g) & ((rr & 63) == (cc & 63))
        to_slot0 = jnp.where(mine & (cc < HEAD_DIM), 1.0, 0.0).astype(BF16)
        kc_a[...] = _dot(kc_ref[...], to_slot0).astype(BF16)
        kw_a[...] = _dot(kw_ref[...], to_slot0).astype(BF16)
        seq = ks_a.shape[0]
        tag = jnp.where(_iota((seq, w), 1) - HEAD_DIM == (_iota((seq, w), 0) >> 6), 1.0, 0.0)
        ks_a[...] = (_dot(ks_ref[...], to_slot0) + tag).astype(BF16)
        pick = jnp.where(_iota((V_ROWS, w), 1) == g * HEAD_DIM + _iota((V_ROWS, w), 0), 1.0, 0.0)
        pick = jnp.where(_iota((V_ROWS, w), 0) < HEAD_DIM, pick, 0.0).astype(BF16)
        vct[...] = _dot_nt(pick[0:HEAD_DIM], vc_ref[...]).astype(BF16)
        ones_row = jnp.where(_iota((V_ROWS, bk), 0) == HEAD_DIM, 1.0, 0.0)
        for j in range(n_tiles):
            vst[j] = (_dot_nt(pick, vs_ref[j * bk:(j + 1) * bk, :]) + ones_row).astype(BF16)
            vwt[j] = (_dot_nt(pick, vw_ref[j * bk:(j + 1) * bk, :]) + ones_row).astype(BF16)

    def lanes4(x):
        return jnp.concatenate([x] * R, axis=1)

    def part_max(x):
        return jnp.max(x.reshape(x.shape[0] // 8, 8, rows), axis=0)

    def tile_front(i):
        tok = slice(i * tq, (i + 1) * tq)
        qn = _rope(_head_rmsnorm(q_ref[tok, :], bd_ref[...], qg_ref[...]),
                   c_ref[tok, :], sa_ref[tok, :], sb_ref[tok, :]) * scale
        slot0 = lane < HEAD_DIM
        parts = []
        for r in range(R):
            qr = qn if r == 0 else pltpu.roll(qn, w - r * HEAD_DIM, 1)
            parts.append(jnp.where(slot0, qr, 0.0))
        qs_f32 = jnp.concatenate(parts, axis=0)
        qs = qs_f32.astype(BF16)
        t_col = i * tq + (_iota((1, rows), 1) & (tq - 1))
        key_le_query = _iota((bk, tq), 0) <= _iota((bk, tq), 1)
        diag_bias = lanes4(jnp.where(key_le_query, 0.0, NEG))

        s = _dot_nt(kc_a[...], qs)

        win_blk = [j for j in (i - 2, i - 1, i) if j >= 0]
        win_sc = []
        for j in win_blk:
            sc = _dot_nt(kw_a[j * bk:(j + 1) * bk, :], qs)
            if j == i - 2:
                sc = sc + lanes4(jnp.where(key_le_query, NEG, 0.0))
            elif j == i:
                sc = sc + diag_bias
            win_sc.append(sc)

        c_end = _iota((ncp, 1), 0) * CMP_STRIDE + (CMP_BLOCK - 1)
        s = jnp.where(c_end <= t_col, s, NEG)
        e = jnp.exp(s - jnp.max(s, axis=0, keepdims=True))
        p = e / jnp.sum(e, axis=0, keepdims=True)
        p = jnp.where(t_col >= CMP_BLOCK - 1, p, 0.0)
        o_cmp = _dot(vct[...], p.astype(BF16))
        p_sum = p[:, 0:tq]
        for r in range(1, R):
            p_sum = p_sum + p[:, r * tq:(r + 1) * tq]

        hi, lo = _split_bf16(p_sum)
        ov = ov_ref[...]
        imp = (_dot(ov, hi) + _dot(ov, lo))[0:nsb]
        jb = _iota((nsb, 1), 0)
        cur = (i * tq + _iota((1, tq), 1)) >> 6
        forced = (jb == 0) | ((cur - jb >= 0) & (cur - jb < N_LOCAL_SEL))
        score = jnp.where(forced, FORCE, jnp.where(jb <= cur, imp, -FORCE))
        cnt = jnp.zeros((nsb, tq), F32)
        sub8 = _iota((8, 1), 0)
        for i2 in range(nsb):
            row = score[i2:i2 + 1, :]
            grp = []
            for r in range(nsb // 8):
                blk = score[8 * r:8 * r + 8]
                gt = jnp.where(row > blk, 1.0, 0.0)
                if 8 * r + 7 <= i2:
                    grp.append(gt)
                elif 8 * r > i2:
                    grp.append(jnp.where(row >= blk, 1.0, 0.0))
                else:
                    grp.append(jnp.where(sub8 > i2 - 8 * r, jnp.where(row >= blk, 1.0, 0.0), gt))
            cnt = cnt + jnp.concatenate(grp, axis=0)
        sel_t = jnp.where((cnt < float(min(N_SEL, nsb))) & (jb <= cur), 0.0, -MASK_BIG)
        pad_t = jnp.concatenate([jnp.zeros((HEAD_DIM, tq), F32), sel_t,
                                 jnp.zeros((LANES - HEAD_DIM - nsb, tq), F32)], axis=0)
        aug = jnp.concatenate([pad_t.T, jnp.zeros((tq, w - LANES), F32)], axis=1)
        qsel = (qs_f32 + jnp.concatenate([aug] * R, axis=0)).astype(BF16)
        return dict(tok=tok, qsel=qsel, win_sc=win_sc, win_blk=win_blk, o_cmp=o_cmp, diag_bias=diag_bias)

    def selected_scores(i, c):
        mx8 = None
        for x in range(i + 1):
            sc = _dot_nt(ks_a[x * bk:(x + 1) * bk, :], c["qsel"])
            if x == i:
                sc = sc + c["diag_bias"]
            s_scr[x] = sc
            mx8 = part_max(sc) if mx8 is None else jnp.maximum(mx8, part_max(sc))
        return jnp.max(mx8, axis=0, keepdims=True)

    def finish(i, c, m_s):
        tok, win_sc, win_blk, o_cmp = c["tok"], c["win_sc"], c["win_blk"], c["o_cmp"]
        m_w = win_sc[0]
        for sc in win_sc[1:]:
            m_w = jnp.maximum(m_w, sc)
        m_w = jnp.max(m_w, axis=0, keepdims=True)
        p_cat = jnp.concatenate([jnp.exp(sc - m_w).astype(BF16) for sc in win_sc], axis=0)
        v_cat = jnp.concatenate([vwt[j] for j in win_blk], axis=1)
        acc_w = _dot(v_cat, p_cat)
        o_win = acc_w[0:HEAD_DIM] / acc_w[HEAD_DIM:HEAD_DIM + 1]

        acc_s = None
        for x0 in range(0, i + 1, 2):
            xs = [x for x in (x0, x0 + 1) if x <= i]
            p2 = jnp.concatenate([jnp.exp(s_scr[x] - m_s).astype(BF16) for x in xs], axis=0)
            v2 = jnp.concatenate([vst[x] for x in xs], axis=1)
            d = _dot(v2, p2)
            acc_s = d if acc_s is None else acc_s + d
        o_sel = acc_s[0:HEAD_DIM] / acc_s[HEAD_DIM:HEAD_DIM + 1]

        def to_rows(o_t):
            return jnp.concatenate([o_t[:, r * tq:(r + 1) * tq] for r in range(R)], axis=0)

        ghi, glo = _split_bf16(_sigmoid(gl_ref[tok, :]))
        row_head = _iota((w, LANES), 0) >> 6
        col = _iota((w, LANES), 1)
        out_t = jnp.zeros((w, tq), F32)
        for br, o_br in enumerate((o_cmp, o_sel, o_win)):
            spread = jnp.where(col == br * (NSA_KV_HEADS * R) + g * R + row_head, 1.0, 0.0).astype(BF16)
            out_t = out_t + (_dot_nt(spread, ghi) + _dot_nt(spread, glo)) * to_rows(o_br)
        o_ref[tok, :] = out_t.T.astype(o_ref.dtype)

    stage_kv()
    ctx = tile_front(0)
    for n in range(q_tiles):
        m_s = selected_scores(n, ctx)
        nxt = tile_front(n + 1) if n + 1 < q_tiles else None
        finish(n, ctx, m_s)
        ctx = nxt


def _nsa_attention(proj, kc, vc, ksw, q_gain, coeffs, B, S):
    T = B * S
    w = GROUP_LANES
    G = NSA_KV_HEADS
    tq, bk = ATT_TQ, ATT_BK
    nq = S // tq
    nc = S // CMP_STRIDE
    nsb = S // SEL_BLOCK
    n_cmp = (S - CMP_BLOCK) // CMP_STRIDE + 1
    cs = np.arange(nc) * CMP_STRIDE
    js = np.arange(LANES) * SEL_BLOCK
    ov = ((cs[None, :] < js[:, None] + SEL_BLOCK) & (cs[None, :] + CMP_BLOCK > js[:, None])
          & (np.arange(nc)[None, :] < n_cmp) & (np.arange(LANES)[:, None] < nsb))
    ov = jnp.asarray(ov.astype(np.float32), dtype=BF16)
    gate_blk = (G * NSA_GROUP * HEAD_DIM) // LANES
    assert tq == bk and WINDOW == 2 * bk and HEAD_DIM + nsb <= LANES
    kern = functools.partial(_nsa_kernel, tq=tq, bk=bk, nsb=nsb, scale=HEAD_DIM ** -0.5, q_tiles=nq)
    seq = lambda col: pl.BlockSpec((S, w), lambda b, g: (b, col))
    const = lambda shape: pl.BlockSpec(shape, lambda b, g: (0, 0), pipeline_mode=pl.Buffered(1))
    rows = NSA_GROUP * tq
    n_tiles = S // bk
    return pl.pallas_call(
        kern,
        grid=(B, G),
        in_specs=[
            pl.BlockSpec((S, w), lambda b, g: (b, g)),
            pl.BlockSpec((S, LANES), lambda b, g: (b, gate_blk)),
            pl.BlockSpec((nc, w), lambda b, g: (b, 0)),
            pl.BlockSpec((nc, w), lambda b, g: (b, 0)),
            seq(0), seq(1), seq(2), seq(3),
            const((w, w)), const((1, w)), const((S, w)), const((S, w)), const((S, w)), const((LANES, nc)),
        ],
        out_specs=pl.BlockSpec((S, w), lambda b, g: (b, g)),
        out_shape=jax.ShapeDtypeStruct((T, G * w), BF16),
        scratch_shapes=[
            pltpu.VMEM((nc, w), BF16), pltpu.VMEM((S, w), BF16), pltpu.VMEM((S, w), BF16),
            pltpu.VMEM((HEAD_DIM, nc), BF16),
            pltpu.VMEM((n_tiles, V_ROWS, bk), BF16), pltpu.VMEM((n_tiles, V_ROWS, bk), BF16),
            pltpu.VMEM((n_tiles, bk, rows), F32),
        ],
        compiler_params=_params("parallel", "parallel"),
        name="nsa_attn",
    )(proj, proj, kc, vc, ksw, ksw, ksw, ksw,
      _block_mean_matrix(w), jnp.tile(q_gain, NSA_GROUP).reshape(1, w), *coeffs, ov)


def kernel(x, ffn1_norm, ffn1_w_in, ffn1_w_out, mix_norm, ffn2_norm, ffn2_w_in, ffn2_w_out,
           sb_w_qkv, sb_w_out, kv_norm, nsa_w_kv, nsa_k_norm, cmp_pos_k, cmp_pos_v,
           cmp_k_w1, cmp_k_w2, cmp_v_w1, cmp_v_w2, nsa_w_in, nsa_q_norm, nsa_w_out):
    B, S, D = x.shape
    depth = ffn1_norm.shape[0]
    n_a = sb_w_qkv.shape[0]
    n_sb_heads = sb_w_out.shape[1] // HEAD_DIM
    assert S % 512 == 0 and S // CMP_STRIDE == LANES, "sequence tiling assumes 16*128 tokens"
    h = x.reshape(B * S, D)
    bf = lambda a: a.astype(BF16)
    coeffs = _rope_coeffs(jnp.arange(S), NSA_KV_HEADS)
    n_q = nsa_w_in.shape[-1]
    n_q_pad = -(-n_q // LANES) * LANES
    kc = vc = ksw = None
    w1_in, w1_out, w2_in, w2_out = bf(ffn1_w_in), bf(ffn1_w_out), bf(ffn2_w_in), bf(ffn2_w_out)
    w_qkv, w_sb_out, w_nsa_out = bf(sb_w_qkv), bf(sb_w_out), bf(nsa_w_out)
    w_nsa_in = jnp.pad(bf(nsa_w_in), ((0, 0), (0, 0), (0, n_q_pad - n_q)))
    w_kv = bf(nsa_w_kv)[None]
    for layer in range(depth):
        h = _ffn(h, ffn1_norm[layer], w1_in, w1_out, layer)
        if layer < n_a:
            qkv = _norm_proj(h, mix_norm[layer], w_qkv, layer, 1024, BF16)
            o = _sb_attention(qkv, B, S, n_sb_heads)
            h = _out_proj(h, o, w_sb_out, layer)
        else:
            li = layer - n_a
            proj = _norm_proj(h, mix_norm[layer], w_nsa_in, li, n_q_pad, F32)
            o = _nsa_attention(proj, kc, vc, ksw, nsa_q_norm[li], coeffs, B, S)
            h = _out_proj(h, o, w_nsa_out, li)
        h = _ffn(h, ffn2_norm[layer], w2_in, w2_out, layer)
        if layer == n_a - 1:
            kv = _norm_proj(h, kv_norm, w_kv, 0, 768, F32)
            kc, vc = _compress(kv, B, S, cmp_pos_k, cmp_pos_v, cmp_k_w1, cmp_k_w2,
                               cmp_v_w1, cmp_v_w2, nsa_k_norm[0])
            ksw = _kv_prep(kv, nsa_k_norm, coeffs, S)
    return h.reshape(B, S, D)
```

```python
import functools

import jax
import jax.numpy as jnp
import numpy as np
from jax import lax
from jax.experimental import pallas as pl
from jax.experimental.pallas import tpu as pltpu

F32 = jnp.float32
BF16 = jnp.bfloat16

HEAD_DIM = 64
NSA_KV_HEADS = 4
NSA_GROUP = 4
ROPE_DIMS = 16
ROPE_THETA = 500000.0
CMP_BLOCK = 32
CMP_STRIDE = 16
SEL_BLOCK = 64
N_SEL = 8
N_LOCAL_SEL = 2
WINDOW = 512
EPS = 1e-6
LOG2E = 1.4426950408889634
NEG = -1e30
FORCE = 1e4

LANES = 128
MXU_TILE = 256
GROUP_LANES = NSA_GROUP * HEAD_DIM
VMEM_LIMIT_BYTES = 56 * 1024 * 1024

FFN_TM = 1024
PROJ_TM = 1024
ATT_TQ = 256
ATT_BK = 256


def _dot(a, b):
    return jnp.dot(a, b, preferred_element_type=F32)


def _dot_nt(a, b):
    return lax.dot_general(a, b, (((1,), (1,)), ((), ())), preferred_element_type=F32)


def _split_bf16(x):
    hi = x.astype(BF16)
    lo = (x - hi.astype(F32)).astype(BF16)
    return hi, lo


def _dot_split(x, w):
    hi, lo = _split_bf16(x)
    n = x.shape[0]
    r = _dot(jnp.concatenate([hi, lo], axis=0), w)
    return r[0:n] + r[n:2 * n]


def _sigmoid(x):
    return 1.0 / (1.0 + jnp.exp(-x))


def _iota(shape, dim):
    return lax.broadcasted_iota(jnp.int32, shape, dim)


def _params(*sem):
    return pltpu.CompilerParams(dimension_semantics=sem, vmem_limit_bytes=VMEM_LIMIT_BYTES)


def _ffn_kernel(h_ref, g_ref, wi_ref, wo_ref, o_ref, *, chunks):
    x = h_ref[...]
    ms = jnp.mean(x * x, axis=-1, keepdims=True)
    xn = (x * lax.rsqrt(ms + EPS) * g_ref[...]).astype(BF16)
    ff = chunks[-1][1]
    ab = [(_dot(xn, wi_ref[:, lo:hi]), _dot(xn, wi_ref[:, ff + lo:ff + hi])) for lo, hi in chunks]
    acc = None
    for (lo, hi), (a, b) in zip(chunks, ab):
        act = (a * _sigmoid(a) * b).astype(BF16)
        part = _dot(act, wo_ref[lo:hi, :])
        acc = part if acc is None else acc + part
    o_ref[...] = x + 0.5 * acc


def _ffn(h, gain, w_in, w_out, layer):
    T, D = h.shape
    F = w_out.shape[1]
    tm = min(FFN_TM, T)
    n_mxu = F // MXU_TILE
    split = -(-n_mxu // 2) * MXU_TILE
    chunks = ((0, split), (split, F)) if F % MXU_TILE == 0 and n_mxu > 1 else ((0, F),)
    resident = dict(pipeline_mode=pl.Buffered(1))
    return pl.pallas_call(
        functools.partial(_ffn_kernel, chunks=chunks),
        grid=(T // tm,),
        in_specs=[
            pl.BlockSpec((tm, D), lambda i: (i, 0)),
            pl.BlockSpec((1, D), lambda i: (0, 0)),
            pl.BlockSpec((None, D, 2 * F), lambda i: (layer, 0, 0), **resident),
            pl.BlockSpec((None, F, D), lambda i: (layer, 0, 0), **resident),
        ],
        out_specs=pl.BlockSpec((tm, D), lambda i: (i, 0)),
        out_shape=jax.ShapeDtypeStruct((T, D), F32),
        compiler_params=_params("parallel"),
        name="ffn",
    )(h, gain.reshape(1, D), w_in, w_out)


def _norm_proj_kernel(h_ref, g_ref, w_ref, o_ref, xn_ref):
    @pl.when(pl.program_id(1) == 0)
    def _():
        x = h_ref[...]
        ms = jnp.mean(x * x, axis=-1, keepdims=True)
        xn_ref[...] = (x * lax.rsqrt(ms + EPS) * g_ref[...]).astype(BF16)

    o_ref[...] = _dot(xn_ref[...], w_ref[...]).astype(o_ref.dtype)


def _norm_proj(h, gain, w, layer, tn, out_dtype):
    T, D = h.shape
    N = w.shape[2]
    tm = min(PROJ_TM, T)
    return pl.pallas_call(
        _norm_proj_kernel,
        grid=(T // tm, N // tn),
        in_specs=[
            pl.BlockSpec((tm, D), lambda i, j: (i, 0)),
            pl.BlockSpec((1, D), lambda i, j: (0, 0)),
            pl.BlockSpec((None, D, tn), lambda i, j: (layer, 0, j)),
        ],
        out_specs=pl.BlockSpec((tm, tn), lambda i, j: (i, j)),
        out_shape=jax.ShapeDtypeStruct((T, N), out_dtype),
        scratch_shapes=[pltpu.VMEM((tm, D), BF16)],
        compiler_params=_params("parallel", "arbitrary"),
        name="norm_proj",
    )(h, gain.reshape(1, D), w)


def _out_proj_kernel(h_ref, x_ref, w_ref, o_ref):
    o_ref[...] = h_ref[...] + _dot(x_ref[...], w_ref[...])


def _out_proj(h, x, w, layer):
    T, D = h.shape
    K = x.shape[1]
    tm = min(PROJ_TM, T)
    return pl.pallas_call(
        _out_proj_kernel,
        grid=(T // tm,),
        in_specs=[
            pl.BlockSpec((tm, D), lambda i: (i, 0)),
            pl.BlockSpec((tm, K), lambda i: (i, 0)),
            pl.BlockSpec((None, K, D), lambda i: (layer, 0, 0)),
        ],
        out_specs=pl.BlockSpec((tm, D), lambda i: (i, 0)),
        out_shape=jax.ShapeDtypeStruct((T, D), F32),
        compiler_params=_params("parallel"),
        name="out_proj",
    )(h, x, w)


def _sb_kernel(q_ref, k_ref, v_ref, o_ref, *, tq, bk, scale, q_tiles):
    lane = _iota((1, LANES), 1)
    is_a = lane < HEAD_DIM
    tri = jnp.where(_iota((bk, bk), 0) > _iota((bk, bk), 1), 1.0, 0.0).astype(BF16)
    past = _iota((tq, bk), 1) < _iota((tq, bk), 0)

    chains = [(t, j, hd) for t in range(q_tiles) for j in range(t, -1, -1) for hd in range(2)]
    n = len(chains)
    qh, carry, acc = {}, {}, {}
    z, log_b, log_1m, within = {}, {}, {}, {}
    for k in range(n + 2):
        if k < n:
            t, j, hd = chains[k]
            if t not in qh:
                q = q_ref[t * tq:(t + 1) * tq, :] * scale
                zero = jnp.zeros_like(q)
                qh[t] = (jnp.where(is_a, q, zero), jnp.where(is_a, zero, q))
                carry[t] = [jnp.zeros((tq, 1), F32), jnp.zeros((tq, 1), F32)]
                acc[t] = [jnp.zeros((tq, LANES), F32), jnp.zeros((tq, LANES), F32)]
            z[k] = _dot_nt(qh[t][hd], k_ref[j * bk:(j + 1) * bk, :])
        if 0 <= k - 1 < n:
            c = k - 1
            t, j, hd = chains[c]
            e = jnp.exp2(jnp.abs(z[c]) * (-LOG2E))
            log_b[c] = jnp.minimum(z[c], 0.0) - jnp.log(1.0 + e)
            m = log_b[c] - z.pop(c)
            log_1m[c] = jnp.where(past, m, 0.0) if j == t else m
            within[c] = _dot(log_1m[c].astype(BF16), tri)
        if 0 <= k - 2 < n:
            c = k - 2
            t, j, hd = chains[c]
            wc = jnp.exp(log_b.pop(c) + (carry[t][hd] + within.pop(c)))
            if j == t:
                wc = jnp.where(past, wc, 0.0)
            carry[t][hd] = carry[t][hd] + jnp.sum(log_1m.pop(c), axis=1, keepdims=True)
            acc[t][hd] = acc[t][hd] + _dot(wc.astype(BF16), v_ref[j * bk:(j + 1) * bk, :])
            if j == 0 and hd == 1:
                o_ref[t * tq:(t + 1) * tq, :] = jnp.where(is_a, acc[t][0], acc[t][1]).astype(o_ref.dtype)
                del qh[t], carry[t], acc[t]


def _sb_attention(qkv, B, S, n_heads):
    T = B * S
    tq, bk = ATT_TQ, ATT_BK
    nq = S // tq
    n_pairs = n_heads * HEAD_DIM // LANES
    kern = functools.partial(_sb_kernel, tq=tq, bk=bk, scale=HEAD_DIM ** -0.5, q_tiles=nq)
    return pl.pallas_call(
        kern,
        grid=(B, n_pairs),
        in_specs=[
            pl.BlockSpec((S, LANES), lambda b, p: (b, p)),
            pl.BlockSpec((S, LANES), lambda b, p: (b, n_pairs + p)),
            pl.BlockSpec((S, LANES), lambda b, p: (b, 2 * n_pairs + p)),
        ],
        out_specs=pl.BlockSpec((S, LANES), lambda b, p: (b, p)),
        out_shape=jax.ShapeDtypeStruct((T, n_heads * HEAD_DIM), BF16),
        compiler_params=_params("parallel", "parallel"),
        name="sb_attn",
    )(qkv, qkv, qkv)


def _head_rmsnorm(x, bd, gain):
    ms = _dot_split(x * x, bd)
    return x * lax.rsqrt(ms + EPS) * gain


def _rope(y, c, sa, sb):
    w = y.shape[-1]
    half = ROPE_DIMS // 2
    return y * c + pltpu.roll(y, w - half, 1) * sa + pltpu.roll(y, half, 1) * sb


def _rope_coeffs(positions, n_heads):
    inv_freq = jnp.power(ROPE_THETA, -jnp.arange(0, ROPE_DIMS, 2, dtype=F32) / ROPE_DIMS)
    ang = positions.astype(F32)[:, None] * inv_freq[None, :]
    cos, sin = jnp.cos(ang), jnp.sin(ang)
    p = positions.shape[0]
    rest = HEAD_DIM - ROPE_DIMS
    zeros_h = jnp.zeros((p, ROPE_DIMS // 2), F32)
    c = jnp.concatenate([cos, cos, jnp.ones((p, rest), F32)], axis=-1)
    sa = jnp.concatenate([-sin, zeros_h, jnp.zeros((p, rest), F32)], axis=-1)
    sb = jnp.concatenate([zeros_h, sin, jnp.zeros((p, rest), F32)], axis=-1)
    return tuple(jnp.tile(t, (1, n_heads)) for t in (c, sa, sb))


def _block_mean_matrix(width):
    idx = np.arange(width) // HEAD_DIM
    return jnp.asarray((idx[:, None] == idx[None, :]).astype(np.float32) / HEAD_DIM, dtype=BF16)


def _kv_prep_kernel(sel_ref, win_ref, bd_ref, gs_ref, gw_ref, c_ref, sa_ref, sb_ref, o_ref):
    w = GROUP_LANES
    bd = bd_ref[...]
    c, sa, sb = c_ref[...], sa_ref[...], sb_ref[...]
    ks = _rope(_head_rmsnorm(sel_ref[:, :w], bd, gs_ref[...]), c, sa, sb)
    kw = _rope(_head_rmsnorm(win_ref[:, :w], bd, gw_ref[...]), c, sa, sb)
    o_ref[:, 0 * w:1 * w] = ks.astype(BF16)
    o_ref[:, 1 * w:2 * w] = sel_ref[:, w:].astype(BF16)
    o_ref[:, 2 * w:3 * w] = kw.astype(BF16)
    o_ref[:, 3 * w:4 * w] = win_ref[:, w:].astype(BF16)


def _kv_prep(kv, k_norm, coeffs, S):
    T = kv.shape[0]
    w = GROUP_LANES
    tm = min(PROJ_TM, S)
    ns = S // tm
    gs = jnp.tile(k_norm[1], NSA_KV_HEADS).reshape(1, w)
    gw = jnp.tile(k_norm[2], NSA_KV_HEADS).reshape(1, w)
    tab = pl.BlockSpec((tm, w), lambda i: (i % ns, 0))
    one = pl.BlockSpec((1, w), lambda i: (0, 0))
    return pl.pallas_call(
        _kv_prep_kernel,
        grid=(T // tm,),
        in_specs=[
            pl.BlockSpec((tm, 2 * w), lambda i: (i, 1)),
            pl.BlockSpec((tm, 2 * w), lambda i: (i, 2)),
            pl.BlockSpec((w, w), lambda i: (0, 0)),
            one, one, tab, tab, tab,
        ],
        out_specs=pl.BlockSpec((tm, 4 * w), lambda i: (i, 0)),
        out_shape=jax.ShapeDtypeStruct((T, 4 * w), BF16),
        compiler_params=_params("parallel"),
        name="kv_prep",
    )(kv, kv, _block_mean_matrix(w), gs, gw, *coeffs)


def _compress_kernel(x_ref, pos_ref, w1_ref, w2_ref, bd_ref, g_ref, c_ref, sa_ref, sb_ref,
                     o_ref, top_ref, bot_ref, wbd_ref):
    t = pl.program_id(0)
    l = pl.program_id(1)
    half = CMP_BLOCK // 2
    hid = w1_ref.shape[-1]

    @pl.when(l == 0)
    def _():
        top_ref[...] = jnp.zeros_like(top_ref)
        bot_ref[...] = jnp.zeros_like(bot_ref)
        wbd_ref[...] = jnp.zeros_like(wbd_ref)

    for u in range(2):
        for g in range(NSA_KV_HEADS):
            wbd_ref[u, g * HEAD_DIM:(g + 1) * HEAD_DIM, g * hid:(g + 1) * hid] = w1_ref[0, u]

    x = x_ref[0]
    top_ref[...] += _dot((x + pos_ref[0, 0]).astype(BF16), wbd_ref[0])
    bot_ref[...] += _dot((x + pos_ref[0, 1]).astype(BF16), wbd_ref[1])

    @pl.when(l == half - 1)
    def _():
        rows = bot_ref.shape[0]
        hidden = top_ref[...] + pltpu.roll(bot_ref[...], rows - 1, 0)
        act = (hidden * _sigmoid(hidden)).astype(BF16)
        y = _dot(act, w2_ref[0])
        yk = _rope(_head_rmsnorm(y, bd_ref[...], g_ref[...]), c_ref[...], sa_ref[...], sb_ref[...])
        o_ref[0] = jnp.where(t == 0, yk, y).astype(BF16)


def _compress(kv, B, S, pos_k, pos_v, k_w1, k_w2, v_w1, v_w2, k_gain):
    w = GROUP_LANES
    G = NSA_KV_HEADS
    half = CMP_BLOCK // 2
    nc = S // CMP_STRIDE
    rows = B * nc
    hid = k_w1.shape[1]
    raw = kv[:, :2 * w].reshape(B, nc, half, 2, w)
    xc = jnp.transpose(raw, (3, 2, 0, 1, 4)).reshape(2, half, rows, w)
    eye = jnp.eye(G, dtype=F32)

    def blockdiag(wm):
        out = jnp.einsum('gh,...ab->...gahb', eye, wm)
        return out.reshape(*wm.shape[:-2], G * wm.shape[-2], G * wm.shape[-1])

    def prep_w1(w1):
        return jnp.transpose(w1.reshape(2, half, HEAD_DIM, hid), (1, 0, 2, 3))

    w1 = jnp.stack([prep_w1(k_w1), prep_w1(v_w1)]).astype(BF16)
    w2 = jnp.stack([blockdiag(k_w2), blockdiag(v_w2)]).astype(BF16)

    def prep_pos(p):
        p = jnp.tile(p, (1, G)).reshape(2, half, 1, w)
        return jnp.transpose(p, (1, 0, 2, 3))

    pos = jnp.stack([prep_pos(pos_k), prep_pos(pos_v)])
    c_end = jnp.arange(nc) * CMP_STRIDE + (CMP_BLOCK - 1)
    coeffs = tuple(jnp.tile(tb, (B, 1)) for tb in _rope_coeffs(c_end, G))
    gain = jnp.tile(k_gain, G).reshape(1, w)
    const2 = lambda t, l: (0, 0)
    out = pl.pallas_call(
        _compress_kernel,
        grid=(2, half),
        in_specs=[
            pl.BlockSpec((None, 1, rows, w), lambda t, l: (t, l, 0, 0)),
            pl.BlockSpec((None, 1, 2, 1, w), lambda t, l: (t, l, 0, 0, 0)),
            pl.BlockSpec((None, 1, 2, HEAD_DIM, hid), lambda t, l: (t, l, 0, 0, 0)),
            pl.BlockSpec((1, G * hid, w), lambda t, l: (t, 0, 0)),
            pl.BlockSpec((w, w), const2),
            pl.BlockSpec((1, w), const2),
            pl.BlockSpec((rows, w), const2),
            pl.BlockSpec((rows, w), const2),
            pl.BlockSpec((rows, w), const2),
        ],
        out_specs=pl.BlockSpec((1, rows, w), lambda t, l: (t, 0, 0)),
        out_shape=jax.ShapeDtypeStruct((2, rows, w), BF16),
        scratch_shapes=[pltpu.VMEM((rows, G * hid), F32), pltpu.VMEM((rows, G * hid), F32),
                        pltpu.VMEM((2, w, G * hid), BF16)],
        compiler_params=_params("arbitrary", "arbitrary"),
        name="nsa_compress",
    )(xc, pos, w1, w2, _block_mean_matrix(w), gain, *coeffs)
    return out[0], out[1]


MASK_BIG = 2.0 ** 30
V_ROWS = HEAD_DIM + 16


def _nsa_kernel(q_ref, gl_ref, kc_ref, vc_ref, ks_ref, vs_ref, kw_ref, vw_ref,
                bd_ref, qg_ref, c_ref, sa_ref, sb_ref, ov_ref,
                o_ref,
                kc_a, ks_a, kw_a, vct, vst, vwt, s_scr,
                *, tq, bk, nsb, scale, q_tiles):
    g = pl.program_id(1)
    w = GROUP_LANES
    R = NSA_GROUP
    rows = R * tq
    lane = _iota((1, w), 1)
    n_tiles = ks_a.shape[0] // bk
    ncp = kc_a.shape[0]

    def stage_kv():
        rr, cc = _iota((w, w), 0), _iota((w, w), 1)
        mine = ((rr >> 6) == g) & ((rr & 63) == (cc & 63))
        to_slot0 = jnp.where(mine & (cc < HEAD_DIM), 1.0, 0.0).astype(BF16)
        kc_a[...] = _dot(kc_ref[...], to_slot0).astype(BF16)
        kw_a[...] = _dot(kw_ref[...], to_slot0).astype(BF16)
        seq = ks_a.shape[0]
        tag = jnp.where(_iota((seq, w), 1) - HEAD_DIM == (_iota((seq, w), 0) >> 6), 1.0, 0.0)
        ks_a[...] = (_dot(ks_ref[...], to_slot0) + tag).astype(BF16)
        pick = jnp.where(_iota((V_ROWS, w), 1) == g * HEAD_DIM + _iota((V_ROWS, w), 0), 1.0, 0.0)
        pick = jnp.where(_iota((V_ROWS, w), 0) < HEAD_DIM, pick, 0.0).astype(BF16)
        vct[...] = _dot_nt(pick[0:HEAD_DIM], vc_ref[...]).astype(BF16)
        ones_row = jnp.where(_iota((V_ROWS, bk), 0) == HEAD_DIM, 1.0, 0.0)
        for j in range(n_tiles):
            vst[j] = (_dot_nt(pick, vs_ref[j * bk:(j + 1) * bk, :]) + ones_row).astype(BF16)
            vwt[j] = (_dot_nt(pick, vw_ref[j * bk:(j + 1) * bk, :]) + ones_row).astype(BF16)

    def lanes4(x):
        return jnp.concatenate([x] * R, axis=1)

    def part_max(x):
        return jnp.max(x.reshape(x.shape[0] // 8, 8, rows), axis=0)

    def tile_front(i):
        tok = slice(i * tq, (i + 1) * tq)
        qn = _rope(_head_rmsnorm(q_ref[tok, :], bd_ref[...], qg_ref[...]),
                   c_ref[tok, :], sa_ref[tok, :], sb_ref[tok, :]) * scale
        slot0 = lane < HEAD_DIM
        parts = []
        for r in range(R):
            qr = qn if r == 0 else pltpu.roll(qn, w - r * HEAD_DIM, 1)
            parts.append(jnp.where(slot0, qr, 0.0))
        qs_f32 = jnp.concatenate(parts, axis=0)
        qs = qs_f32.astype(BF16)
        t_col = i * tq + (_iota((1, rows), 1) & (tq - 1))
        key_le_query = _iota((bk, tq), 0) <= _iota((bk, tq), 1)
        diag_bias = lanes4(jnp.where(key_le_query, 0.0, NEG))

        s = _dot_nt(kc_a[...], qs)

        win_blk = [j for j in (i - 2, i - 1, i) if j >= 0]
        win_sc = []
        for j in win_blk:
            sc = _dot_nt(kw_a[j * bk:(j + 1) * bk, :], qs)
            if j == i - 2:
                sc = sc + lanes4(jnp.where(key_le_query, NEG, 0.0))
            elif j == i:
                sc = sc + diag_bias
            win_sc.append(sc)

        c_end = _iota((ncp, 1), 0) * CMP_STRIDE + (CMP_BLOCK - 1)
        s = jnp.where(c_end <= t_col, s, NEG)
        e = jnp.exp(s - jnp.max(s, axis=0, keepdims=True))
        p = e / jnp.sum(e, axis=0, keepdims=True)
        p = jnp.where(t_col >= CMP_BLOCK - 1, p, 0.0)
        o_cmp = _dot(vct[...], p.astype(BF16))
        p_sum = p[:, 0:tq]
        for r in range(1, R):
            p_sum = p_sum + p[:, r * tq:(r + 1) * tq]

        hi, lo = _split_bf16(p_sum)
        ov = ov_ref[...]
        imp = (_dot(ov, hi) + _dot(ov, lo))[0:nsb]
        jb = _iota((nsb, 1), 0)
        cur = (i * tq + _iota((1, tq), 1)) >> 6
        forced = (jb == 0) | ((cur - jb >= 0) & (cur - jb < N_LOCAL_SEL))
        score = jnp.where(forced, FORCE, jnp.where(jb <= cur, imp, -FORCE))
        cnt = jnp.zeros((nsb, tq), F32)
        sub8 = _iota((8, 1), 0)
        for i2 in range(nsb):
            row = score[i2:i2 + 1, :]
            grp = []
            for r in range(nsb // 8):
                blk = score[8 * r:8 * r + 8]
                gt = jnp.where(row > blk, 1.0, 0.0)
                if 8 * r + 7 <= i2:
                    grp.append(gt)
                elif 8 * r > i2:
                    grp.append(jnp.where(row >= blk, 1.0, 0.0))
                else:
                    grp.append(jnp.where(sub8 > i2 - 8 * r, jnp.where(row >= blk, 1.0, 0.0), gt))
            cnt = cnt + jnp.concatenate(grp, axis=0)
        sel_t = jnp.where((cnt < float(min(N_SEL, nsb))) & (jb <= cur), 0.0, -MASK_BIG)
        pad_t = jnp.concatenate([jnp.zeros((HEAD_DIM, tq), F32), sel_t,
                                 jnp.zeros((LANES - HEAD_DIM - nsb, tq), F32)], axis=0)
        aug = jnp.concatenate([pad_t.T, jnp.zeros((tq, w - LANES), F32)], axis=1)
        qsel = (qs_f32 + jnp.concatenate([aug] * R, axis=0)).astype(BF16)
        return dict(tok=tok, qsel=qsel, win_sc=win_sc, win_blk=win_blk, o_cmp=o_cmp, diag_bias=diag_bias)

    def selected_scores(i, c):
        mx8 = None
        for x in range(i + 1):
            sc = _dot_nt(ks_a[x * bk:(x + 1) * bk, :], c["qsel"])
            if x == i:
                sc = sc + c["diag_bias"]
            s_scr[x] = sc
            mx8 = part_max(sc) if mx8 is None else jnp.maximum(mx8, part_max(sc))
        return jnp.max(mx8, axis=0, keepdims=True)

    def finish(i, c, m_s):
        tok, win_sc, win_blk, o_cmp = c["tok"], c["win_sc"], c["win_blk"], c["o_cmp"]
        m_w = win_sc[0]
        for sc in win_sc[1:]:
            m_w = jnp.maximum(m_w, sc)
        m_w = jnp.max(m_w, axis=0, keepdims=True)
        p_cat = jnp.concatenate([jnp.exp(sc - m_w).astype(BF16) for sc in win_sc], axis=0)
        v_cat = jnp.concatenate([vwt[j] for j in win_blk], axis=1)
        acc_w = _dot(v_cat, p_cat)
        o_win = acc_w[0:HEAD_DIM] / acc_w[HEAD_DIM:HEAD_DIM + 1]

        acc_s = None
        for x0 in range(0, i + 1, 2):
            xs = [x for x in (x0, x0 + 1) if x <= i]
            p2 = jnp.concatenate([jnp.exp(s_scr[x] - m_s).astype(BF16) for x in xs], axis=0)
            v2 = jnp.concatenate([vst[x] for x in xs], axis=1)
            d = _dot(v2, p2)
            acc_s = d if acc_s is None else acc_s + d
        o_sel = acc_s[0:HEAD_DIM] / acc_s[HEAD_DIM:HEAD_DIM + 1]

        def to_rows(o_t):
            return jnp.concatenate([o_t[:, r * tq:(r + 1) * tq] for r in range(R)], axis=0)

        ghi, glo = _split_bf16(_sigmoid(gl_ref[tok, :]))
        row_head = _iota((w, LANES), 0) >> 6
        col = _iota((w, LANES), 1)
        out_t = jnp.zeros((w, tq), F32)
        for br, o_br in enumerate((o_cmp, o_sel, o_win)):
            spread = jnp.where(col == br * (NSA_KV_HEADS * R) + g * R + row_head, 1.0, 0.0).astype(BF16)
            out_t = out_t + (_dot_nt(spread, ghi) + _dot_nt(spread, glo)) * to_rows(o_br)
        o_ref[tok, :] = out_t.T.astype(o_ref.dtype)

    stage_kv()
    ctx = tile_front(0)
    for n in range(q_tiles):
        m_s = selected_scores(n, ctx)
        nxt = tile_front(n + 1) if n + 1 < q_tiles else None
        finish(n, ctx, m_s)
        ctx = nxt


def _nsa_attention(proj, kc, vc, ksw, q_gain, coeffs, B, S):
    T = B * S
    w = GROUP_LANES
    G = NSA_KV_HEADS
    tq, bk = ATT_TQ, ATT_BK
    nq = S // tq
    nc = S // CMP_STRIDE
    nsb = S // SEL_BLOCK
    n_cmp = (S - CMP_BLOCK) // CMP_STRIDE + 1
    cs = np.arange(nc) * CMP_STRIDE
    js = np.arange(LANES) * SEL_BLOCK
    ov = ((cs[None, :] < js[:, None] + SEL_BLOCK) & (cs[None, :] + CMP_BLOCK > js[:, None])
          & (np.arange(nc)[None, :] < n_cmp) & (np.arange(LANES)[:, None] < nsb))
    ov = jnp.asarray(ov.astype(np.float32), dtype=BF16)
    gate_blk = (G * NSA_GROUP * HEAD_DIM) // LANES
    assert tq == bk and WINDOW == 2 * bk and HEAD_DIM + nsb <= LANES
    kern = functools.partial(_nsa_kernel, tq=tq, bk=bk, nsb=nsb, scale=HEAD_DIM ** -0.5, q_tiles=nq)
    seq = lambda col: pl.BlockSpec((S, w), lambda b, g: (b, col))
    const = lambda shape: pl.BlockSpec(shape, lambda b, g: (0, 0), pipeline_mode=pl.Buffered(1))
    rows = NSA_GROUP * tq
    n_tiles = S // bk
    return pl.pallas_call(
        kern,
        grid=(B, G),
        in_specs=[
            pl.BlockSpec((S, w), lambda b, g: (b, g)),
            pl.BlockSpec((S, LANES), lambda b, g: (b, gate_blk)),
            pl.BlockSpec((nc, w), lambda b, g: (b, 0)),
            pl.BlockSpec((nc, w), lambda b, g: (b, 0)),
            seq(0), seq(1), seq(2), seq(3),
            const((w, w)), const((1, w)), const((S, w)), const((S, w)), const((S, w)), const((LANES, nc)),
        ],
        out_specs=pl.BlockSpec((S, w), lambda b, g: (b, g)),
        out_shape=jax.ShapeDtypeStruct((T, G * w), BF16),
        scratch_shapes=[
            pltpu.VMEM((nc, w), BF16), pltpu.VMEM((S, w), BF16), pltpu.VMEM((S, w), BF16),
            pltpu.VMEM((HEAD_DIM, nc), BF16),
            pltpu.VMEM((n_tiles, V_ROWS, bk), BF16), pltpu.VMEM((n_tiles, V_ROWS, bk), BF16),
            pltpu.VMEM((n_tiles, bk, rows), F32),
        ],
        compiler_params=_params("parallel", "parallel"),
        name="nsa_attn",
    )(proj, proj, kc, vc, ksw, ksw, ksw, ksw,
      _block_mean_matrix(w), jnp.tile(q_gain, NSA_GROUP).reshape(1, w), *coeffs, ov)


def kernel(x, ffn1_norm, ffn1_w_in, ffn1_w_out, mix_norm, ffn2_norm, ffn2_w_in, ffn2_w_out,
           sb_w_qkv, sb_w_out, kv_norm, nsa_w_kv, nsa_k_norm, cmp_pos_k, cmp_pos_v,
           cmp_k_w1, cmp_k_w2, cmp_v_w1, cmp_v_w2, nsa_w_in, nsa_q_norm, nsa_w_out):
    B, S, D = x.shape
    depth = ffn1_norm.shape[0]
    n_a = sb_w_qkv.shape[0]
    n_sb_heads = sb_w_out.shape[1] // HEAD_DIM
    assert S % 512 == 0 and S // CMP_STRIDE == LANES, "sequence tiling assumes 16*128 tokens"
    h = x.reshape(B * S, D)
    bf = lambda a: a.astype(BF16)
    coeffs = _rope_coeffs(jnp.arange(S), NSA_KV_HEADS)
    n_q = nsa_w_in.shape[-1]
    n_q_pad = -(-n_q // LANES) * LANES
    kc = vc = ksw = None
    w1_in, w1_out, w2_in, w2_out = bf(ffn1_w_in), bf(ffn1_w_out), bf(ffn2_w_in), bf(ffn2_w_out)
    w_qkv, w_sb_out, w_nsa_out = bf(sb_w_qkv), bf(sb_w_out), bf(nsa_w_out)
    w_nsa_in = jnp.pad(bf(nsa_w_in), ((0, 0), (0, 0), (0, n_q_pad - n_q)))
    w_kv = bf(nsa_w_kv)[None]
    for layer in range(depth):
        h = _ffn(h, ffn1_norm[layer], w1_in, w1_out, layer)
        if layer < n_a:
            qkv = _norm_proj(h, mix_norm[layer], w_qkv, layer, 1024, BF16)
            o = _sb_attention(qkv, B, S, n_sb_heads)
            h = _out_proj(h, o, w_sb_out, layer)
        else:
            li = layer - n_a
            proj = _norm_proj(h, mix_norm[layer], w_nsa_in, li, n_q_pad, F32)
            o = _nsa_attention(proj, kc, vc, ksw, nsa_q_norm[li], coeffs, B, S)
            h = _out_proj(h, o, w_nsa_out, li)
        h = _ffn(h, ffn2_norm[layer], w2_in, w2_out, layer)
        if layer == n_a - 1:
            kv = _norm_proj(h, kv_norm, w_kv, 0, 768, F32)
            kc, vc = _compress(kv, B, S, cmp_pos_k, cmp_pos_v, cmp_k_w1, cmp_k_w2,
                               cmp_v_w1, cmp_v_w2, nsa_k_norm[0])
            ksw = _kv_prep(kv, nsa_k_norm, coeffs, S)
    return h.reshape(B, S, D)
```

```python
import functools

import jax
import jax.numpy as jnp
import numpy as np
from jax import lax
from jax.experimental import pallas as pl
from jax.experimental.pallas import tpu as pltpu

F32 = jnp.float32
BF16 = jnp.bfloat16

HEAD_DIM = 64
NSA_KV_HEADS = 4
NSA_GROUP = 4
ROPE_DIMS = 16
ROPE_THETA = 500000.0
CMP_BLOCK = 32
CMP_STRIDE = 16
SEL_BLOCK = 64
N_SEL = 8
N_LOCAL_SEL = 2
WINDOW = 512
EPS = 1e-6
LOG2E = 1.4426950408889634
NEG = -1e30
FORCE = 1e4

LANES = 128
MXU_TILE = 256
GROUP_LANES = NSA_GROUP * HEAD_DIM
VMEM_LIMIT_BYTES = 56 * 1024 * 1024

FFN_TM = 512
PROJ_TM = 1024
ATT_TQ = 256
ATT_BK = 256


def _dot(a, b):
    return jnp.dot(a, b, preferred_element_type=F32)


def _dot_nt(a, b):
    return lax.dot_general(a, b, (((1,), (1,)), ((), ())), preferred_element_type=F32)


def _split_bf16(x):
    hi = x.astype(BF16)
    lo = (x - hi.astype(F32)).astype(BF16)
    return hi, lo


def _dot_split(x, w):
    hi, lo = _split_bf16(x)
    n = x.shape[0]
    r = _dot(jnp.concatenate([hi, lo], axis=0), w)
    return r[0:n] + r[n:2 * n]


def _sigmoid(x):
    return 1.0 / (1.0 + jnp.exp(-x))


def _iota(shape, dim):
    return lax.broadcasted_iota(jnp.int32, shape, dim)


def _params(*sem):
    return pltpu.CompilerParams(dimension_semantics=sem, vmem_limit_bytes=VMEM_LIMIT_BYTES)


def _ffn_kernel(*refs, chunks, mixer):
    if mixer:
        h_ref, a_ref, wp_ref, g_ref, wi_ref, wo_ref, o_ref = refs
        x = h_ref[...] + _dot(a_ref[...], wp_ref[...])
    else:
        h_ref, g_ref, wi_ref, wo_ref, o_ref = refs
        x = h_ref[...]
    ms = jnp.mean(x * x, axis=-1, keepdims=True)
    xn = (x * lax.rsqrt(ms + EPS) * g_ref[...]).astype(BF16)
    ff = chunks[-1][1]
    ab = [(_dot(xn, wi_ref[:, lo:hi]), _dot(xn, wi_ref[:, ff + lo:ff + hi])) for lo, hi in chunks]
    acc = None
    for (lo, hi), (a, b) in zip(chunks, ab):
        act = (a * _sigmoid(a) * b).astype(BF16)
        part = _dot(act, wo_ref[lo:hi, :])
        acc = part if acc is None else acc + part
    o_ref[...] = x + 0.5 * acc


def _ffn(h, gain, w_in, w_out, layer, mixer=None):
    T, D = h.shape
    F = w_out.shape[1]
    tm = min(FFN_TM, T)
    n_mxu = F // MXU_TILE
    split = -(-n_mxu // 2) * MXU_TILE
    chunks = ((0, split), (split, F)) if F % MXU_TILE == 0 and n_mxu > 1 else ((0, F),)
    resident = dict(pipeline_mode=pl.Buffered(1))
    mix_specs, mix_args = [], []
    if mixer is not None:
        o, w_proj, lp = mixer
        K = o.shape[1]
        mix_specs = [pl.BlockSpec((tm, K), lambda i: (i, 0)),
                     pl.BlockSpec((None, K, D), lambda i: (lp, 0, 0), **resident)]
        mix_args = [o, w_proj]
    return pl.pallas_call(
        functools.partial(_ffn_kernel, chunks=chunks, mixer=mixer is not None),
        grid=(T // tm,),
        in_specs=[
            pl.BlockSpec((tm, D), lambda i: (i, 0)),
            *mix_specs,
            pl.BlockSpec((1, D), lambda i: (0, 0)),
            pl.BlockSpec((None, D, 2 * F), lambda i: (layer, 0, 0), **resident),
            pl.BlockSpec((None, F, D), lambda i: (layer, 0, 0), **resident),
        ],
        out_specs=pl.BlockSpec((tm, D), lambda i: (i, 0)),
        out_shape=jax.ShapeDtypeStruct((T, D), F32),
        compiler_params=_params("parallel"),
        name="ffn",
    )(h, *mix_args, gain.reshape(1, D), w_in, w_out)


def _norm_proj_kernel(h_ref, g_ref, w_ref, o_ref, xn_ref):
    @pl.when(pl.program_id(1) == 0)
    def _():
        x = h_ref[...]
        ms = jnp.mean(x * x, axis=-1, keepdims=True)
        xn_ref[...] = (x * lax.rsqrt(ms + EPS) * g_ref[...]).astype(BF16)

    o_ref[...] = _dot(xn_ref[...], w_ref[...]).astype(o_ref.dtype)


def _norm_proj(h, gain, w, layer, tn, out_dtype):
    T, D = h.shape
    N = w.shape[2]
    tm = min(PROJ_TM, T)
    return pl.pallas_call(
        _norm_proj_kernel,
        grid=(T // tm, N // tn),
        in_specs=[
            pl.BlockSpec((tm, D), lambda i, j: (i, 0)),
            pl.BlockSpec((1, D), lambda i, j: (0, 0)),
            pl.BlockSpec((None, D, tn), lambda i, j: (layer, 0, j)),
        ],
        out_specs=pl.BlockSpec((tm, tn), lambda i, j: (i, j)),
        out_shape=jax.ShapeDtypeStruct((T, N), out_dtype),
        scratch_shapes=[pltpu.VMEM((tm, D), BF16)],
        compiler_params=_params("parallel", "arbitrary"),
        name="norm_proj",
    )(h, gain.reshape(1, D), w)


def _sb_kernel(q_ref, k_ref, v_ref, o_ref, *, tq, bk, scale, q_tiles):
    lane = _iota((1, LANES), 1)
    is_a = lane < HEAD_DIM
    tri = jnp.where(_iota((bk, bk), 0) > _iota((bk, bk), 1), 1.0, 0.0).astype(BF16)
    past = _iota((tq, bk), 1) < _iota((tq, bk), 0)

    chains = [(t, j, hd) for t in range(q_tiles) for j in range(t, -1, -1) for hd in range(2)]
    n = len(chains)
    qh, carry, acc = {}, {}, {}
    z, log_b, log_1m, within = {}, {}, {}, {}
    for k in range(n + 2):
        if k < n:
            t, j, hd = chains[k]
            if t not in qh:
                q = q_ref[t * tq:(t + 1) * tq, :] * scale
                zero = jnp.zeros_like(q)
                qh[t] = (jnp.where(is_a, q, zero), jnp.where(is_a, zero, q))
                carry[t] = [jnp.zeros((tq, 1), F32), jnp.zeros((tq, 1), F32)]
                acc[t] = [jnp.zeros((tq, LANES), F32), jnp.zeros((tq, LANES), F32)]
            z[k] = _dot_nt(qh[t][hd], k_ref[j * bk:(j + 1) * bk, :])
        if 0 <= k - 1 < n:
            c = k - 1
            t, j, hd = chains[c]
            e = jnp.exp2(jnp.abs(z[c]) * (-LOG2E))
            log_b[c] = jnp.minimum(z[c], 0.0) - jnp.log(1.0 + e)
            m = log_b[c] - z.pop(c)
            log_1m[c] = jnp.where(past, m, 0.0) if j == t else m
            within[c] = _dot(log_1m[c].astype(BF16), tri)
        if 0 <= k - 2 < n:
            c = k - 2
            t, j, hd = chains[c]
            wc = jnp.exp(log_b.pop(c) + (carry[t][hd] + within.pop(c)))
            if j == t:
                wc = jnp.where(past, wc, 0.0)
            carry[t][hd] = carry[t][hd] + jnp.sum(log_1m.pop(c), axis=1, keepdims=True)
            acc[t][hd] = acc[t][hd] + _dot(wc.astype(BF16), v_ref[j * bk:(j + 1) * bk, :])
            if j == 0 and hd == 1:
                o_ref[t * tq:(t + 1) * tq, :] = jnp.where(is_a, acc[t][0], acc[t][1]).astype(o_ref.dtype)
                del qh[t], carry[t], acc[t]


def _sb_attention(qkv, B, S, n_heads):
    T = B * S
    tq, bk = ATT_TQ, ATT_BK
    nq = S // tq
    n_pairs = n_heads * HEAD_DIM // LANES
    kern = functools.partial(_sb_kernel, tq=tq, bk=bk, scale=HEAD_DIM ** -0.5, q_tiles=nq)
    return pl.pallas_call(
        kern,
        grid=(B, n_pairs),
        in_specs=[
            pl.BlockSpec((S, LANES), lambda b, p: (b, p)),
            pl.BlockSpec((S, LANES), lambda b, p: (b, n_pairs + p)),
            pl.BlockSpec((S, LANES), lambda b, p: (b, 2 * n_pairs + p)),
        ],
        out_specs=pl.BlockSpec((S, LANES), lambda b, p: (b, p)),
        out_shape=jax.ShapeDtypeStruct((T, n_heads * HEAD_DIM), BF16),
        compiler_params=_params("parallel", "parallel"),
        name="sb_attn",
    )(qkv, qkv, qkv)


def _head_rmsnorm(x, bd, gain):
    ms = _dot_split(x * x, bd)
    return x * lax.rsqrt(ms + EPS) * gain


def _rope(y, c, sa, sb):
    w = y.shape[-1]
    half = ROPE_DIMS // 2
    return y * c + pltpu.roll(y, w - half, 1) * sa + pltpu.roll(y, half, 1) * sb


def _rope_coeffs(positions, n_heads):
    inv_freq = jnp.power(ROPE_THETA, -jnp.arange(0, ROPE_DIMS, 2, dtype=F32) / ROPE_DIMS)
    ang = positions.astype(F32)[:, None] * inv_freq[None, :]
    cos, sin = jnp.cos(ang), jnp.sin(ang)
    p = positions.shape[0]
    rest = HEAD_DIM - ROPE_DIMS
    zeros_h = jnp.zeros((p, ROPE_DIMS // 2), F32)
    c = jnp.concatenate([cos, cos, jnp.ones((p, rest), F32)], axis=-1)
    sa = jnp.concatenate([-sin, zeros_h, jnp.zeros((p, rest), F32)], axis=-1)
    sb = jnp.concatenate([zeros_h, sin, jnp.zeros((p, rest), F32)], axis=-1)
    return tuple(jnp.tile(t, (1, n_heads)) for t in (c, sa, sb))


def _block_mean_matrix(width):
    idx = np.arange(width) // HEAD_DIM
    return jnp.asarray((idx[:, None] == idx[None, :]).astype(np.float32) / HEAD_DIM, dtype=BF16)


def _kv_prep_kernel(sel_ref, win_ref, bd_ref, gs_ref, gw_ref, c_ref, sa_ref, sb_ref, o_ref):
    w = GROUP_LANES
    bd = bd_ref[...]
    c, sa, sb = c_ref[...], sa_ref[...], sb_ref[...]
    ks = _rope(_head_rmsnorm(sel_ref[:, :w], bd, gs_ref[...]), c, sa, sb)
    kw = _rope(_head_rmsnorm(win_ref[:, :w], bd, gw_ref[...]), c, sa, sb)
    o_ref[:, 0 * w:1 * w] = ks.astype(BF16)
    o_ref[:, 1 * w:2 * w] = sel_ref[:, w:].astype(BF16)
    o_ref[:, 2 * w:3 * w] = kw.astype(BF16)
    o_ref[:, 3 * w:4 * w] = win_ref[:, w:].astype(BF16)


def _kv_prep(kv, k_norm, coeffs, S):
    T = kv.shape[0]
    w = GROUP_LANES
    tm = min(PROJ_TM, S)
    ns = S // tm
    gs = jnp.tile(k_norm[1], NSA_KV_HEADS).reshape(1, w)
    gw = jnp.tile(k_norm[2], NSA_KV_HEADS).reshape(1, w)
    tab = pl.BlockSpec((tm, w), lambda i: (i % ns, 0))
    one = pl.BlockSpec((1, w), lambda i: (0, 0))
    return pl.pallas_call(
        _kv_prep_kernel,
        grid=(T // tm,),
        in_specs=[
            pl.BlockSpec((tm, 2 * w), lambda i: (i, 1)),
            pl.BlockSpec((tm, 2 * w), lambda i: (i, 2)),
            pl.BlockSpec((w, w), lambda i: (0, 0)),
            one, one, tab, tab, tab,
        ],
        out_specs=pl.BlockSpec((tm, 4 * w), lambda i: (i, 0)),
        out_shape=jax.ShapeDtypeStruct((T, 4 * w), BF16),
        compiler_params=_params("parallel"),
        name="kv_prep",
    )(kv, kv, _block_mean_matrix(w), gs, gw, *coeffs)


def _compress_kernel(x_ref, pos_ref, w1_ref, w2_ref, bd_ref, g_ref, c_ref, sa_ref, sb_ref,
                     o_ref, top_ref, bot_ref, wbd_ref):
    t = pl.program_id(0)
    l = pl.program_id(1)
    half = CMP_BLOCK // 2
    hid = w1_ref.shape[-1]

    @pl.when(l == 0)
    def _():
        top_ref[...] = jnp.zeros_like(top_ref)
        bot_ref[...] = jnp.zeros_like(bot_ref)
        wbd_ref[...] = jnp.zeros_like(wbd_ref)

    for u in range(2):
        for g in range(NSA_KV_HEADS):
            wbd_ref[u, g * HEAD_DIM:(g + 1) * HEAD_DIM, g * hid:(g + 1) * hid] = w1_ref[0, u]

    x = x_ref[0]
    top_ref[...] += _dot((x + pos_ref[0, 0]).astype(BF16), wbd_ref[0])
    bot_ref[...] += _dot((x + pos_ref[0, 1]).astype(BF16), wbd_ref[1])

    @pl.when(l == half - 1)
    def _():
        rows = bot_ref.shape[0]
        hidden = top_ref[...] + pltpu.roll(bot_ref[...], rows - 1, 0)
        act = (hidden * _sigmoid(hidden)).astype(BF16)
        y = _dot(act, w2_ref[0])
        yk = _rope(_head_rmsnorm(y, bd_ref[...], g_ref[...]), c_ref[...], sa_ref[...], sb_ref[...])
        o_ref[0] = jnp.where(t == 0, yk, y).astype(BF16)


def _compress(kv, B, S, pos_k, pos_v, k_w1, k_w2, v_w1, v_w2, k_gain):
    w = GROUP_LANES
    G = NSA_KV_HEADS
    half = CMP_BLOCK // 2
    nc = S // CMP_STRIDE
    rows = B * nc
    hid = k_w1.shape[1]
    raw = kv[:, :2 * w].reshape(B, nc, half, 2, w)
    xc = jnp.transpose(raw, (3, 2, 0, 1, 4)).reshape(2, half, rows, w)
    eye = jnp.eye(G, dtype=F32)

    def blockdiag(wm):
        out = jnp.einsum('gh,...ab->...gahb', eye, wm)
        return out.reshape(*wm.shape[:-2], G * wm.shape[-2], G * wm.shape[-1])

    def prep_w1(w1):
        return jnp.transpose(w1.reshape(2, half, HEAD_DIM, hid), (1, 0, 2, 3))

    w1 = jnp.stack([prep_w1(k_w1), prep_w1(v_w1)]).astype(BF16)
    w2 = jnp.stack([blockdiag(k_w2), blockdiag(v_w2)]).astype(BF16)

    def prep_pos(p):
        p = jnp.tile(p, (1, G)).reshape(2, half, 1, w)
        return jnp.transpose(p, (1, 0, 2, 3))

    pos = jnp.stack([prep_pos(pos_k), prep_pos(pos_v)])
    c_end = jnp.arange(nc) * CMP_STRIDE + (CMP_BLOCK - 1)
    coeffs = tuple(jnp.tile(tb, (B, 1)) for tb in _rope_coeffs(c_end, G))
    gain = jnp.tile(k_gain, G).reshape(1, w)
    const2 = lambda t, l: (0, 0)
    out = pl.pallas_call(
        _compress_kernel,
        grid=(2, half),
        in_specs=[
            pl.BlockSpec((None, 1, rows, w), lambda t, l: (t, l, 0, 0)),
            pl.BlockSpec((None, 1, 2, 1, w), lambda t, l: (t, l, 0, 0, 0)),
            pl.BlockSpec((None, 1, 2, HEAD_DIM, hid), lambda t, l: (t, l, 0, 0, 0)),
            pl.BlockSpec((1, G * hid, w), lambda t, l: (t, 0, 0)),
            pl.BlockSpec((w, w), const2),
            pl.BlockSpec((1, w), const2),
            pl.BlockSpec((rows, w), const2),
            pl.BlockSpec((rows, w), const2),
            pl.BlockSpec((rows, w), const2),
        ],
        out_specs=pl.BlockSpec((1, rows, w), lambda t, l: (t, 0, 0)),
        out_shape=jax.ShapeDtypeStruct((2, rows, w), BF16),
        scratch_shapes=[pltpu.VMEM((rows, G * hid), F32), pltpu.VMEM((rows, G * hid), F32),
                        pltpu.VMEM((2, w, G * hid), BF16)],
        compiler_params=_params("arbitrary", "arbitrary"),
        name="nsa_compress",
    )(xc, pos, w1, w2, _block_mean_matrix(w), gain, *coeffs)
    return out[0], out[1]


MASK_BIG = 2.0 ** 30
V_ROWS = HEAD_DIM + 16


def _nsa_kernel(q_ref, gl_ref, kc_ref, vc_ref, ks_ref, vs_ref, kw_ref, vw_ref,
                bd_ref, qg_ref, c_ref, sa_ref, sb_ref, ov_ref,
                o_ref,
                kc_a, ks_a, kw_a, vct, vst, vwt, s_scr,
                *, tq, bk, nsb, scale, q_tiles):
    g = pl.program_id(1)
    w = GROUP_LANES
    R = NSA_GROUP
    rows = R * tq
    lane = _iota((1, w), 1)
    n_tiles = ks_a.shape[0] // bk
    ncp = kc_a.shape[0]

    def stage_kv():
        rr, cc = _iota((w, w), 0), _iota((w, w), 1)
        mine = ((rr >> 6) == g) & ((rr & 63) == (cc & 63))
        to_slot0 = jnp.where(mine & (cc < HEAD_DIM), 1.0, 0.0).astype(BF16)
        kc_a[...] = _dot(kc_ref[...], to_slot0).astype(BF16)
        kw_a[...] = _dot(kw_ref[...], to_slot0).astype(BF16)
        seq = ks_a.shape[0]
        tag = jnp.where(_iota((seq, w), 1) - HEAD_DIM == (_iota((seq, w), 0) >> 6), 1.0, 0.0)
        ks_a[...] = (_dot(ks_ref[...], to_slot0) + tag).astype(BF16)
        pick = jnp.where(_iota((V_ROWS, w), 1) == g * HEAD_DIM + _iota((V_ROWS, w), 0), 1.0, 0.0)
        pick = jnp.where(_iota((V_ROWS, w), 0) < HEAD_DIM, pick, 0.0).astype(BF16)
        vct[...] = _dot_nt(pick[0:HEAD_DIM], vc_ref[...]).astype(BF16)
        ones_row = jnp.where(_iota((V_ROWS, bk), 0) == HEAD_DIM, 1.0, 0.0)
        for j in range(n_tiles):
            vst[j] = (_dot_nt(pick, vs_ref[j * bk:(j + 1) * bk, :]) + ones_row).astype(BF16)
            vwt[j] = (_dot_nt(pick, vw_ref[j * bk:(j + 1) * bk, :]) + ones_row).astype(BF16)

    def lanes4(x):
        return jnp.concatenate([x] * R, axis=1)

    def part_max(x):
        return jnp.max(x.reshape(x.shape[0] // 8, 8, rows), axis=0)

    def tile_front(i):
        tok = slice(i * tq, (i + 1) * tq)
        qn = _rope(_head_rmsnorm(q_ref[tok, :], bd_ref[...], qg_ref[...]),
                   c_ref[tok, :], sa_ref[tok, :], sb_ref[tok, :]) * scale
        slot0 = lane < HEAD_DIM
        parts = []
        for r in range(R):
            qr = qn if r == 0 else pltpu.roll(qn, w - r * HEAD_DIM, 1)
            parts.append(jnp.where(slot0, qr, 0.0))
        qs_f32 = jnp.concatenate(parts, axis=0)
        qs = qs_f32.astype(BF16)
        t_col = i * tq + (_iota((1, rows), 1) & (tq - 1))
        key_le_query = _iota((bk, tq), 0) <= _iota((bk, tq), 1)
        diag_bias = lanes4(jnp.where(key_le_query, 0.0, NEG))

        s = _dot_nt(kc_a[...], qs)

        win_blk = [j for j in (i - 2, i - 1, i) if j >= 0]
        win_sc = []
        for j in win_blk:
            sc = _dot_nt(kw_a[j * bk:(j + 1) * bk, :], qs)
            if j == i - 2:
                sc = sc + lanes4(jnp.where(key_le_query, NEG, 0.0))
            elif j == i:
                sc = sc + diag_bias
            win_sc.append(sc)

        c_end = _iota((ncp, 1), 0) * CMP_STRIDE + (CMP_BLOCK - 1)
        s = jnp.where(c_end <= t_col, s, NEG)
        e = jnp.exp(s - jnp.max(s, axis=0, keepdims=True))
        p = e / jnp.sum(e, axis=0, keepdims=True)
        p = jnp.where(t_col >= CMP_BLOCK - 1, p, 0.0)
        o_cmp = _dot(vct[...], p.astype(BF16))
        p_sum = p[:, 0:tq]
        for r in range(1, R):
            p_sum = p_sum + p[:, r * tq:(r + 1) * tq]

        hi, lo = _split_bf16(p_sum)
        ov = ov_ref[...]
        imp = (_dot(ov, hi) + _dot(ov, lo))[0:nsb]
        jb = _iota((nsb, 1), 0)
        cur = (i * tq + _iota((1, tq), 1)) >> 6
        forced = (jb == 0) | ((cur - jb >= 0) & (cur - jb < N_LOCAL_SEL))
        score = jnp.where(forced, FORCE, jnp.where(jb <= cur, imp, -FORCE))
        cnt = jnp.zeros((nsb, tq), F32)
        sub8 = _iota((8, 1), 0)
        for i2 in range(nsb):
            row = score[i2:i2 + 1, :]
            grp = []
            for r in range(nsb // 8):
                blk = score[8 * r:8 * r + 8]
                gt = jnp.where(row > blk, 1.0, 0.0)
                if 8 * r + 7 <= i2:
                    grp.append(gt)
                elif 8 * r > i2:
                    grp.append(jnp.where(row >= blk, 1.0, 0.0))
                else:
                    grp.append(jnp.where(sub8 > i2 - 8 * r, jnp.where(row >= blk, 1.0, 0.0), gt))
            cnt = cnt + jnp.concatenate(grp, axis=0)
        sel_t = jnp.where((cnt < float(min(N_SEL, nsb))) & (jb <= cur), 0.0, -MASK_BIG)
        pad_t = jnp.concatenate([jnp.zeros((HEAD_DIM, tq), F32), sel_t,
                                 jnp.zeros((LANES - HEAD_DIM - nsb, tq), F32)], axis=0)
        aug = jnp.concatenate([pad_t.T, jnp.zeros((tq, w - LANES), F32)], axis=1)
        qsel = (qs_f32 + jnp.concatenate([aug] * R, axis=0)).astype(BF16)
        return dict(tok=tok, qsel=qsel, win_sc=win_sc, win_blk=win_blk, o_cmp=o_cmp, diag_bias=diag_bias)

    def selected_scores(i, c):
        mx8 = None
        for x in range(i + 1):
            sc = _dot_nt(ks_a[x * bk:(x + 1) * bk, :], c["qsel"])
            if x == i:
                sc = sc + c["diag_bias"]
            s_scr[x] = sc
            mx8 = part_max(sc) if mx8 is None else jnp.maximum(mx8, part_max(sc))
        return jnp.max(mx8, axis=0, keepdims=True)

    def finish(i, c, m_s):
        tok, win_sc, win_blk, o_cmp = c["tok"], c["win_sc"], c["win_blk"], c["o_cmp"]
        m_w = win_sc[0]
        for sc in win_sc[1:]:
            m_w = jnp.maximum(m_w, sc)
        m_w = jnp.max(m_w, axis=0, keepdims=True)
        p_cat = jnp.concatenate([jnp.exp(sc - m_w).astype(BF16) for sc in win_sc], axis=0)
        v_cat = jnp.concatenate([vwt[j] for j in win_blk], axis=1)
        acc_w = _dot(v_cat, p_cat)
        o_win = acc_w[0:HEAD_DIM] / acc_w[HEAD_DIM:HEAD_DIM + 1]

        acc_s = None
        for x0 in range(0, i + 1, 2):
            xs = [x for x in (x0, x0 + 1) if x <= i]
            p2 = jnp.concatenate([jnp.exp(s_scr[x] - m_s).astype(BF16) for x in xs], axis=0)
            v2 = jnp.concatenate([vst[x] for x in xs], axis=1)
            d = _dot(v2, p2)
            acc_s = d if acc_s is None else acc_s + d
        o_sel = acc_s[0:HEAD_DIM] / acc_s[HEAD_DIM:HEAD_DIM + 1]

        def to_rows(o_t):
            return jnp.concatenate([o_t[:, r * tq:(r + 1) * tq] for r in range(R)], axis=0)

        ghi, glo = _split_bf16(_sigmoid(gl_ref[tok, :]))
        row_head = _iota((w, LANES), 0) >> 6
        col = _iota((w, LANES), 1)
        out_t = jnp.zeros((w, tq), F32)
        for br, o_br in enumerate((o_cmp, o_sel, o_win)):
            spread = jnp.where(col == br * (NSA_KV_HEADS * R) + g * R + row_head, 1.0, 0.0).astype(BF16)
            out_t = out_t + (_dot_nt(spread, ghi) + _dot_nt(spread, glo)) * to_rows(o_br)
        o_ref[tok, :] = out_t.T.astype(o_ref.dtype)

    stage_kv()
    ctx = tile_front(0)
    for n in range(q_tiles):
        m_s = selected_scores(n, ctx)
        nxt = tile_front(n + 1) if n + 1 < q_tiles else None
        finish(n, ctx, m_s)
        ctx = nxt


def _nsa_attention(proj, kc, vc, ksw, q_gain, coeffs, B, S):
    T = B * S
    w = GROUP_LANES
    G = NSA_KV_HEADS
    tq, bk = ATT_TQ, ATT_BK
    nq = S // tq
    nc = S // CMP_STRIDE
    nsb = S // SEL_BLOCK
    n_cmp = (S - CMP_BLOCK) // CMP_STRIDE + 1
    cs = np.arange(nc) * CMP_STRIDE
    js = np.arange(LANES) * SEL_BLOCK
    ov = ((cs[None, :] < js[:, None] + SEL_BLOCK) & (cs[None, :] + CMP_BLOCK > js[:, None])
          & (np.arange(nc)[None, :] < n_cmp) & (np.arange(LANES)[:, None] < nsb))
    ov = jnp.asarray(ov.astype(np.float32), dtype=BF16)
    gate_blk = (G * NSA_GROUP * HEAD_DIM) // LANES
    assert tq == bk and WINDOW == 2 * bk and HEAD_DIM + nsb <= LANES
    kern = functools.partial(_nsa_kernel, tq=tq, bk=bk, nsb=nsb, scale=HEAD_DIM ** -0.5, q_tiles=nq)
    seq = lambda col: pl.BlockSpec((S, w), lambda b, g: (b, col))
    const = lambda shape: pl.BlockSpec(shape, lambda b, g: (0, 0), pipeline_mode=pl.Buffered(1))
    rows = NSA_GROUP * tq
    n_tiles = S // bk
    return pl.pallas_call(
        kern,
        grid=(B, G),
        in_specs=[
            pl.BlockSpec((S, w), lambda b, g: (b, g)),
            pl.BlockSpec((S, LANES), lambda b, g: (b, gate_blk)),
            pl.BlockSpec((nc, w), lambda b, g: (b, 0)),
            pl.BlockSpec((nc, w), lambda b, g: (b, 0)),
            seq(0), seq(1), seq(2), seq(3),
            const((w, w)), const((1, w)), const((S, w)), const((S, w)), const((S, w)), const((LANES, nc)),
        ],
        out_specs=pl.BlockSpec((S, w), lambda b, g: (b, g)),
        out_shape=jax.ShapeDtypeStruct((T, G * w), BF16),
        scratch_shapes=[
            pltpu.VMEM((nc, w), BF16), pltpu.VMEM((S, w), BF16), pltpu.VMEM((S, w), BF16),
            pltpu.VMEM((HEAD_DIM, nc), BF16),
            pltpu.VMEM((n_tiles, V_ROWS, bk), BF16), pltpu.VMEM((n_tiles, V_ROWS, bk), BF16),
            pltpu.VMEM((n_tiles, bk, rows), F32),
        ],
        compiler_params=_params("parallel", "parallel"),
        name="nsa_attn",
    )(proj, proj, kc, vc, ksw, ksw, ksw, ksw,
      _block_mean_matrix(w), jnp.tile(q_gain, NSA_GROUP).reshape(1, w), *coeffs, ov)


def kernel(x, ffn1_norm, ffn1_w_in, ffn1_w_out, mix_norm, ffn2_norm, ffn2_w_in, ffn2_w_out,
           sb_w_qkv, sb_w_out, kv_norm, nsa_w_kv, nsa_k_norm, cmp_pos_k, cmp_pos_v,
           cmp_k_w1, cmp_k_w2, cmp_v_w1, cmp_v_w2, nsa_w_in, nsa_q_norm, nsa_w_out):
    B, S, D = x.shape
    depth = ffn1_norm.shape[0]
    n_a = sb_w_qkv.shape[0]
    n_sb_heads = sb_w_out.shape[1] // HEAD_DIM
    assert S % 512 == 0 and S // CMP_STRIDE == LANES, "sequence tiling assumes 16*128 tokens"
    h = x.reshape(B * S, D)
    bf = lambda a: a.astype(BF16)
    coeffs = _rope_coeffs(jnp.arange(S), NSA_KV_HEADS)
    n_q = nsa_w_in.shape[-1]
    n_q_pad = -(-n_q // LANES) * LANES
    kc = vc = ksw = None
    w1_in, w1_out, w2_in, w2_out = bf(ffn1_w_in), bf(ffn1_w_out), bf(ffn2_w_in), bf(ffn2_w_out)
    w_qkv, w_sb_out, w_nsa_out = bf(sb_w_qkv), bf(sb_w_out), bf(nsa_w_out)
    w_nsa_in = jnp.pad(bf(nsa_w_in), ((0, 0), (0, 0), (0, n_q_pad - n_q)))
    w_kv = bf(nsa_w_kv)[None]
    for layer in range(depth):
        h = _ffn(h, ffn1_norm[layer], w1_in, w1_out, layer)
        if layer < n_a:
            qkv = _norm_proj(h, mix_norm[layer], w_qkv, layer, 1024, BF16)
            mixer = (_sb_attention(qkv, B, S, n_sb_heads), w_sb_out, layer)
        else:
            li = layer - n_a
            proj = _norm_proj(h, mix_norm[layer], w_nsa_in, li, n_q_pad, F32)
            mixer = (_nsa_attention(proj, kc, vc, ksw, nsa_q_norm[li], coeffs, B, S), w_nsa_out, li)
        h = _ffn(h, ffn2_norm[layer], w2_in, w2_out, layer, mixer=mixer)
        if layer == n_a - 1:
            kv = _norm_proj(h, kv_norm, w_kv, 0, 768, F32)
            kc, vc = _compress(kv, B, S, cmp_pos_k, cmp_pos_v, cmp_k_w1, cmp_k_w2,
                               cmp_v_w1, cmp_v_w2, nsa_k_norm[0])
            ksw = _kv_prep(kv, nsa_k_norm, coeffs, S)
    return h.reshape(B, S, D)
```
